```python
import jax
import jax.numpy as jnp
from jax import lax
import numpy as np

D_MODEL = 1024
BATCH = 16
SEQ = 256
DEPTH = 2
DEC_BATCH = 8
DEC_SEQ = 2048
PAST_LEN = 256

GRID_W = 64
N_MOD = 6
RMS_EPS = 1e-6
CONV_W = 512
CONV_K = 3
N_HEADS = 8
N_KV_HEADS = 2
GQA_GROUP = N_HEADS // N_KV_HEADS
HEAD_DIM = 64
ATT_W = N_HEADS * HEAD_DIM
KV_W = N_KV_HEADS * HEAD_DIM
WINDOW = 128
Q_BLOCK = 128
ATTN_SCALE = HEAD_DIM ** -0.5
ROPE_THETA = 10000.0
RWKV_HEADS = 8
RWKV_HEAD = 64
RWKV_W = RWKV_HEADS * RWKV_HEAD
DECAY_LORA = 64
ICLR_LORA = 64
GATE_LORA = 128
GN_EPS = 64e-5
N_EXPERTS = 32
TOP_K = 4
D_EXPERT = 1024
SWIGLU_LIMIT = 7.0
SWIGLU_ALPHA = 1.702
MOE_BLOCK = 128
PROJ_SPLITS = (CONV_W, CONV_W, CONV_W, ATT_W, KV_W, KV_W, RWKV_W, RWKV_W, RWKV_W, DECAY_LORA, DECAY_LORA, ICLR_LORA, ICLR_LORA, GATE_LORA, D_MODEL, D_MODEL, D_MODEL)
P_TOTAL = 3 * CONV_W + ATT_W + 2 * KV_W + 3 * RWKV_W + 2 * DECAY_LORA + 2 * ICLR_LORA + GATE_LORA + 3 * D_MODEL

kernel_name = "hybrid_conv_swa_rwkv7_moe_dit_step"


def _rmsnorm(x, g):
    xf = x.astype(jnp.float32)
    y = xf * lax.rsqrt(jnp.mean(xf * xf, axis=-1, keepdims=True) + RMS_EPS)
    return (y * g.astype(jnp.float32)).astype(x.dtype)


def _short_conv(xc, bg, cg, w):
    u = cg * xc
    up = jnp.pad(u, ((0, 0), (1, 1), (0, 0)))
    conv = w[0] * up[:, :-2] + w[1] * up[:, 1:-1] + w[2] * up[:, 2:]
    return bg * conv


def _axial_rope(x):
    T = x.shape[1]
    rows = T // GRID_W
    row_pos = jnp.repeat(jnp.arange(rows, dtype=jnp.float32), GRID_W)
    col_pos = jnp.tile(jnp.arange(GRID_W, dtype=jnp.float32), rows)
    half = HEAD_DIM // 2
    inv_freq = 1.0 / (ROPE_THETA ** (jnp.arange(0, half, 2, dtype=jnp.float32) / half))
    xf = x.astype(jnp.float32)

    def rot(xa, pos):
        ang = pos[:, None] * inv_freq[None, :]
        cos = jnp.cos(ang)[None, :, None, :]
        sin = jnp.sin(ang)[None, :, None, :]
        x1, x2 = xa[..., : half // 2], xa[..., half // 2:]
        return jnp.concatenate([x1 * cos - x2 * sin, x1 * sin + x2 * cos], axis=-1)

    out = jnp.concatenate([rot(xf[..., :half], row_pos), rot(xf[..., half:], col_pos)], axis=-1)
    return out.astype(x.dtype)


def _sink_attend(q_blk, keys, vals, masks, sink):
    B, Q = q_blk.shape[0], q_blk.shape[1]
    qg = q_blk.reshape(B, Q, N_KV_HEADS, GQA_GROUP, HEAD_DIM)
    logits = []
    for kb, m in zip(keys, masks):
        s = jnp.einsum("bqkgd,bskd->bkgqs", qg, kb).astype(jnp.float32) * ATTN_SCALE
        if m is not None:
            s = jnp.where(m, s, -jnp.inf)
        logits.append(s)
    sink_col = jnp.broadcast_to(sink.astype(jnp.float32).reshape(N_KV_HEADS, GQA_GROUP)[None, :, :, None, None], (B, N_KV_HEADS, GQA_GROUP, Q, 1))
    probs = jax.nn.softmax(jnp.concatenate(logits + [sink_col], axis=-1), axis=-1)
    out = None
    start = 0
    for vb in vals:
        n = vb.shape[1]
        pv = jnp.einsum("bkgqs,bskd->bqkgd", probs[..., start:start + n].astype(vb.dtype), vb)
        out = pv if out is None else out + pv
        start += n
    return out.reshape(B, Q, N_HEADS, HEAD_DIM)


def _attn_context(q, k, v, sink):
    B, L = q.shape[0], q.shape[1]
    nb = L // Q_BLOCK
    qb = jnp.moveaxis(q.reshape(B, nb, Q_BLOCK, N_HEADS, HEAD_DIM), 1, 0)
    out = lax.map(lambda qq: _sink_attend(qq, [k], [v], [None], sink), qb)
    return jnp.moveaxis(out, 0, 1).reshape(B, L, ATT_W)


def _attn_latent(q, k, v, k_ctx, v_ctx, sink):
    B, T = q.shape[0], q.shape[1]
    nb = T // Q_BLOCK
    span = Q_BLOCK + 2 * WINDOW
    kp = jnp.pad(k, ((0, 0), (WINDOW, WINDOW), (0, 0), (0, 0)))
    vp = jnp.pad(v, ((0, 0), (WINDOW, WINDOW), (0, 0), (0, 0)))
    qi = jnp.arange(Q_BLOCK)[:, None]
    kj = jnp.arange(span)[None, :]
    band = jnp.abs(qi + WINDOW - kj) <= WINDOW

    def block(b):
        qq = lax.dynamic_slice_in_dim(q, b * Q_BLOCK, Q_BLOCK, axis=1)
        kw = lax.dynamic_slice_in_dim(kp, b * Q_BLOCK, span, axis=1)
        vw = lax.dynamic_slice_in_dim(vp, b * Q_BLOCK, span, axis=1)
        jpos = b * Q_BLOCK - WINDOW + kj
        mask = band & (jpos >= 0) & (jpos < T)
        return _sink_attend(qq, [kw, k_ctx], [vw, v_ctx], [mask, None], sink)

    out = lax.map(block, jnp.arange(nb))
    return jnp.moveaxis(out, 0, 1).reshape(B, T, ATT_W)


def _wkv_scan(s0, r, w, k, v, kk, a, reverse):
    xs = tuple(jnp.moveaxis(t, 1, 0) for t in (r, w, k, v, kk, a))

    def step(S, inp):
        r_t, w_t, k_t, v_t, kk_t, a_t = inp
        s_kk = jnp.einsum("bhij,bhj->bhi", S, kk_t)
        S = S * w_t[:, :, None, :] - s_kk[..., None] * (kk_t * a_t)[:, :, None, :] + v_t[..., None] * k_t[:, :, None, :]
        return S, jnp.einsum("bhij,bhj->bhi", S, r_t)

    s_fin, out = lax.scan(step, s0, xs, reverse=reverse)
    return s_fin, jnp.moveaxis(out, 0, 1)


def _rwkv(rr, rk, rv, w_lo, a_lo, g_lo, p, s_f0, s_b0):
    B, L = rr.shape[0], rr.shape[1]
    f32 = jnp.float32

    def heads(t):
        return t.astype(f32).reshape(B, L, RWKV_HEADS, RWKV_HEAD)

    r = heads(rr)
    v = heads(rv)
    k_raw = rk.astype(f32)
    kk = heads(k_raw * p["rwkv_k_k"].astype(f32))
    kk = kk / jnp.maximum(jnp.sqrt(jnp.sum(kk * kk, axis=-1, keepdims=True)), 1e-12)
    r_k = p["rwkv_r_k"].astype(f32).reshape(RWKV_HEADS, RWKV_HEAD)
    outs, bonuses, finals = [], [], []
    for d, (wl, al, s0, rev) in enumerate(((w_lo[0], a_lo[0], s_f0, False), (w_lo[1], a_lo[1], s_b0, True))):
        logw = -jax.nn.softplus(-(p["rwkv_w0"][d] + jnp.tanh(wl) @ p["rwkv_w2"][d]).astype(f32)) - 0.5
        decay = jnp.exp(-jnp.exp(logw))
        a = jax.nn.sigmoid((p["rwkv_a0"][d] + al @ p["rwkv_a2"][d]).astype(f32))
        k = heads(k_raw * (1.0 + (a - 1.0) * p["rwkv_k_a"].astype(f32)))
        s_fin, o = _wkv_scan(s0, r, heads(decay), k, v, kk, heads(a), rev)
        outs.append(o)
        bonuses.append(jnp.sum(r * k * r_k, axis=-1, keepdims=True) * v)
        finals.append(s_fin)
    o = outs[0] + outs[1]
    mu = jnp.mean(o, axis=-1, keepdims=True)
    var = jnp.mean(jnp.square(o - mu), axis=-1, keepdims=True)
    o = ((o - mu) * lax.rsqrt(var + GN_EPS)).reshape(B, L, RWKV_W) * p["rwkv_ln_g"].astype(f32) + p["rwkv_ln_b"].astype(f32)
    o = o + (bonuses[0] + bonuses[1]).reshape(B, L, RWKV_W)
    g = (jax.nn.sigmoid(g_lo) @ p["rwkv_g2"]).astype(f32)
    return (o * g).astype(rr.dtype), finals[0], finals[1]


def _moe(h, router_w, router_b, w1, b1, w2, b2):
    B, L, D = h.shape
    T = B * L
    x = h.reshape(T, D)
    logits = (x @ router_w + router_b).astype(jnp.float32)
    top_v, top_e = lax.top_k(logits, TOP_K)
    gates = jax.nn.softmax(top_v, axis=-1).reshape(-1)
    e_flat = top_e.reshape(-1).astype(jnp.int32)
    tok_flat = jnp.repeat(jnp.arange(T, dtype=jnp.int32), TOP_K)
    n_assign = T * TOP_K
    order = jnp.argsort(e_flat)
    e_s, tok_s, gate_s = e_flat[order], tok_flat[order], gates[order]
    counts = jnp.bincount(e_flat, length=N_EXPERTS).astype(jnp.int32)
    starts = jnp.cumsum(counts) - counts
    padded = (counts + MOE_BLOCK - 1) // MOE_BLOCK * MOE_BLOCK
    pends = jnp.cumsum(padded)
    pstarts = pends - padded
    dest = pstarts[e_s] + jnp.arange(n_assign, dtype=jnp.int32) - starts[e_s]
    n_blk = -(-(n_assign + N_EXPERTS * (MOE_BLOCK - 1)) // MOE_BLOCK)
    rows = jnp.zeros((n_blk * MOE_BLOCK, D), x.dtype).at[dest].set(x[tok_s])
    blk_e = jnp.minimum(jnp.searchsorted(pends, jnp.arange(n_blk, dtype=jnp.int32) * MOE_BLOCK, side="right"), N_EXPERTS - 1)

    def expert_block(args):
        xb, e = args
        hm = xb @ w1[e] + b1[e]
        glu = jnp.minimum(hm[:, :D_EXPERT], SWIGLU_LIMIT)
        lin = jnp.clip(hm[:, D_EXPERT:], -SWIGLU_LIMIT, SWIGLU_LIMIT)
        return (glu * jax.nn.sigmoid(SWIGLU_ALPHA * glu) * (lin + 1.0)) @ w2[e] + b2[e]

    y_rows = lax.map(expert_block, (rows.reshape(n_blk, MOE_BLOCK, D), blk_e)).reshape(n_blk * MOE_BLOCK, D)
    out = jnp.zeros((T, D), x.dtype).at[tok_s].add(gate_s[:, None].astype(x.dtype) * y_rows[dest])
    return out.reshape(B, L, D)


def _layer(x, mod, p, ctx):
    B, L, _ = x.shape
    sh1, sc1, gt1, sh2, sc2, gt2 = jnp.split(mod, N_MOD, axis=-1)
    h = _rmsnorm(x, p["g_norm1"]) * (1.0 + sc1) + sh1
    z = h @ p["w_in"]
    (cx, cb, cc, q, k, v, rr, rk, rv, wf, wb, af, ab, g1, ga, gb, gc) = jnp.split(z, np.cumsum(PROJ_SPLITS)[:-1].tolist(), axis=-1)
    y_conv = _short_conv(cx, cb, cc, p["conv_w"])
    q = q.reshape(B, L, N_HEADS, HEAD_DIM)
    k = k.reshape(B, L, N_KV_HEADS, HEAD_DIM)
    v = v.reshape(B, L, N_KV_HEADS, HEAD_DIM)
    if ctx is None:
        y_attn = _attn_context(q, k, v, p["attn_sink"])
        s_f0 = jnp.zeros((B, RWKV_HEADS, RWKV_HEAD, RWKV_HEAD), jnp.float32)
        s_b0 = s_f0
    else:
        k_ctx, v_ctx, s_f0, s_b0 = ctx
        y_attn = _attn_latent(_axial_rope(q), _axial_rope(k), v, k_ctx, v_ctx, p["attn_sink"])
    y_rwkv, s_f, s_b = _rwkv(rr, rk, rv, (wf, wb), (af, ab), g1, p, s_f0.astype(jnp.float32), s_b0.astype(jnp.float32))
    merged = (jax.nn.sigmoid(ga) * (y_conv @ p["w_branch_conv"])
              + jax.nn.sigmoid(gb) * (y_attn @ p["w_branch_attn"])
              + jax.nn.sigmoid(gc) * (y_rwkv @ p["w_branch_rwkv"]))
    x = x + gt1 * (merged @ p["w_out"])
    h2 = _rmsnorm(x, p["g_norm2"]) * (1.0 + sc2) + sh2
    x = x + gt2 * _moe(h2, p["router_w"], p["router_b"], p["moe_w1"], p["moe_b1"], p["moe_w2"], p["moe_b2"])
    return x, (k, v, s_f, s_b)


def setup_inputs(seed: int = 0) -> dict:
    key = jax.random.key(seed)
    ks = jax.random.split(key, 40)
    D = D_MODEL

    def nrm(i, shape, scale):
        return jax.random.normal(ks[i], shape, jnp.float32) * scale

    return {
        "x_prompt": nrm(0, (BATCH, SEQ, D), 1.0),
        "x_sample": nrm(1, (DEC_BATCH, DEC_SEQ, D), 1.0),
        "cache_k": nrm(2, (DEC_BATCH, DEPTH, PAST_LEN, N_KV_HEADS, HEAD_DIM), 1.0),
        "cache_v": nrm(3, (DEC_BATCH, DEPTH, PAST_LEN, N_KV_HEADS, HEAD_DIM), 1.0),
        "state_rwkv_fwd": nrm(4, (DEC_BATCH, DEPTH, RWKV_HEADS, RWKV_HEAD, RWKV_HEAD), 0.3),
        "state_rwkv_bwd": nrm(5, (DEC_BATCH, DEPTH, RWKV_HEADS, RWKV_HEAD, RWKV_HEAD), 0.3),
        "c": nrm(6, (DEC_BATCH, D), 1.0),
        "c_ctx": nrm(7, (D,), 1.0),
        "w_ada": nrm(8, (DEPTH, D, N_MOD * D), 0.5 * D ** -0.5),
        "b_ada": nrm(9, (DEPTH, N_MOD * D), 0.02),
        "g_norm1": 1.0 + nrm(10, (DEPTH, D), 0.02),
        "g_norm2": 1.0 + nrm(11, (DEPTH, D), 0.02),
        "w_in": nrm(12, (DEPTH, D, P_TOTAL), D ** -0.5),
        "conv_w": nrm(13, (DEPTH, CONV_K, CONV_W), CONV_K ** -0.5),
        "attn_sink": nrm(14, (DEPTH, N_HEADS), 0.5),
        "rwkv_w0": nrm(15, (DEPTH, 2, RWKV_W), 1.0),
        "rwkv_w2": nrm(16, (DEPTH, 2, DECAY_LORA, RWKV_W), 0.1 * DECAY_LORA ** -0.5),
        "rwkv_a0": nrm(17, (DEPTH, 2, RWKV_W), 0.5),
        "rwkv_a2": nrm(18, (DEPTH, 2, ICLR_LORA, RWKV_W), 0.1 * ICLR_LORA ** -0.5),
        "rwkv_g2": nrm(19, (DEPTH, GATE_LORA, RWKV_W), GATE_LORA ** -0.5),
        "rwkv_k_k": 0.85 + nrm(20, (DEPTH, RWKV_W), 0.02),
        "rwkv_k_a": 1.0 + nrm(21, (DEPTH, RWKV_W), 0.02),
        "rwkv_r_k": nrm(22, (DEPTH, RWKV_W), 0.1),
        "rwkv_ln_g": 1.0 + nrm(23, (DEPTH, RWKV_W), 0.02),
        "rwkv_ln_b": nrm(24, (DEPTH, RWKV_W), 0.02),
        "w_branch_conv": nrm(25, (DEPTH, CONV_W, D), CONV_W ** -0.5),
        "w_branch_attn": nrm(26, (DEPTH, ATT_W, D), ATT_W ** -0.5),
        "w_branch_rwkv": nrm(27, (DEPTH, RWKV_W, D), RWKV_W ** -0.5),
        "w_out": nrm(28, (DEPTH, D, D), D ** -0.5),
        "router_w": nrm(29, (DEPTH, D, N_EXPERTS), D ** -0.5),
        "router_b": nrm(30, (DEPTH, N_EXPERTS), 0.01),
        "moe_w1": nrm(31, (DEPTH, N_EXPERTS, D, 2 * D_EXPERT), D ** -0.5),
        "moe_b1": nrm(32, (DEPTH, N_EXPERTS, 2 * D_EXPERT), 0.01),
        "moe_w2": nrm(33, (DEPTH, N_EXPERTS, D_EXPERT, D), D_EXPERT ** -0.5),
        "moe_b2": nrm(34, (DEPTH, N_EXPERTS, D), 0.01),
        "g_final": 1.0 + nrm(35, (D,), 0.02),
    }


def reference(x_prompt, x_sample, cache_k, cache_v, state_rwkv_fwd, state_rwkv_bwd, c, c_ctx,
              w_ada, b_ada, g_norm1, g_norm2, w_in, conv_w, attn_sink,
              rwkv_w0, rwkv_w2, rwkv_a0, rwkv_a2, rwkv_g2, rwkv_k_k, rwkv_k_a, rwkv_r_k, rwkv_ln_g, rwkv_ln_b,
              w_branch_conv, w_branch_attn, w_branch_rwkv, w_out,
              router_w, router_b, moe_w1, moe_b1, moe_w2, moe_b2, g_final):
    def layer_params(l):
        return {
            "w_ada": w_ada[l], "b_ada": b_ada[l], "g_norm1": g_norm1[l], "g_norm2": g_norm2[l],
            "w_in": w_in[l], "conv_w": conv_w[l], "attn_sink": attn_sink[l],
            "rwkv_w0": rwkv_w0[l], "rwkv_w2": rwkv_w2[l], "rwkv_a0": rwkv_a0[l], "rwkv_a2": rwkv_a2[l],
            "rwkv_g2": rwkv_g2[l], "rwkv_k_k": rwkv_k_k[l], "rwkv_k_a": rwkv_k_a[l], "rwkv_r_k": rwkv_r_k[l],
            "rwkv_ln_g": rwkv_ln_g[l], "rwkv_ln_b": rwkv_ln_b[l],
            "w_branch_conv": w_branch_conv[l], "w_branch_attn": w_branch_attn[l], "w_branch_rwkv": w_branch_rwkv[l],
            "w_out": w_out[l], "router_w": router_w[l], "router_b": router_b[l],
            "moe_w1": moe_w1[l], "moe_b1": moe_b1[l], "moe_w2": moe_w2[l], "moe_b2": moe_b2[l],
        }

    x = x_prompt
    ks_l, vs_l, sf_l, sb_l = [], [], [], []
    for l in range(DEPTH):
        p = layer_params(l)
        mod = (jax.nn.silu(c_ctx) @ p["w_ada"] + p["b_ada"])[None, None, :]
        x, (k_l, v_l, s_f, s_b) = _layer(x, mod, p, None)
        ks_l.append(k_l)
        vs_l.append(v_l)
        sf_l.append(s_f.astype(x_prompt.dtype))
        sb_l.append(s_b.astype(x_prompt.dtype))
    y_prompt = _rmsnorm(x, g_final)
    new_cache_k = jnp.stack(ks_l, axis=1)
    new_cache_v = jnp.stack(vs_l, axis=1)
    new_state_rwkv_fwd = jnp.stack(sf_l, axis=1)
    new_state_rwkv_bwd = jnp.stack(sb_l, axis=1)

    x = x_sample
    for l in range(DEPTH):
        p = layer_params(l)
        mod = (jax.nn.silu(c) @ p["w_ada"] + p["b_ada"])[:, None, :]
        x, _ = _layer(x, mod, p, (cache_k[:, l], cache_v[:, l], state_rwkv_fwd[:, l], state_rwkv_bwd[:, l]))
    y_sample = _rmsnorm(x, g_final)
    return (y_prompt, y_sample, new_cache_k, new_cache_v, new_state_rwkv_fwd, new_state_rwkv_bwd)
```

```python
import functools

import numpy as np
import jax
import jax.numpy as jnp
from jax import lax
from jax.experimental import pallas as pl
from jax.experimental.pallas import tpu as pltpu

F32 = jnp.float32
BF16 = jnp.bfloat16

D_MODEL = 1024
N_MOD = 6
RMS_EPS = 1e-6
CONV_W = 512
N_HEADS = 8
N_KV_HEADS = 2
GQA_GROUP = N_HEADS // N_KV_HEADS
HEAD_DIM = 64
ATT_W = N_HEADS * HEAD_DIM
KV_W = N_KV_HEADS * HEAD_DIM
WINDOW = 128
Q_BLOCK = 128
ATTN_SCALE = HEAD_DIM ** -0.5
ROPE_THETA = 10000.0
GRID_W = 64
RWKV_HEAD = 64
RWKV_W = 512
DECAY_LORA = 64
ICLR_LORA = 64
GATE_LORA = 128
GN_EPS = 64e-5
N_EXPERTS = 32
TOP_K = 4
D_EXPERT = 1024
SWIGLU_LIMIT = 7.0
SWIGLU_ALPHA = 1.702
P_TOTAL = 7296

_SRC_OFF = dict(conv=0, qkv=1536, rwkv=2304, lora=3840, gates=4224)
COL_RWKV, COL_CONV, COL_GATES, COL_QKV, COL_LORA = 0, 1536, 3072, 6144, 6912
LORA_W = 2 * DECAY_LORA + 2 * ICLR_LORA + GATE_LORA

CHUNK = 64
PAIR = 2 * RWKV_HEAD
N_PAIR = RWKV_W // PAIR
NEG_BIG = -1e30
EXP_NEG_HALF = float(np.exp(-0.5))

TM_IN = 512
TN_IN = 2432
TM_MIX = 256
TM_ROUTE = 512
TM_MOE = 256
VMEM_LIMIT = 56 * 1024 * 1024


def _cparams(sem, vmem=VMEM_LIMIT):
    return pltpu.CompilerParams(dimension_semantics=sem, vmem_limit_bytes=vmem)


def _dot(a, b, dims=(((1,), (0,)), ((), ()))):
    return lax.dot_general(a, b, dims, preferred_element_type=F32)


_NT = (((1,), (1,)), ((), ()))
_TN = (((0,), (0,)), ((), ()))


def _split(x):
    hi = x.astype(BF16)
    lo = (x - hi.astype(F32)).astype(BF16)
    return hi, lo


def _dot3(a, b, dims=(((1,), (0,)), ((), ()))):
    ah, al = _split(a)
    bh, bl = _split(b)
    return _dot(ah, bh, dims) + (_dot(ah, bl, dims) + _dot(al, bh, dims))


def _dot_exact_rhs(a, b_bf16, dims=(((1,), (0,)), ((), ()))):
    ah, al = _split(a)
    return _dot(ah, b_bf16, dims) + _dot(al, b_bf16, dims)


def _sigmoid(x):
    return 1.0 / (1.0 + jnp.exp(-x))


def _rms_mod(x, g, scale, shift):
    ms = jnp.mean(x * x, axis=-1, keepdims=True)
    return (x * lax.rsqrt(ms + RMS_EPS) * g) * (1.0 + scale) + shift


def _ada_kernel(cond_ref, w_ref, b_ref, o_ref):
    c = cond_ref[...]
    s = c * _sigmoid(c)
    o_ref[...] = _dot(s.astype(BF16), w_ref[...].astype(BF16)) + b_ref[...]


def _ada_call(cond, w_ada, b_ada):
    depth, d, n = w_ada.shape
    r = cond.shape[0]
    tn = 1536
    return pl.pallas_call(
        _ada_kernel,
        grid=(depth, n // tn),
        in_specs=[
            pl.BlockSpec((r, d), lambda l, j: (0, 0)),
            pl.BlockSpec((None, d, tn), lambda l, j: (l, 0, j)),
            pl.BlockSpec((None, 1, tn), lambda l, j: (l, 0, j)),
        ],
        out_specs=pl.BlockSpec((None, r, tn), lambda l, j: (l, 0, j)),
        out_shape=jax.ShapeDtypeStruct((depth, r, n), F32),
        compiler_params=_cparams(("arbitrary", "arbitrary")),
        name="ada",
    )(cond, w_ada, b_ada.reshape(depth, 1, n))


def _in_kernel(has_res, *refs):
    if has_res:
        x_ref, moe_ref, modp_ref, mod_ref, g_ref, w_ref, z_ref, xo_ref, h_scr = refs
    else:
        x_ref, mod_ref, g_ref, w_ref, z_ref, h_scr = refs

    @pl.when(pl.program_id(1) == 0)
    def _():
        x = x_ref[...]
        if has_res:
            x = x + modp_ref[5:6, :] * moe_ref[...]
            xo_ref[...] = x
        h_scr[...] = _rms_mod(x, g_ref[...], mod_ref[1:2, :], mod_ref[0:1, :]).astype(BF16)

    z_ref[...] = _dot(h_scr[...], w_ref[...])


def _in_call(x, w_bf16, g_norm, mod, row_of_tile, res=None):
    t, d = x.shape
    n = w_bf16.shape[1]
    tm, tn = TM_IN, TN_IN
    mod_spec = pl.BlockSpec((None, N_MOD, d), lambda i, j: (row_of_tile(i, tm), 0, 0))
    x_spec = pl.BlockSpec((tm, d), lambda i, j: (i, 0))
    in_specs = [x_spec]
    args = [x]
    if res is not None:
        moe, mod_prev = res
        in_specs += [x_spec, mod_spec]
        args += [moe, mod_prev]
    in_specs += [mod_spec, pl.BlockSpec((1, d), lambda i, j: (0, 0)), pl.BlockSpec((d, tn), lambda i, j: (0, j))]
    args += [mod, g_norm.reshape(1, d), w_bf16]
    z_spec = pl.BlockSpec((tm, tn), lambda i, j: (i, j))
    z_shape = jax.ShapeDtypeStruct((t, n), F32)
    if res is not None:
        out_specs, out_shape = [z_spec, x_spec], [z_shape, jax.ShapeDtypeStruct((t, d), F32)]
    else:
        out_specs, out_shape = z_spec, z_shape
    return pl.pallas_call(
        functools.partial(_in_kernel, res is not None),
        grid=(t // tm, n // tn),
        in_specs=in_specs,
        out_specs=out_specs,
        out_shape=out_shape,
        scratch_shapes=[pltpu.VMEM((tm, d), BF16)],
        compiler_params=_cparams(("arbitrary", "arbitrary")),
        name="in_proj",
    )(*args)


def _softmax_pv(scores, values, sink):
    m = sink
    for s in scores:
        m = jnp.maximum(m, jnp.max(s, axis=-1, keepdims=True))
    den = jnp.exp(sink - m)
    acc = None
    for s, v in zip(scores, values):
        p = jnp.exp(s - m)
        den = den + jnp.sum(p, axis=-1, keepdims=True)
        pv = _dot(p.astype(BF16), v)
        acc = pv if acc is None else acc + pv
    return acc / den


def _attn_ctx_kernel(sink_ref, q_ref, kv_ref, o_ref):
    q = (q_ref[...] * ATTN_SCALE).astype(BF16)
    kv = kv_ref[...].astype(BF16)
    outs = []
    for h in range(N_HEADS):
        g = h // GQA_GROUP
        qh = q[:, HEAD_DIM * h:HEAD_DIM * (h + 1)]
        kh = kv[:, HEAD_DIM * g:HEAD_DIM * (g + 1)]
        vh = kv[:, KV_W + HEAD_DIM * g:KV_W + HEAD_DIM * (g + 1)]
        outs.append(_softmax_pv([_dot(qh, kh, _NT)], [vh], sink_ref[h]))
    o_ref[...] = jnp.concatenate(outs, axis=-1).astype(BF16)


def _attn_ctx_call(z, sink, n_seq, seq_len):
    return pl.pallas_call(
        _attn_ctx_kernel,
        grid=(n_seq,),
        in_specs=[
            pl.BlockSpec(memory_space=pltpu.SMEM),
            pl.BlockSpec((seq_len, ATT_W), lambda s: (s, COL_QKV // ATT_W)),
            pl.BlockSpec((seq_len, 2 * KV_W), lambda s: (s, (COL_QKV + ATT_W) // (2 * KV_W))),
        ],
        out_specs=pl.BlockSpec((seq_len, ATT_W), lambda s: (s, 0)),
        out_shape=jax.ShapeDtypeStruct((n_seq * seq_len, ATT_W), BF16),
        compiler_params=_cparams(("arbitrary",)),
        name="attn_ctx",
    )(sink, z, z)


def _rope(x, cos, sin_signed):
    n = x.shape[-1]
    lane = lax.broadcasted_iota(jnp.int32, x.shape, 1)
    up = pltpu.roll(x, n - 16, 1)
    dn = pltpu.roll(x, 16, 1)
    partner = jnp.where((lane & 31) < 16, up, dn)
    return x * cos + partner * sin_signed


def _attn_lat_kernel(sink_ref, q_ref, kvp_ref, kvc_ref, kvn_ref, ck_ref, cv_ref,
                     cosq_ref, sinq_ref, cosp_ref, sinp_ref, cosn_ref, sinn_ref, o_ref):
    qb = pl.program_id(1)
    nb = pl.num_programs(1)
    cq, sq = cosq_ref[...], sinq_ref[...]
    q = _rope(q_ref[...], jnp.concatenate([cq] * 4, axis=1), jnp.concatenate([sq] * 4, axis=1))
    q = (q * ATTN_SCALE).astype(BF16)
    kvp, kvc, kvn = kvp_ref[...], kvc_ref[...], kvn_ref[...]
    kp = _rope(kvp[:, :KV_W], cosp_ref[...], sinp_ref[...]).astype(BF16)
    kc = _rope(kvc[:, :KV_W], cq, sq).astype(BF16)
    kn = _rope(kvn[:, :KV_W], cosn_ref[...], sinn_ref[...]).astype(BF16)
    vp, vc, vn = (t[:, KV_W:].astype(BF16) for t in (kvp, kvc, kvn))
    ck = ck_ref[...].astype(BF16)
    cv = cv_ref[...].astype(BF16)
    qi = lax.broadcasted_iota(jnp.int32, (Q_BLOCK, Q_BLOCK), 0)
    kj = lax.broadcasted_iota(jnp.int32, (Q_BLOCK, Q_BLOCK), 1)
    mask_p = (kj >= qi) & (qb > 0)
    mask_n = (kj <= qi) & (qb < nb - 1)
    outs = []
    for h in range(N_HEADS):
        g = h // GQA_GROUP
        hs = slice(HEAD_DIM * h, HEAD_DIM * (h + 1))
        gs = slice(HEAD_DIM * g, HEAD_DIM * (g + 1))
        qh = q[:, hs]
        s_p = jnp.where(mask_p, _dot(qh, kp[:, gs], _NT), NEG_BIG)
        s_c = _dot(qh, kc[:, gs], _NT)
        s_n = jnp.where(mask_n, _dot(qh, kn[:, gs], _NT), NEG_BIG)
        s_x = _dot(qh, ck[:, gs], _NT)
        outs.append(_softmax_pv([s_p, s_c, s_n, s_x], [vp[:, gs], vc[:, gs], vn[:, gs], cv[:, gs]], sink_ref[h]))
    o_ref[...] = jnp.concatenate(outs, axis=-1).astype(BF16)


def _attn_lat_call(z, sink, cache_k, cache_v, cos_t, sin_t, n_seq, seq_len, row0):
    nb = seq_len // Q_BLOCK
    base = row0 // Q_BLOCK
    past = cache_k.shape[1]

    def rows(off):
        return lambda b, i: (base + b * nb + jnp.clip(i + off, 0, nb - 1))

    def kv_spec(off):
        r = rows(off)
        return pl.BlockSpec((Q_BLOCK, 2 * KV_W), lambda b, i: (r(b, i), (COL_QKV + ATT_W) // (2 * KV_W)))

    def tab_spec(off):
        return pl.BlockSpec((Q_BLOCK, KV_W), lambda b, i: (jnp.clip(i + off, 0, nb - 1), 0))

    r0 = rows(0)
    return pl.pallas_call(
        _attn_lat_kernel,
        grid=(n_seq, nb),
        in_specs=[
            pl.BlockSpec(memory_space=pltpu.SMEM),
            pl.BlockSpec((Q_BLOCK, ATT_W), lambda b, i: (r0(b, i), COL_QKV // ATT_W)),
            kv_spec(-1), kv_spec(0), kv_spec(1),
            pl.BlockSpec((None, past, KV_W), lambda b, i: (b, 0, 0)),
            pl.BlockSpec((None, past, KV_W), lambda b, i: (b, 0, 0)),
            tab_spec(0), tab_spec(0), tab_spec(-1), tab_spec(-1), tab_spec(1), tab_spec(1),
        ],
        out_specs=pl.BlockSpec((Q_BLOCK, ATT_W), lambda b, i: (b * nb + i, 0)),
        out_shape=jax.ShapeDtypeStruct((n_seq * seq_len, ATT_W), BF16),
        compiler_params=_cparams(("arbitrary", "arbitrary")),
        name="attn_lat",
    )(sink, z, z, z, z, cache_k, cache_v, cos_t, sin_t, cos_t, sin_t, cos_t, sin_t)


def _rope_tables(seq_len):
    half = HEAD_DIM // 2
    pos = np.arange(seq_len)
    inv_freq = 1.0 / (ROPE_THETA ** (np.arange(0, half, 2, dtype=np.float32) / half))
    inv_freq = inv_freq.astype(np.float32)

    def part(p):
        ang = (p.astype(np.float32)[:, None] * inv_freq[None, :]).astype(np.float32)
        c, s = np.cos(ang), np.sin(ang)
        return np.concatenate([c, c], axis=1), np.concatenate([-s, s], axis=1)

    c_r, s_r = part(pos // GRID_W)
    c_c, s_c = part(pos % GRID_W)
    cos = np.concatenate([c_r, c_c] * N_KV_HEADS, axis=1).astype(np.float32)
    sin = np.concatenate([s_r, s_c] * N_KV_HEADS, axis=1).astype(np.float32)
    return jnp.asarray(cos), jnp.asarray(sin)


def _rwkv_dir_inputs(zm, zl, d, w0, w2, a0, a2, kk_w, ka_w, ones_head):
    r = zm[:, :RWKV_W]
    kraw = zm[:, RWKV_W:2 * RWKV_W]
    v = zm[:, 2 * RWKV_W:]
    wl = zl[:, DECAY_LORA * d:DECAY_LORA * (d + 1)]
    al = zl[:, 2 * DECAY_LORA + ICLR_LORA * d:2 * DECAY_LORA + ICLR_LORA * (d + 1)]
    xw = w0 + _dot(jnp.tanh(wl).astype(BF16), w2.astype(BF16))
    ld = -EXP_NEG_HALF * _sigmoid(xw)
    a = _sigmoid(a0 + _dot(al.astype(BF16), a2.astype(BF16)))
    k = kraw * (1.0 + (a - 1.0) * ka_w)
    kkr = kraw * kk_w
    n2 = _dot_exact_rhs(kkr * kkr, ones_head)
    kk = kkr / jnp.maximum(jnp.sqrt(n2), 1e-12)
    return r, v, kk, ld, a, k


def _pair_chunk(r, v, kk, ld, a, k, z, rev):
    c = CHUNK
    ti = lax.broadcasted_iota(jnp.int32, (c, c), 0)
    si = lax.broadcasted_iota(jnp.int32, (c, c), 1)
    incl = (si >= ti) if rev else (si <= ti)
    strict = (si > ti) if rev else (si < ti)
    eye = (si == ti).astype(F32)
    m_incl = jnp.where(incl, 1.0, 0.0).astype(BF16)
    ldh, ldl = _split(ld)
    cin = _dot(m_incl, ldh) + _dot(m_incl, ldl)
    tot = jnp.sum(ld, axis=0, keepdims=True)
    e_neg = jnp.exp(-cin)
    e_end = jnp.exp(tot - cin)
    bb = kk * a
    a_m = kk * jnp.exp(cin - ld)
    b_m = (bb * e_neg).astype(BF16)
    k_m = (k * e_neg).astype(BF16)
    r_m = r * jnp.exp(cin)
    b_end = (bb * e_end).astype(BF16)
    k_end = (k * e_end).astype(BF16)
    lane = lax.broadcasted_iota(jnp.int32, (1, PAIR), 1)
    head_masks = (lane < RWKV_HEAD, lane >= RWKV_HEAD)
    a_h = [jnp.where(m, a_m, 0.0).astype(BF16) for m in head_masks]
    r_h = [jnp.where(m, r_m, 0.0).astype(BF16) for m in head_masks]
    lhs = jnp.concatenate(a_h + r_h, axis=0)
    xb = _dot(lhs, b_m, _NT)
    xk = _dot(lhs, k_m, _NT)
    w_sum = jnp.zeros((c, PAIR), F32)
    u_sum = jnp.zeros((c, PAIR), F32)
    r_eff = r_m
    o_loc = jnp.zeros((c, PAIR), F32)
    for h in range(2):
        m_ab = jnp.where(strict, xb[c * h:c * (h + 1)], 0.0)
        m_ak = jnp.where(strict, xk[c * h:c * (h + 1)], 0.0).astype(BF16)
        m_rb = jnp.where(incl, xb[c * (2 + h):c * (3 + h)], 0.0).astype(BF16)
        m_rk = jnp.where(incl, xk[c * (2 + h):c * (3 + h)], 0.0).astype(BF16)
        t_inv = eye - m_ab
        lp = m_ab
        for _ in range(5):
            lpb = lp.astype(BF16)
            lp = _dot(lpb, lpb)
            t_inv = t_inv + _dot(t_inv.astype(BF16), lp.astype(BF16))
        v_h = jnp.where(head_masks[h], v, 0.0).astype(BF16)
        mak_v = _dot(m_ak, v_h)
        wu = _dot(t_inv.astype(BF16), jnp.concatenate([a_h[h], mak_v.astype(BF16)], axis=1))
        w_sum = w_sum + wu[:, :PAIR]
        u_sum = u_sum + wu[:, PAIR:]
        ro = _dot(m_rb, wu.astype(BF16))
        r_eff = r_eff - ro[:, :PAIR]
        o_loc = o_loc + (_dot(m_rk, v_h) - ro[:, PAIR:])
    wu_cat = jnp.concatenate([w_sum, u_sum], axis=1).astype(BF16)
    bt_wu = _dot(b_end, wu_cat, _TN)
    kt_v = _dot(k_end, v.astype(BF16), _TN)
    ri = lax.broadcasted_iota(jnp.int32, (PAIR, PAIR), 0)
    ci = lax.broadcasted_iota(jnp.int32, (PAIR, PAIR), 1)
    same_head = (ri < RWKV_HEAD) == (ci < RWKV_HEAD)
    g_t = jnp.where(same_head, jnp.where(ri == ci, jnp.exp(tot), 0.0) - bt_wu[:, :PAIR], 0.0)
    h_t = jnp.where(same_head, kt_v - bt_wu[:, PAIR:], 0.0)
    zh, zl = _split(z)

    def hp(x):
        xh, xl = _split(x)
        return _dot(xh, zh) + (_dot(xh, zl) + _dot(xl, zh))

    out = hp(r_eff) + o_loc
    z_new = hp(g_t) + h_t
    return out, z_new


def _rwkv_kernel(zmf_ref, zlf_ref, zmb_ref, zlb_ref, w0_ref, w2_ref, a0_ref, a2_ref, kkw_ref, kaw_ref,
                 ones_ref, s0f_ref, s0b_ref, of_ref, ob_ref, zf_ref, zb_ref, zf_scr, zb_scr):
    @pl.when(pl.program_id(1) == 0)
    def _():
        zf_scr[...] = s0f_ref[...]
        zb_scr[...] = s0b_ref[...]

    ones_head = ones_ref[...]
    for d, (zm_ref, zl_ref, z_scr, o_ref) in enumerate(((zmf_ref, zlf_ref, zf_scr, of_ref),
                                                        (zmb_ref, zlb_ref, zb_scr, ob_ref))):
        r, v, kk, ld, a, k = _rwkv_dir_inputs(zm_ref[...], zl_ref[...], d, w0_ref[d:d + 1, :], w2_ref[d],
                                              a0_ref[d:d + 1, :], a2_ref[d], kkw_ref[...], kaw_ref[...], ones_head)
        for p in range(N_PAIR):
            ps = slice(PAIR * p, PAIR * (p + 1))
            out, z_new = _pair_chunk(r[:, ps], v[:, ps], kk[:, ps], ld[:, ps], a[:, ps], k[:, ps], z_scr[p], d == 1)
            o_ref[:, ps] = out
            z_scr[p] = z_new
    zf_ref[...] = zf_scr[...]
    zb_ref[...] = zb_scr[...]


def _rwkv_call(z, lw, s0f, s0b, n_seq, seq_len, row0):
    nc = seq_len // CHUNK
    base = row0 // CHUNK

    def fwd(s, c):
        return base + s * nc + c

    def bwd(s, c):
        return base + s * nc + (nc - 1 - c)

    def zm_spec(f):
        return pl.BlockSpec((CHUNK, 3 * RWKV_W), lambda s, c: (f(s, c), COL_RWKV // (3 * RWKV_W)))

    def zl_spec(f):
        return pl.BlockSpec((CHUNK, LORA_W), lambda s, c: (f(s, c), COL_LORA // LORA_W))

    def full(shape):
        return pl.BlockSpec(shape, lambda s, c: (0,) * len(shape))

    st_spec = pl.BlockSpec((None, N_PAIR, PAIR, PAIR), lambda s, c: (s, 0, 0, 0))
    o_shape = jax.ShapeDtypeStruct((n_seq * seq_len, RWKV_W), F32)
    st_shape = jax.ShapeDtypeStruct((n_seq, N_PAIR, PAIR, PAIR), F32)
    return pl.pallas_call(
        _rwkv_kernel,
        grid=(n_seq, nc),
        in_specs=[
            zm_spec(fwd), zl_spec(fwd), zm_spec(bwd), zl_spec(bwd),
            full((2, RWKV_W)), full((2, DECAY_LORA, RWKV_W)), full((2, RWKV_W)), full((2, ICLR_LORA, RWKV_W)),
            full((1, RWKV_W)), full((1, RWKV_W)), full((RWKV_W, RWKV_W)),
            st_spec, st_spec,
        ],
        out_specs=[
            pl.BlockSpec((CHUNK, RWKV_W), lambda s, c: (s * nc + c, 0)),
            pl.BlockSpec((CHUNK, RWKV_W), lambda s, c: (s * nc + (nc - 1 - c), 0)),
            st_spec, st_spec,
        ],
        out_shape=[o_shape, o_shape, st_shape, st_shape],
        scratch_shapes=[pltpu.VMEM((N_PAIR, PAIR, PAIR), F32), pltpu.VMEM((N_PAIR, PAIR, PAIR), F32)],
        compiler_params=_cparams(("arbitrary", "arbitrary")),
        name="rwkv_scan",
    )(z, z, z, z, lw["w0"], lw["w2"], lw["a0"], lw["a2"], lw["kk_w"], lw["ka_w"], lw["ones_head"], s0f, s0b)


def _state_to_z(s):
    n = s.shape[0]
    st = jnp.swapaxes(s, -1, -2).reshape(n, N_PAIR, 2, RWKV_HEAD, RWKV_HEAD)
    zero = jnp.zeros_like(st[:, :, 0])
    top = jnp.concatenate([st[:, :, 0], zero], axis=-1)
    bot = jnp.concatenate([zero, st[:, :, 1]], axis=-1)
    return jnp.concatenate([top, bot], axis=-2)


def _z_to_state(z):
    n = z.shape[0]
    h0 = z[:, :, :RWKV_HEAD, :RWKV_HEAD]
    h1 = z[:, :, RWKV_HEAD:, RWKV_HEAD:]
    st = jnp.stack([h0, h1], axis=2).reshape(n, 2 * N_PAIR, RWKV_HEAD, RWKV_HEAD)
    return jnp.swapaxes(st, -1, -2)


def _mix_kernel(tiles, x_ref, zc_ref, zcp_ref, zcn_ref, zm_ref, zl_ref, zg_ref, of_ref, ob_ref, ya_ref, mod_ref,
                convw_ref, a0_ref, a2_ref, g2_ref, kaw_ref, rkw_ref, lng_ref, lnb_ref, ones_ref, mean_ref,
                wa_ref, wb_ref, wc_ref, wo_ref, o_ref):
    n_ctx_tiles, per_ctx, per_lat = tiles
    i = pl.program_id(0)
    pos = jnp.where(i < n_ctx_tiles, i % per_ctx, (i - n_ctx_tiles) % per_lat)
    per = jnp.where(i < n_ctx_tiles, per_ctx, per_lat)
    tm = x_ref.shape[0]

    zc = zc_ref[...]
    u = zc[:, 2 * CONV_W:] * zc[:, :CONV_W]
    zp = zcp_ref[...]
    zn = zcn_ref[...]
    u_prev_row = jnp.where(pos > 0, zp[7:8, 2 * CONV_W:] * zp[7:8, :CONV_W], 0.0)
    u_next_row = jnp.where(pos < per - 1, zn[0:1, 2 * CONV_W:] * zn[0:1, :CONV_W], 0.0)
    row = lax.broadcasted_iota(jnp.int32, (tm, CONV_W), 0)
    u_prev = jnp.where(row == 0, u_prev_row, pltpu.roll(u, 1, 0))
    u_next = jnp.where(row == tm - 1, u_next_row, pltpu.roll(u, tm - 1, 0))
    cw = convw_ref[...]
    y_conv = zc[:, CONV_W:2 * CONV_W] * (cw[0:1, :] * u_prev + cw[1:2, :] * u + cw[2:3, :] * u_next)

    zm = zm_ref[...]
    zl = zl_ref[...]
    r = zm[:, :RWKV_W]
    kraw = zm[:, RWKV_W:2 * RWKV_W]
    v = zm[:, 2 * RWKV_W:]
    ones_head = ones_ref[...]
    mean_head = mean_ref[...]
    o = of_ref[...] + ob_ref[...]
    mu = _dot_exact_rhs(o, mean_head)
    dlt = o - mu
    var = _dot_exact_rhs(dlt * dlt, mean_head)
    y = dlt * lax.rsqrt(var + GN_EPS) * lng_ref[...] + lnb_ref[...]
    for d in range(2):
        al = zl[:, 2 * DECAY_LORA + ICLR_LORA * d:2 * DECAY_LORA + ICLR_LORA * (d + 1)]
        a = _sigmoid(a0_ref[d:d + 1, :] + _dot(al.astype(BF16), a2_ref[d].astype(BF16)))
        k = kraw * (1.0 + (a - 1.0) * kaw_ref[...])
        y = y + _dot_exact_rhs(r * k * rkw_ref[...], ones_head) * v
    g1 = zl[:, 2 * DECAY_LORA + 2 * ICLR_LORA:]
    y_rwkv = y * _dot(_sigmoid(g1).astype(BF16), g2_ref[...].astype(BF16))

    zg = zg_ref[...]
    merged = (_sigmoid(zg[:, :D_MODEL]) * _dot(y_conv.astype(BF16), wa_ref[...])
              + _sigmoid(zg[:, D_MODEL:2 * D_MODEL]) * _dot(ya_ref[...], wb_ref[...])
              + _sigmoid(zg[:, 2 * D_MODEL:]) * _dot(y_rwkv.astype(BF16), wc_ref[...]))
    o_ref[...] = x_ref[...] + mod_ref[2:3, :] * _dot(merged.astype(BF16), wo_ref[...])


def _mix_call(x, z, o_f, o_b, y_attn, mod, row_of_tile, lw, tiles):
    t, d = x.shape
    tm = TM_MIX
    nt = t // tm
    hb = tm // 8

    def rows(w, col):
        return pl.BlockSpec((tm, w), lambda i: (i, col // w))

    def full(shape):
        return pl.BlockSpec(shape, lambda i: (0,) * len(shape))

    in_specs = [
        rows(d, 0),
        rows(3 * CONV_W, COL_CONV),
        pl.BlockSpec((8, 3 * CONV_W), lambda i: (jnp.maximum(i * hb - 1, 0), COL_CONV // (3 * CONV_W))),
        pl.BlockSpec((8, 3 * CONV_W), lambda i: (jnp.minimum((i + 1) * hb, nt * hb - 1), COL_CONV // (3 * CONV_W))),
        rows(3 * RWKV_W, COL_RWKV),
        rows(LORA_W, COL_LORA),
        rows(3 * D_MODEL, COL_GATES),
        rows(RWKV_W, 0), rows(RWKV_W, 0), rows(ATT_W, 0),
        pl.BlockSpec((None, N_MOD, d), lambda i: (row_of_tile(i, tm), 0, 0)),
        full((3, CONV_W)), full((2, RWKV_W)), full((2, ICLR_LORA, RWKV_W)), full((GATE_LORA, RWKV_W)),
        full((1, RWKV_W)), full((1, RWKV_W)), full((1, RWKV_W)), full((1, RWKV_W)),
        full((RWKV_W, RWKV_W)), full((RWKV_W, RWKV_W)),
        full((CONV_W, d)), full((ATT_W, d)), full((RWKV_W, d)), full((d, d)),
    ]
    return pl.pallas_call(
        functools.partial(_mix_kernel, tiles),
        grid=(nt,),
        in_specs=in_specs,
        out_specs=rows(d, 0),
        out_shape=jax.ShapeDtypeStruct((t, d), F32),
        compiler_params=_cparams(("arbitrary",)),
        name="mix",
    )(x, z, z, z, z, z, z, o_f, o_b, y_attn, mod,
      lw["conv_w"], lw["a0"], lw["a2"], lw["g2"], lw["ka_w"], lw["rk_w"], lw["ln_g"], lw["ln_b"],
      lw["ones_head"], lw["mean_head"], lw["wa"], lw["wb"], lw["wc"], lw["wo"])


def _route_kernel(x_ref, mod_ref, g_ref, rwt_ref, rb_ref, tri_ref, h_ref, e_ref, gate_ref, rank_ref, cnt_ref, cnt_scr):
    @pl.when(pl.program_id(0) == 0)
    def _():
        cnt_scr[...] = jnp.zeros_like(cnt_scr)

    h2 = _rms_mod(x_ref[...], g_ref[...], mod_ref[4:5, :], mod_ref[3:4, :])
    h_ref[...] = h2.astype(BF16)
    logits = _dot3(rwt_ref[...], h2, _NT) + rb_ref[...]
    ne, tm = logits.shape
    ex = lax.broadcasted_iota(jnp.int32, (ne, tm), 0)
    work = logits
    vals, hots = [], []
    for kq in range(TOP_K):
        m = jnp.max(work, axis=0, keepdims=True)
        idx = jnp.min(jnp.where(work == m, ex, ne), axis=0, keepdims=True)
        hot = ex == idx
        vals.append(m)
        hots.append(hot)
        e_ref[kq:kq + 1, :] = idx
        work = jnp.where(hot, -jnp.inf, work)
    exps = [jnp.exp(vk - vals[0]) for vk in vals]
    den = exps[0] + exps[1] + exps[2] + exps[3]
    chosen = jnp.where(hots[0] | hots[1] | hots[2] | hots[3], 1.0, 0.0)
    before = cnt_scr[:, 0:1] + _dot(chosen.astype(BF16), tri_ref[...])
    for kq in range(TOP_K):
        gate_ref[kq:kq + 1, :] = exps[kq] / den
        rank_ref[kq:kq + 1, :] = jnp.sum(jnp.where(hots[kq], before, 0.0), axis=0, keepdims=True).astype(jnp.int32)
    cnt_scr[...] = cnt_scr[...] + jnp.sum(chosen, axis=1, keepdims=True)
    cnt_ref[...] = cnt_scr[...].astype(jnp.int32)


def _route_call(x, mod, row_of_tile, g_norm, router_w, router_b):
    t, d = x.shape
    tm = TM_ROUTE
    ne = router_w.shape[1]
    tri = jnp.triu(jnp.ones((tm, tm), F32), 1).astype(BF16)

    def full(shape):
        return pl.BlockSpec(shape, lambda i: (0,) * len(shape))

    kt_spec = pl.BlockSpec((TOP_K, tm), lambda i: (0, i))
    return pl.pallas_call(
        _route_kernel,
        grid=(t // tm,),
        in_specs=[
            pl.BlockSpec((tm, d), lambda i: (i, 0)),
            pl.BlockSpec((None, N_MOD, d), lambda i: (row_of_tile(i, tm), 0, 0)),
            full((1, d)), full((ne, d)), full((ne, 1)), full((tm, tm)),
        ],
        out_specs=[pl.BlockSpec((tm, d), lambda i: (i, 0)), kt_spec, kt_spec, kt_spec, full((ne, 128))],
        out_shape=[
            jax.ShapeDtypeStruct((t, d), BF16),
            jax.ShapeDtypeStruct((TOP_K, t), jnp.int32),
            jax.ShapeDtypeStruct((TOP_K, t), F32),
            jax.ShapeDtypeStruct((TOP_K, t), jnp.int32),
            jax.ShapeDtypeStruct((ne, 128), jnp.int32),
        ],
        scratch_shapes=[pltpu.VMEM((ne, 128), F32)],
        compiler_params=_cparams(("arbitrary",)),
        name="route",
    )(x, mod, g_norm.reshape(1, d), router_w.T, router_b.reshape(ne, 1), tri)


def _moe_kernel(be_ref, first_ref, nused_ref, x_ref, gate_ref, w1_ref, b1_ref, w2_ref, b2_ref, o_ref, w1_scr, w2_scr):
    i = pl.program_id(0)

    @pl.when(first_ref[i] == 1)
    def _():
        w1_scr[...] = w1_ref[...].astype(BF16)
        w2_scr[...] = w2_ref[...].astype(BF16)

    @pl.when(i < nused_ref[0])
    def _():
        hm = _dot(x_ref[...], w1_scr[...]) + b1_ref[...]
        glu = jnp.minimum(hm[:, :D_EXPERT], SWIGLU_LIMIT)
        lin = jnp.clip(hm[:, D_EXPERT:], -SWIGLU_LIMIT, SWIGLU_LIMIT)
        act = glu * _sigmoid(SWIGLU_ALPHA * glu) * (lin + 1.0)
        y = _dot(act.astype(BF16), w2_scr[...]) + b2_ref[...]
        o_ref[...] = y * gate_ref[...]

    @pl.when(i >= nused_ref[0])
    def _():
        o_ref[...] = jnp.zeros_like(o_ref)


def _moe_call(rows, gate_rows, blk_e, blk_first, n_used, w1, b1, w2, b2):
    npad, d = rows.shape
    tm = TM_MOE
    ne, _, dh2 = w1.shape
    de = w2.shape[1]
    grid_spec = pltpu.PrefetchScalarGridSpec(
        num_scalar_prefetch=3,
        grid=(npad // tm,),
        in_specs=[
            pl.BlockSpec((tm, d), lambda i, be, fi, nu: (i, 0)),
            pl.BlockSpec((tm, 1), lambda i, be, fi, nu: (i, 0)),
            pl.BlockSpec((None, d, dh2), lambda i, be, fi, nu: (be[i], 0, 0)),
            pl.BlockSpec((None, 1, dh2), lambda i, be, fi, nu: (be[i], 0, 0)),
            pl.BlockSpec((None, de, d), lambda i, be, fi, nu: (be[i], 0, 0)),
            pl.BlockSpec((None, 1, d), lambda i, be, fi, nu: (be[i], 0, 0)),
        ],
        out_specs=pl.BlockSpec((tm, d), lambda i, be, fi, nu: (i, 0)),
        scratch_shapes=[pltpu.VMEM((d, dh2), BF16), pltpu.VMEM((de, d), BF16)],
    )
    return pl.pallas_call(
        _moe_kernel,
        grid_spec=grid_spec,
        out_shape=jax.ShapeDtypeStruct((npad, d), F32),
        compiler_params=_cparams(("arbitrary",)),
        name="moe",
    )(blk_e, blk_first, n_used, rows, gate_rows, w1, b1.reshape(ne, 1, dh2), w2, b2.reshape(ne, 1, d))


def _moe_layer(h2, e_t, gate_t, rank_t, counts, w1, b1, w2, b2):
    t, d = h2.shape
    tm = TM_MOE
    ne = w1.shape[0]
    n_assign = t * TOP_K
    n_blk = -(-(n_assign + ne * (tm - 1)) // tm)
    npad = n_blk * tm
    padded = (counts + tm - 1) // tm * tm
    pends = jnp.cumsum(padded)
    pstarts = pends - padded
    dest = pstarts[e_t] + rank_t
    n_used = (pends[-1] // tm).astype(jnp.int32)
    blk = jnp.arange(n_blk, dtype=jnp.int32)
    blk_e = jnp.minimum(jnp.searchsorted(pends, blk * tm, side="right"), ne - 1).astype(jnp.int32)
    blk_e = jnp.where(blk < n_used, blk_e, blk_e[jnp.maximum(n_used - 1, 0)])
    blk_first = jnp.concatenate([jnp.ones((1,), jnp.int32), (blk_e[1:] != blk_e[:-1]).astype(jnp.int32)])
    dflat = dest.reshape(-1)
    tok = jnp.tile(jnp.arange(t, dtype=jnp.int32), TOP_K)
    src_tok = jnp.zeros((npad,), jnp.int32).at[dflat].set(tok)
    gate_rows = jnp.zeros((npad,), F32).at[dflat].set(gate_t.reshape(-1))
    rows = h2[src_tok]
    y_rows = _moe_call(rows, gate_rows.reshape(npad, 1), blk_e, blk_first, n_used.reshape(1), w1, b1, w2, b2)
    return y_rows[dest].sum(axis=0)


def _final_kernel(x_ref, moe_ref, mod_ref, g_ref, o_ref):
    x = x_ref[...] + mod_ref[5:6, :] * moe_ref[...]
    ms = jnp.mean(x * x, axis=-1, keepdims=True)
    o_ref[...] = x * lax.rsqrt(ms + RMS_EPS) * g_ref[...]


def _final_call(x, moe, mod, row_of_tile, g_final):
    t, d = x.shape
    tm = TM_IN
    spec = pl.BlockSpec((tm, d), lambda i: (i, 0))
    return pl.pallas_call(
        _final_kernel,
        grid=(t // tm,),
        in_specs=[spec, spec, pl.BlockSpec((None, N_MOD, d), lambda i: (row_of_tile(i, tm), 0, 0)),
                  pl.BlockSpec((1, d), lambda i: (0, 0))],
        out_specs=spec,
        out_shape=jax.ShapeDtypeStruct((t, d), F32),
        compiler_params=_cparams(("arbitrary",)),
        name="final_norm",
    )(x, moe, mod, g_final.reshape(1, d))


def _permute_cols(w):
    o = _SRC_OFF
    return jnp.concatenate([
        w[..., o["rwkv"]:o["lora"]], w[..., o["conv"]:o["qkv"]], w[..., o["gates"]:],
        w[..., o["qkv"]:o["rwkv"]], w[..., o["lora"]:o["gates"]]], axis=-1)


def kernel(x_prompt, x_sample, cache_k, cache_v, state_rwkv_fwd, state_rwkv_bwd, c, c_ctx, w_ada, b_ada, g_norm1, g_norm2, w_in, conv_w, attn_sink, rwkv_w0, rwkv_w2, rwkv_a0, rwkv_a2, rwkv_g2, rwkv_k_k, rwkv_k_a, rwkv_r_k, rwkv_ln_g, rwkv_ln_b, w_branch_conv, w_branch_attn, w_branch_rwkv, w_out, router_w, router_b, moe_w1, moe_b1, moe_w2, moe_b2, g_final):
    bc, lc, d = x_prompt.shape
    bl, tl, _ = x_sample.shape
    depth = w_in.shape[0]
    n_ctx = bc * lc
    t_all = n_ctx + bl * tl
    assert lc % TM_MIX == 0 and tl % TM_IN == 0 and n_ctx % TM_IN == 0 and t_all % TM_ROUTE == 0

    def row_of_tile(i, tm):
        return jnp.where(i < n_ctx // tm, 0, 1 + (i - n_ctx // tm) // (tl // tm))

    x = jnp.concatenate([x_prompt.reshape(n_ctx, d), x_sample.reshape(bl * tl, d)], axis=0)
    n_cond = -(-(1 + bl) // 8) * 8
    cond = jnp.zeros((n_cond, d), F32).at[0].set(c_ctx).at[1:1 + bl].set(c)
    mods = _ada_call(cond, w_ada, b_ada).reshape(depth, n_cond, N_MOD, d)

    w_in_p = _permute_cols(w_in).astype(BF16)
    head_id = np.arange(RWKV_W) // RWKV_HEAD
    ones_head = jnp.asarray((head_id[:, None] == head_id[None, :]).astype(np.float32), BF16)
    mean_head = (ones_head.astype(F32) / RWKV_HEAD).astype(BF16)
    cos_t, sin_t = _rope_tables(tl)
    zeros_state = jnp.zeros((bc, 2 * N_PAIR, RWKV_HEAD, RWKV_HEAD), F32)
    tiles = (n_ctx // TM_MIX, lc // TM_MIX, tl // TM_MIX)

    new_k, new_v, new_sf, new_sb = [], [], [], []
    res = None
    for l in range(depth):
        mod = mods[l]
        lw = dict(
            w0=rwkv_w0[l], w2=rwkv_w2[l], a0=rwkv_a0[l], a2=rwkv_a2[l], g2=rwkv_g2[l],
            kk_w=rwkv_k_k[l].reshape(1, -1), ka_w=rwkv_k_a[l].reshape(1, -1), rk_w=rwkv_r_k[l].reshape(1, -1),
            ln_g=rwkv_ln_g[l].reshape(1, -1), ln_b=rwkv_ln_b[l].reshape(1, -1), conv_w=conv_w[l],
            ones_head=ones_head, mean_head=mean_head,
            wa=w_branch_conv[l].astype(BF16), wb=w_branch_attn[l].astype(BF16), wc=w_branch_rwkv[l].astype(BF16),
            wo=w_out[l].astype(BF16),
        )
        if res is None:
            z = _in_call(x, w_in_p[l], g_norm1[l], mod, row_of_tile)
        else:
            z, x = _in_call(x, w_in_p[l], g_norm1[l], mod, row_of_tile, res=res)

        kv_ctx = z[:n_ctx, COL_QKV + ATT_W:COL_QKV + ATT_W + 2 * KV_W]
        new_k.append(kv_ctx[:, :KV_W].reshape(bc, lc, N_KV_HEADS, HEAD_DIM))
        new_v.append(kv_ctx[:, KV_W:].reshape(bc, lc, N_KV_HEADS, HEAD_DIM))

        ya_c = _attn_ctx_call(z, attn_sink[l], bc, lc)
        ya_l = _attn_lat_call(z, attn_sink[l], cache_k[:, l].reshape(bl, -1, KV_W), cache_v[:, l].reshape(bl, -1, KV_W),
                              cos_t, sin_t, bl, tl, n_ctx)
        y_attn = jnp.concatenate([ya_c, ya_l], axis=0)

        of_c, ob_c, zf_c, zb_c = _rwkv_call(z, lw, _state_to_z(zeros_state), _state_to_z(zeros_state), bc, lc, 0)
        of_l, ob_l, _, _ = _rwkv_call(z, lw, _state_to_z(state_rwkv_fwd[:, l].astype(F32)),
                                      _state_to_z(state_rwkv_bwd[:, l].astype(F32)), bl, tl, n_ctx)
        new_sf.append(_z_to_state(zf_c))
        new_sb.append(_z_to_state(zb_c))
        o_f = jnp.concatenate([of_c, of_l], axis=0)
        o_b = jnp.concatenate([ob_c, ob_l], axis=0)

        x_mid = _mix_call(x, z, o_f, o_b, y_attn, mod, row_of_tile, lw, tiles)
        h2, e_t, gate_t, rank_t, cnt = _route_call(x_mid, mod, row_of_tile, g_norm2[l], router_w[l], router_b[l])
        moe = _moe_layer(h2, e_t, gate_t, rank_t, cnt[:, 0], moe_w1[l], moe_b1[l], moe_w2[l], moe_b2[l])
        x = x_mid
        res = (moe, mod)

    y = _final_call(x, res[0], res[1], row_of_tile, g_final)
    y_prompt = y[:n_ctx].reshape(bc, lc, d)
    y_sample = y[n_ctx:].reshape(bl, tl, d)
    dt = x_prompt.dtype
    return (y_prompt, y_sample, jnp.stack(new_k, axis=1), jnp.stack(new_v, axis=1),
            jnp.stack(new_sf, axis=1).astype(dt), jnp.stack(new_sb, axis=1).astype(dt))
```

```python
import functools

import numpy as np
import jax
import jax.numpy as jnp
from jax import lax
from jax.experimental import pallas as pl
from jax.experimental.pallas import tpu as pltpu

F32 = jnp.float32
BF16 = jnp.bfloat16

D_MODEL = 1024
N_MOD = 6
RMS_EPS = 1e-6
CONV_W = 512
N_HEADS = 8
N_KV_HEADS = 2
GQA_GROUP = N_HEADS // N_KV_HEADS
HEAD_DIM = 64
ATT_W = N_HEADS * HEAD_DIM
KV_W = N_KV_HEADS * HEAD_DIM
WINDOW = 128
Q_BLOCK = 128
ATTN_SCALE = HEAD_DIM ** -0.5
ROPE_THETA = 10000.0
GRID_W = 64
RWKV_HEAD = 64
RWKV_W = 512
DECAY_LORA = 64
ICLR_LORA = 64
GATE_LORA = 128
GN_EPS = 64e-5
N_EXPERTS = 32
TOP_K = 4
D_EXPERT = 1024
SWIGLU_LIMIT = 7.0
SWIGLU_ALPHA = 1.702
P_TOTAL = 7296

_SRC_OFF = dict(conv=0, qkv=1536, rwkv=2304, lora=3840, gates=4224)
COL_RWKV, COL_CONV, COL_GATES, COL_QKV, COL_LORA = 0, 1536, 3072, 6144, 6912
LORA_W = 2 * DECAY_LORA + 2 * ICLR_LORA + GATE_LORA

CHUNK = 64
PAIR = 2 * RWKV_HEAD
N_PAIR = RWKV_W // PAIR
NEG_BIG = -1e30
EXP_NEG_HALF = float(np.exp(-0.5))

TM_IN = 512
TN_IN = 2432
TM_MIX = 256
TM_ROUTE = 512
TM_MOE = 256
TM_DISPATCH = 512
TM_COMBINE = 256
VMEM_LIMIT = 56 * 1024 * 1024


def _cparams(sem, vmem=VMEM_LIMIT):
    return pltpu.CompilerParams(dimension_semantics=sem, vmem_limit_bytes=vmem)


def _dot(a, b, dims=(((1,), (0,)), ((), ()))):
    return lax.dot_general(a, b, dims, preferred_element_type=F32)


_NT = (((1,), (1,)), ((), ()))
_TN = (((0,), (0,)), ((), ()))


def _split(x):
    hi = x.astype(BF16)
    lo = (x - hi.astype(F32)).astype(BF16)
    return hi, lo


def _dot3(a, b, dims=(((1,), (0,)), ((), ()))):
    ah, al = _split(a)
    bh, bl = _split(b)
    return _dot(ah, bh, dims) + (_dot(ah, bl, dims) + _dot(al, bh, dims))


def _dot_exact_rhs(a, b_bf16, dims=(((1,), (0,)), ((), ()))):
    ah, al = _split(a)
    return _dot(ah, b_bf16, dims) + _dot(al, b_bf16, dims)


def _sigmoid(x):
    return 1.0 / (1.0 + jnp.exp(-x))


def _rms_mod(x, g, scale, shift):
    ms = jnp.mean(x * x, axis=-1, keepdims=True)
    return (x * lax.rsqrt(ms + RMS_EPS) * g) * (1.0 + scale) + shift


def _ada_kernel(cond_ref, w_ref, b_ref, o_ref):
    c = cond_ref[...]
    s = c * _sigmoid(c)
    o_ref[...] = _dot(s.astype(BF16), w_ref[...].astype(BF16)) + b_ref[...]


def _ada_call(cond, w_ada, b_ada):
    depth, d, n = w_ada.shape
    r = cond.shape[0]
    tn = 1536
    return pl.pallas_call(
        _ada_kernel,
        grid=(depth, n // tn),
        in_specs=[
            pl.BlockSpec((r, d), lambda l, j: (0, 0)),
            pl.BlockSpec((None, d, tn), lambda l, j: (l, 0, j)),
            pl.BlockSpec((None, 1, tn), lambda l, j: (l, 0, j)),
        ],
        out_specs=pl.BlockSpec((None, r, tn), lambda l, j: (l, 0, j)),
        out_shape=jax.ShapeDtypeStruct((depth, r, n), F32),
        compiler_params=_cparams(("arbitrary", "arbitrary")),
        name="ada",
    )(cond, w_ada, b_ada.reshape(depth, 1, n))


def _in_kernel(has_res, *refs):
    if has_res:
        x_ref, moe_ref, modp_ref, mod_ref, g_ref, w_ref, z_ref, xo_ref, h_scr = refs
    else:
        x_ref, mod_ref, g_ref, w_ref, z_ref, h_scr = refs

    @pl.when(pl.program_id(1) == 0)
    def _():
        x = x_ref[...]
        if has_res:
            x = x + modp_ref[5:6, :] * moe_ref[...]
            xo_ref[...] = x
        h_scr[...] = _rms_mod(x, g_ref[...], mod_ref[1:2, :], mod_ref[0:1, :]).astype(BF16)

    z_ref[...] = _dot(h_scr[...], w_ref[...])


def _in_call(x, w_bf16, g_norm, mod, row_of_tile, res=None):
    t, d = x.shape
    n = w_bf16.shape[1]
    tm, tn = TM_IN, TN_IN
    mod_spec = pl.BlockSpec((None, N_MOD, d), lambda i, j: (row_of_tile(i, tm), 0, 0))
    x_spec = pl.BlockSpec((tm, d), lambda i, j: (i, 0))
    in_specs = [x_spec]
    args = [x]
    if res is not None:
        moe, mod_prev = res
        in_specs += [x_spec, mod_spec]
        args += [moe, mod_prev]
    in_specs += [mod_spec, pl.BlockSpec((1, d), lambda i, j: (0, 0)), pl.BlockSpec((d, tn), lambda i, j: (0, j))]
    args += [mod, g_norm.reshape(1, d), w_bf16]
    z_spec = pl.BlockSpec((tm, tn), lambda i, j: (i, j))
    z_shape = jax.ShapeDtypeStruct((t, n), F32)
    if res is not None:
        out_specs, out_shape = [z_spec, x_spec], [z_shape, jax.ShapeDtypeStruct((t, d), F32)]
    else:
        out_specs, out_shape = z_spec, z_shape
    return pl.pallas_call(
        functools.partial(_in_kernel, res is not None),
        grid=(t // tm, n // tn),
        in_specs=in_specs,
        out_specs=out_specs,
        out_shape=out_shape,
        scratch_shapes=[pltpu.VMEM((tm, d), BF16)],
        compiler_params=_cparams(("arbitrary", "arbitrary")),
        name="in_proj",
    )(*args)


def _softmax_pv(scores, values, sink):
    m = sink
    for s in scores:
        m = jnp.maximum(m, jnp.max(s, axis=-1, keepdims=True))
    den = jnp.exp(sink - m)
    acc = None
    for s, v in zip(scores, values):
        p = jnp.exp(s - m)
        den = den + jnp.sum(p, axis=-1, keepdims=True)
        pv = _dot(p.astype(BF16), v)
        acc = pv if acc is None else acc + pv
    return acc / den


def _attn_ctx_kernel(sink_ref, q_ref, kv_ref, o_ref):
    q = (q_ref[...] * ATTN_SCALE).astype(BF16)
    kv = kv_ref[...].astype(BF16)
    outs = []
    for h in range(N_HEADS):
        g = h // GQA_GROUP
        qh = q[:, HEAD_DIM * h:HEAD_DIM * (h + 1)]
        kh = kv[:, HEAD_DIM * g:HEAD_DIM * (g + 1)]
        vh = kv[:, KV_W + HEAD_DIM * g:KV_W + HEAD_DIM * (g + 1)]
        outs.append(_softmax_pv([_dot(qh, kh, _NT)], [vh], sink_ref[h]))
    o_ref[...] = jnp.concatenate(outs, axis=-1).astype(BF16)


def _attn_ctx_call(z, sink, n_seq, seq_len):
    return pl.pallas_call(
        _attn_ctx_kernel,
        grid=(n_seq,),
        in_specs=[
            pl.BlockSpec(memory_space=pltpu.SMEM),
            pl.BlockSpec((seq_len, ATT_W), lambda s: (s, COL_QKV // ATT_W)),
            pl.BlockSpec((seq_len, 2 * KV_W), lambda s: (s, (COL_QKV + ATT_W) // (2 * KV_W))),
        ],
        out_specs=pl.BlockSpec((seq_len, ATT_W), lambda s: (s, 0)),
        out_shape=jax.ShapeDtypeStruct((n_seq * seq_len, ATT_W), BF16),
        compiler_params=_cparams(("arbitrary",)),
        name="attn_ctx",
    )(sink, z, z)


def _rope(x, cos, sin_signed):
    n = x.shape[-1]
    lane = lax.broadcasted_iota(jnp.int32, x.shape, 1)
    up = pltpu.roll(x, n - 16, 1)
    dn = pltpu.roll(x, 16, 1)
    partner = jnp.where((lane & 31) < 16, up, dn)
    return x * cos + partner * sin_signed


def _attn_lat_kernel(sink_ref, q_ref, kvp_ref, kvc_ref, kvn_ref, ck_ref, cv_ref,
                     cosq_ref, sinq_ref, cosp_ref, sinp_ref, cosn_ref, sinn_ref, o_ref):
    qb = pl.program_id(1)
    nb = pl.num_programs(1)
    cq, sq = cosq_ref[...], sinq_ref[...]
    q = _rope(q_ref[...], jnp.concatenate([cq] * 4, axis=1), jnp.concatenate([sq] * 4, axis=1))
    q = (q * ATTN_SCALE).astype(BF16)
    kvp, kvc, kvn = kvp_ref[...], kvc_ref[...], kvn_ref[...]
    kp = _rope(kvp[:, :KV_W], cosp_ref[...], sinp_ref[...]).astype(BF16)
    kc = _rope(kvc[:, :KV_W], cq, sq).astype(BF16)
    kn = _rope(kvn[:, :KV_W], cosn_ref[...], sinn_ref[...]).astype(BF16)
    vp, vc, vn = (t[:, KV_W:].astype(BF16) for t in (kvp, kvc, kvn))
    ck = ck_ref[...].astype(BF16)
    cv = cv_ref[...].astype(BF16)
    qi = lax.broadcasted_iota(jnp.int32, (Q_BLOCK, Q_BLOCK), 0)
    kj = lax.broadcasted_iota(jnp.int32, (Q_BLOCK, Q_BLOCK), 1)
    mask_p = (kj >= qi) & (qb > 0)
    mask_n = (kj <= qi) & (qb < nb - 1)
    outs = []
    for h in range(N_HEADS):
        g = h // GQA_GROUP
        hs = slice(HEAD_DIM * h, HEAD_DIM * (h + 1))
        gs = slice(HEAD_DIM * g, HEAD_DIM * (g + 1))
        qh = q[:, hs]
        s_p = jnp.where(mask_p, _dot(qh, kp[:, gs], _NT), NEG_BIG)
        s_c = _dot(qh, kc[:, gs], _NT)
        s_n = jnp.where(mask_n, _dot(qh, kn[:, gs], _NT), NEG_BIG)
        s_x = _dot(qh, ck[:, gs], _NT)
        outs.append(_softmax_pv([s_p, s_c, s_n, s_x], [vp[:, gs], vc[:, gs], vn[:, gs], cv[:, gs]], sink_ref[h]))
    o_ref[...] = jnp.concatenate(outs, axis=-1).astype(BF16)


def _attn_lat_call(z, sink, cache_k, cache_v, cos_t, sin_t, n_seq, seq_len, row0):
    nb = seq_len // Q_BLOCK
    base = row0 // Q_BLOCK
    past = cache_k.shape[1]

    def rows(off):
        return lambda b, i: (base + b * nb + jnp.clip(i + off, 0, nb - 1))

    def kv_spec(off):
        r = rows(off)
        return pl.BlockSpec((Q_BLOCK, 2 * KV_W), lambda b, i: (r(b, i), (COL_QKV + ATT_W) // (2 * KV_W)))

    def tab_spec(off):
        return pl.BlockSpec((Q_BLOCK, KV_W), lambda b, i: (jnp.clip(i + off, 0, nb - 1), 0))

    r0 = rows(0)
    return pl.pallas_call(
        _attn_lat_kernel,
        grid=(n_seq, nb),
        in_specs=[
            pl.BlockSpec(memory_space=pltpu.SMEM),
            pl.BlockSpec((Q_BLOCK, ATT_W), lambda b, i: (r0(b, i), COL_QKV // ATT_W)),
            kv_spec(-1), kv_spec(0), kv_spec(1),
            pl.BlockSpec((None, past, KV_W), lambda b, i: (b, 0, 0)),
            pl.BlockSpec((None, past, KV_W), lambda b, i: (b, 0, 0)),
            tab_spec(0), tab_spec(0), tab_spec(-1), tab_spec(-1), tab_spec(1), tab_spec(1),
        ],
        out_specs=pl.BlockSpec((Q_BLOCK, ATT_W), lambda b, i: (b * nb + i, 0)),
        out_shape=jax.ShapeDtypeStruct((n_seq * seq_len, ATT_W), BF16),
        compiler_params=_cparams(("arbitrary", "arbitrary")),
        name="attn_lat",
    )(sink, z, z, z, z, cache_k, cache_v, cos_t, sin_t, cos_t, sin_t, cos_t, sin_t)


def _rope_tables(seq_len):
    half = HEAD_DIM // 2
    pos = np.arange(seq_len)
    inv_freq = 1.0 / (ROPE_THETA ** (np.arange(0, half, 2, dtype=np.float32) / half))
    inv_freq = inv_freq.astype(np.float32)

    def part(p):
        ang = (p.astype(np.float32)[:, None] * inv_freq[None, :]).astype(np.float32)
        c, s = np.cos(ang), np.sin(ang)
        return np.concatenate([c, c], axis=1), np.concatenate([-s, s], axis=1)

    c_r, s_r = part(pos // GRID_W)
    c_c, s_c = part(pos % GRID_W)
    cos = np.concatenate([c_r, c_c] * N_KV_HEADS, axis=1).astype(np.float32)
    sin = np.concatenate([s_r, s_c] * N_KV_HEADS, axis=1).astype(np.float32)
    return jnp.asarray(cos), jnp.asarray(sin)


def _rwkv_dir_inputs(zm, zl, d, w0, w2, a0, a2, kk_w, ka_w, ones_head):
    r = zm[:, :RWKV_W]
    kraw = zm[:, RWKV_W:2 * RWKV_W]
    v = zm[:, 2 * RWKV_W:]
    wl = zl[:, DECAY_LORA * d:DECAY_LORA * (d + 1)]
    al = zl[:, 2 * DECAY_LORA + ICLR_LORA * d:2 * DECAY_LORA + ICLR_LORA * (d + 1)]
    xw = w0 + _dot(jnp.tanh(wl).astype(BF16), w2.astype(BF16))
    ld = -EXP_NEG_HALF * _sigmoid(xw)
    a = _sigmoid(a0 + _dot(al.astype(BF16), a2.astype(BF16)))
    k = kraw * (1.0 + (a - 1.0) * ka_w)
    kkr = kraw * kk_w
    n2 = _dot_exact_rhs(kkr * kkr, ones_head)
    kk = kkr / jnp.maximum(jnp.sqrt(n2), 1e-12)
    return r, v, kk, ld, a, k


def _tri_masks(rev):
    c = CHUNK
    ti = lax.broadcasted_iota(jnp.int32, (c, c), 0)
    si = lax.broadcasted_iota(jnp.int32, (c, c), 1)
    incl = (si >= ti) if rev else (si <= ti)
    strict = (si > ti) if rev else (si < ti)
    return incl, strict, (si == ti).astype(F32)


def _chunk_prepare(r, v, kk, ld, a, k, rev):
    incl, _, _ = _tri_masks(rev)
    m_incl = jnp.where(incl, 1.0, 0.0).astype(BF16)
    ldh, ldl = _split(ld)
    cin = _dot(m_incl, ldh) + _dot(m_incl, ldl)
    tot = jnp.sum(ld, axis=0, keepdims=True)
    e_neg = jnp.exp(-cin)
    e_end = jnp.exp(tot - cin)
    bb = kk * a
    return dict(
        a_m=kk * jnp.exp(cin - ld), r_m=r * jnp.exp(cin),
        b_m=(bb * e_neg).astype(BF16), k_m=(k * e_neg).astype(BF16),
        b_end=(bb * e_end).astype(BF16), k_end=(k * e_end).astype(BF16),
        v=v, e_tot=jnp.exp(tot), rev=rev)


def _chunk_problems(probs):
    c = CHUNK
    n = len(probs)
    lane = lax.broadcasted_iota(jnp.int32, (1, PAIR), 1)
    head_masks = (lane < RWKV_HEAD, lane >= RWKV_HEAD)
    masks = {rev: _tri_masks(rev) for rev in (False, True)}
    heads = [(i, h) for i in range(n) for h in range(2)]
    a_h = {(i, h): jnp.where(head_masks[h], probs[i]["a_m"], 0.0).astype(BF16) for i, h in heads}
    r_h = {(i, h): jnp.where(head_masks[h], probs[i]["r_m"], 0.0).astype(BF16) for i, h in heads}
    v_h = {(i, h): jnp.where(head_masks[h], probs[i]["v"], 0.0).astype(BF16) for i, h in heads}
    lhs = [jnp.concatenate([a_h[i, 0], a_h[i, 1], r_h[i, 0], r_h[i, 1]], axis=0) for i in range(n)]
    xb = [_dot(lhs[i], probs[i]["b_m"], _NT) for i in range(n)]
    xk = [_dot(lhs[i], probs[i]["k_m"], _NT) for i in range(n)]
    m_ab, m_ak, m_rb, m_rk = {}, {}, {}, {}
    for i, h in heads:
        incl, strict, _ = masks[probs[i]["rev"]]
        m_ab[i, h] = jnp.where(strict, xb[i][c * h:c * (h + 1)], 0.0)
        m_ak[i, h] = jnp.where(strict, xk[i][c * h:c * (h + 1)], 0.0).astype(BF16)
        m_rb[i, h] = jnp.where(incl, xb[i][c * (2 + h):c * (3 + h)], 0.0).astype(BF16)
        m_rk[i, h] = jnp.where(incl, xk[i][c * (2 + h):c * (3 + h)], 0.0).astype(BF16)
    mak_v = {q: _dot(m_ak[q], v_h[q]) for q in heads}
    mrk_v = {q: _dot(m_rk[q], v_h[q]) for q in heads}
    eye = masks[False][2]
    t_inv = {q: eye - m_ab[q] for q in heads}
    lp = dict(m_ab)
    for _ in range(5):
        lpb = {q: lp[q].astype(BF16) for q in heads}
        lp = {q: _dot(lpb[q], lpb[q]) for q in heads}
        t_inv = {q: t_inv[q] + _dot(t_inv[q].astype(BF16), lp[q].astype(BF16)) for q in heads}
    wu = {q: _dot(t_inv[q].astype(BF16), jnp.concatenate([a_h[q], mak_v[q].astype(BF16)], axis=1)) for q in heads}
    ro = {q: _dot(m_rb[q], wu[q].astype(BF16)) for q in heads}
    ri = lax.broadcasted_iota(jnp.int32, (PAIR, PAIR), 0)
    ci = lax.broadcasted_iota(jnp.int32, (PAIR, PAIR), 1)
    same_head = (ri < RWKV_HEAD) == (ci < RWKV_HEAD)
    wu_cat = [(wu[i, 0] + wu[i, 1]).astype(BF16) for i in range(n)]
    bt_wu = [_dot(probs[i]["b_end"], wu_cat[i], _TN) for i in range(n)]
    kt_v = [_dot(probs[i]["k_end"], probs[i]["v"].astype(BF16), _TN) for i in range(n)]
    results = []
    for i in range(n):
        ro_sum = ro[i, 0] + ro[i, 1]
        r_eff = probs[i]["r_m"] - ro_sum[:, :PAIR]
        o_loc = mrk_v[i, 0] + mrk_v[i, 1] - ro_sum[:, PAIR:]
        g_t = jnp.where(same_head, jnp.where(ri == ci, probs[i]["e_tot"], 0.0) - bt_wu[i][:, :PAIR], 0.0)
        h_t = jnp.where(same_head, kt_v[i] - bt_wu[i][:, PAIR:], 0.0)
        zh, zl = _split(probs[i]["z"])
        both = jnp.concatenate([r_eff, g_t], axis=0)
        bh, bl = _split(both)
        prod = _dot(bh, zh) + (_dot(bh, zl) + _dot(bl, zh))
        results.append((prod[:c] + o_loc, prod[c:] + h_t))
    return results


def _rwkv_kernel(zmf_ref, zlf_ref, zmb_ref, zlb_ref, w0_ref, w2_ref, a0_ref, a2_ref, kkw_ref, kaw_ref,
                 ones_ref, s0f_ref, s0b_ref, of_ref, ob_ref, zf_ref, zb_ref, zf_scr, zb_scr):
    @pl.when(pl.program_id(1) == 0)
    def _():
        zf_scr[...] = s0f_ref[...]
        zb_scr[...] = s0b_ref[...]

    ones_head = ones_ref[...]
    dirs = ((zmf_ref, zlf_ref, zf_scr, of_ref), (zmb_ref, zlb_ref, zb_scr, ob_ref))
    probs = []
    for d, (zm_ref, zl_ref, z_scr, _) in enumerate(dirs):
        r, v, kk, ld, a, k = _rwkv_dir_inputs(zm_ref[...], zl_ref[...], d, w0_ref[d:d + 1, :], w2_ref[d],
                                              a0_ref[d:d + 1, :], a2_ref[d], kkw_ref[...], kaw_ref[...], ones_head)
        full = _chunk_prepare(r, v, kk, ld, a, k, d == 1)
        for p in range(N_PAIR):
            ps = slice(PAIR * p, PAIR * (p + 1))
            prob = {key: (val[:, ps] if hasattr(val, "shape") else val) for key, val in full.items()}
            prob["z"] = z_scr[p]
            probs.append(prob)
    results = _chunk_problems(probs)
    for d, (_, _, z_scr, o_ref) in enumerate(dirs):
        for p in range(N_PAIR):
            out, z_new = results[d * N_PAIR + p]
            o_ref[:, PAIR * p:PAIR * (p + 1)] = out
            z_scr[p] = z_new
    zf_ref[...] = zf_scr[...]
    zb_ref[...] = zb_scr[...]


def _rwkv_call(z, lw, s0f, s0b, n_seq, seq_len, row0):
    nc = seq_len // CHUNK
    base = row0 // CHUNK

    def fwd(s, c):
        return base + s * nc + c

    def bwd(s, c):
        return base + s * nc + (nc - 1 - c)

    def zm_spec(f):
        return pl.BlockSpec((CHUNK, 3 * RWKV_W), lambda s, c: (f(s, c), COL_RWKV // (3 * RWKV_W)))

    def zl_spec(f):
        return pl.BlockSpec((CHUNK, LORA_W), lambda s, c: (f(s, c), COL_LORA // LORA_W))

    def full(shape):
        return pl.BlockSpec(shape, lambda s, c: (0,) * len(shape))

    st_spec = pl.BlockSpec((None, N_PAIR, PAIR, PAIR), lambda s, c: (s, 0, 0, 0))
    o_shape = jax.ShapeDtypeStruct((n_seq * seq_len, RWKV_W), F32)
    st_shape = jax.ShapeDtypeStruct((n_seq, N_PAIR, PAIR, PAIR), F32)
    return pl.pallas_call(
        _rwkv_kernel,
        grid=(n_seq, nc),
        in_specs=[
            zm_spec(fwd), zl_spec(fwd), zm_spec(bwd), zl_spec(bwd),
            full((2, RWKV_W)), full((2, DECAY_LORA, RWKV_W)), full((2, RWKV_W)), full((2, ICLR_LORA, RWKV_W)),
            full((1, RWKV_W)), full((1, RWKV_W)), full((RWKV_W, RWKV_W)),
            st_spec, st_spec,
        ],
        out_specs=[
            pl.BlockSpec((CHUNK, RWKV_W), lambda s, c: (s * nc + c, 0)),
            pl.BlockSpec((CHUNK, RWKV_W), lambda s, c: (s * nc + (nc - 1 - c), 0)),
            st_spec, st_spec,
        ],
        out_shape=[o_shape, o_shape, st_shape, st_shape],
        scratch_shapes=[pltpu.VMEM((N_PAIR, PAIR, PAIR), F32), pltpu.VMEM((N_PAIR, PAIR, PAIR), F32)],
        compiler_params=_cparams(("arbitrary", "arbitrary")),
        name="rwkv_scan",
    )(z, z, z, z, lw["w0"], lw["w2"], lw["a0"], lw["a2"], lw["kk_w"], lw["ka_w"], lw["ones_head"], s0f, s0b)


def _state_to_z(s):
    n = s.shape[0]
    st = jnp.swapaxes(s, -1, -2).reshape(n, N_PAIR, 2, RWKV_HEAD, RWKV_HEAD)
    zero = jnp.zeros_like(st[:, :, 0])
    top = jnp.concatenate([st[:, :, 0], zero], axis=-1)
    bot = jnp.concatenate([zero, st[:, :, 1]], axis=-1)
    return jnp.concatenate([top, bot], axis=-2)


def _z_to_state(z):
    n = z.shape[0]
    h0 = z[:, :, :RWKV_HEAD, :RWKV_HEAD]
    h1 = z[:, :, RWKV_HEAD:, RWKV_HEAD:]
    st = jnp.stack([h0, h1], axis=2).reshape(n, 2 * N_PAIR, RWKV_HEAD, RWKV_HEAD)
    return jnp.swapaxes(st, -1, -2)


def _mix_kernel(tiles, x_ref, zc_ref, zcp_ref, zcn_ref, zm_ref, zl_ref, zg_ref, of_ref, ob_ref, ya_ref, mod_ref,
                convw_ref, a0_ref, a2_ref, g2_ref, kaw_ref, rkw_ref, lng_ref, lnb_ref, ones_ref, mean_ref,
                wa_ref, wb_ref, wc_ref, wo_ref, o_ref):
    n_ctx_tiles, per_ctx, per_lat = tiles
    i = pl.program_id(0)
    pos = jnp.where(i < n_ctx_tiles, i % per_ctx, (i - n_ctx_tiles) % per_lat)
    per = jnp.where(i < n_ctx_tiles, per_ctx, per_lat)
    tm = x_ref.shape[0]

    zc = zc_ref[...]
    u = zc[:, 2 * CONV_W:] * zc[:, :CONV_W]
    zp = zcp_ref[...]
    zn = zcn_ref[...]
    u_prev_row = jnp.where(pos > 0, zp[7:8, 2 * CONV_W:] * zp[7:8, :CONV_W], 0.0)
    u_next_row = jnp.where(pos < per - 1, zn[0:1, 2 * CONV_W:] * zn[0:1, :CONV_W], 0.0)
    row = lax.broadcasted_iota(jnp.int32, (tm, CONV_W), 0)
    u_prev = jnp.where(row == 0, u_prev_row, pltpu.roll(u, 1, 0))
    u_next = jnp.where(row == tm - 1, u_next_row, pltpu.roll(u, tm - 1, 0))
    cw = convw_ref[...]
    y_conv = zc[:, CONV_W:2 * CONV_W] * (cw[0:1, :] * u_prev + cw[1:2, :] * u + cw[2:3, :] * u_next)

    zm = zm_ref[...]
    zl = zl_ref[...]
    r = zm[:, :RWKV_W]
    kraw = zm[:, RWKV_W:2 * RWKV_W]
    v = zm[:, 2 * RWKV_W:]
    ones_head = ones_ref[...]
    mean_head = mean_ref[...]
    o = of_ref[...] + ob_ref[...]
    mu = _dot_exact_rhs(o, mean_head)
    dlt = o - mu
    var = _dot_exact_rhs(dlt * dlt, mean_head)
    y = dlt * lax.rsqrt(var + GN_EPS) * lng_ref[...] + lnb_ref[...]
    for d in range(2):
        al = zl[:, 2 * DECAY_LORA + ICLR_LORA * d:2 * DECAY_LORA + ICLR_LORA * (d + 1)]
        a = _sigmoid(a0_ref[d:d + 1, :] + _dot(al.astype(BF16), a2_ref[d].astype(BF16)))
        k = kraw * (1.0 + (a - 1.0) * kaw_ref[...])
        y = y + _dot_exact_rhs(r * k * rkw_ref[...], ones_head) * v
    g1 = zl[:, 2 * DECAY_LORA + 2 * ICLR_LORA:]
    y_rwkv = y * _dot(_sigmoid(g1).astype(BF16), g2_ref[...].astype(BF16))

    zg = zg_ref[...]
    merged = (_sigmoid(zg[:, :D_MODEL]) * _dot(y_conv.astype(BF16), wa_ref[...])
              + _sigmoid(zg[:, D_MODEL:2 * D_MODEL]) * _dot(ya_ref[...], wb_ref[...])
              + _sigmoid(zg[:, 2 * D_MODEL:]) * _dot(y_rwkv.astype(BF16), wc_ref[...]))
    o_ref[...] = x_ref[...] + mod_ref[2:3, :] * _dot(merged.astype(BF16), wo_ref[...])


def _mix_call(x, z, o_f, o_b, y_attn, mod, row_of_tile, lw, tiles):
    t, d = x.shape
    tm = TM_MIX
    nt = t // tm
    hb = tm // 8

    def rows(w, col):
        return pl.BlockSpec((tm, w), lambda i: (i, col // w))

    def full(shape):
        return pl.BlockSpec(shape, lambda i: (0,) * len(shape))

    in_specs = [
        rows(d, 0),
        rows(3 * CONV_W, COL_CONV),
        pl.BlockSpec((8, 3 * CONV_W), lambda i: (jnp.maximum(i * hb - 1, 0), COL_CONV // (3 * CONV_W))),
        pl.BlockSpec((8, 3 * CONV_W), lambda i: (jnp.minimum((i + 1) * hb, nt * hb - 1), COL_CONV // (3 * CONV_W))),
        rows(3 * RWKV_W, COL_RWKV),
        rows(LORA_W, COL_LORA),
        rows(3 * D_MODEL, COL_GATES),
        rows(RWKV_W, 0), rows(RWKV_W, 0), rows(ATT_W, 0),
        pl.BlockSpec((None, N_MOD, d), lambda i: (row_of_tile(i, tm), 0, 0)),
        full((3, CONV_W)), full((2, RWKV_W)), full((2, ICLR_LORA, RWKV_W)), full((GATE_LORA, RWKV_W)),
        full((1, RWKV_W)), full((1, RWKV_W)), full((1, RWKV_W)), full((1, RWKV_W)),
        full((RWKV_W, RWKV_W)), full((RWKV_W, RWKV_W)),
        full((CONV_W, d)), full((ATT_W, d)), full((RWKV_W, d)), full((d, d)),
    ]
    return pl.pallas_call(
        functools.partial(_mix_kernel, tiles),
        grid=(nt,),
        in_specs=in_specs,
        out_specs=rows(d, 0),
        out_shape=jax.ShapeDtypeStruct((t, d), F32),
        compiler_params=_cparams(("arbitrary",)),
        name="mix",
    )(x, z, z, z, z, z, z, o_f, o_b, y_attn, mod,
      lw["conv_w"], lw["a0"], lw["a2"], lw["g2"], lw["ka_w"], lw["rk_w"], lw["ln_g"], lw["ln_b"],
      lw["ones_head"], lw["mean_head"], lw["wa"], lw["wb"], lw["wc"], lw["wo"])


def _route_kernel(x_ref, mod_ref, g_ref, rwt_ref, rb_ref, tri_ref, h_ref, e_ref, gate_ref, rank_ref, cnt_ref, cnt_scr):
    @pl.when(pl.program_id(0) == 0)
    def _():
        cnt_scr[...] = jnp.zeros_like(cnt_scr)

    h2 = _rms_mod(x_ref[...], g_ref[...], mod_ref[4:5, :], mod_ref[3:4, :])
    h_ref[...] = h2.reshape(h_ref.shape)
    logits = _dot3(rwt_ref[...], h2, _NT) + rb_ref[...]
    ne, tm = logits.shape
    ex = lax.broadcasted_iota(jnp.int32, (ne, tm), 0)
    work = logits
    vals, hots = [], []
    for kq in range(TOP_K):
        m = jnp.max(work, axis=0, keepdims=True)
        idx = jnp.min(jnp.where(work == m, ex, ne), axis=0, keepdims=True)
        hot = ex == idx
        vals.append(m)
        hots.append(hot)
        e_ref[kq:kq + 1, :] = idx
        work = jnp.where(hot, -jnp.inf, work)
    exps = [jnp.exp(vk - vals[0]) for vk in vals]
    den = exps[0] + exps[1] + exps[2] + exps[3]
    chosen = jnp.where(hots[0] | hots[1] | hots[2] | hots[3], 1.0, 0.0)
    before = cnt_scr[:, 0:1] + _dot(chosen.astype(BF16), tri_ref[...])
    for kq in range(TOP_K):
        gate_ref[kq:kq + 1, :] = exps[kq] / den
        rank_ref[kq:kq + 1, :] = jnp.sum(jnp.where(hots[kq], before, 0.0), axis=0, keepdims=True).astype(jnp.int32)
    cnt_scr[...] = cnt_scr[...] + jnp.sum(chosen, axis=1, keepdims=True)
    cnt_ref[...] = cnt_scr[...].astype(jnp.int32)


def _route_call(x, mod, row_of_tile, g_norm, router_w, router_b):
    t, d = x.shape
    tm = TM_ROUTE
    ne = router_w.shape[1]
    tri = jnp.triu(jnp.ones((tm, tm), F32), 1).astype(BF16)

    def full(shape):
        return pl.BlockSpec(shape, lambda i: (0,) * len(shape))

    kt_spec = pl.BlockSpec((TOP_K, tm), lambda i: (0, i))
    return pl.pallas_call(
        _route_kernel,
        grid=(t // tm,),
        in_specs=[
            pl.BlockSpec((tm, d), lambda i: (i, 0)),
            pl.BlockSpec((None, N_MOD, d), lambda i: (row_of_tile(i, tm), 0, 0)),
            full((1, d)), full((ne, d)), full((ne, 1)), full((tm, tm)),
        ],
        out_specs=[pl.BlockSpec((tm, 1, d), lambda i: (i, 0, 0)), kt_spec, kt_spec, kt_spec, full((ne, 128))],
        out_shape=[
            jax.ShapeDtypeStruct((t, 1, d), F32),
            jax.ShapeDtypeStruct((TOP_K, t), jnp.int32),
            jax.ShapeDtypeStruct((TOP_K, t), F32),
            jax.ShapeDtypeStruct((TOP_K, t), jnp.int32),
            jax.ShapeDtypeStruct((ne, 128), jnp.int32),
        ],
        scratch_shapes=[pltpu.VMEM((ne, 128), F32)],
        compiler_params=_cparams(("arbitrary",)),
        name="route",
    )(x, mod, g_norm.reshape(1, d), router_w.T, router_b.reshape(ne, 1), tri)


def _row_copy_wait(buf_hbm, n_rows, sem):
    view = buf_hbm.at[pl.ds(0, n_rows)]
    pltpu.make_async_copy(view, view, sem).wait()


def _dispatch_kernel(dest_ref, h_hbm, rows_in_hbm, rows_hbm, sem):
    del rows_in_hbm
    tm = TM_DISPATCH
    t_all = h_hbm.shape[0]
    t0 = pl.program_id(0) * tm

    def body(t, carry):
        for kq in range(TOP_K):
            pltpu.make_async_copy(h_hbm.at[t0 + t], rows_hbm.at[dest_ref[kq * t_all + t0 + t]], sem).start()
        return carry

    lax.fori_loop(0, tm, body, 0)
    _row_copy_wait(rows_hbm, TOP_K * tm, sem)


def _dispatch_call(h3, dest_flat, npad):
    t, _, d = h3.shape
    grid_spec = pltpu.PrefetchScalarGridSpec(
        num_scalar_prefetch=1,
        grid=(t // TM_DISPATCH,),
        in_specs=[pl.BlockSpec(memory_space=pl.ANY), pl.BlockSpec(memory_space=pl.ANY)],
        out_specs=pl.BlockSpec(memory_space=pl.ANY),
        scratch_shapes=[pltpu.SemaphoreType.DMA(())],
    )
    return pl.pallas_call(
        _dispatch_kernel,
        grid_spec=grid_spec,
        out_shape=jax.ShapeDtypeStruct((npad, 1, d), F32),
        input_output_aliases={2: 0},
        compiler_params=_cparams(("arbitrary",)),
        name="moe_dispatch",
    )(dest_flat, h3, jnp.zeros((npad, 1, d), F32))


def _moe_kernel(be_ref, first_ref, nused_ref, x_ref, w1_ref, b1_ref, w2_ref, b2_ref, o_ref, w1_scr, w2_scr, x_scr):
    i = pl.program_id(0)

    @pl.when(first_ref[i] == 1)
    def _():
        w1_scr[...] = w1_ref[...].astype(BF16)
        w2_scr[...] = w2_ref[...].astype(BF16)

    @pl.when(i < nused_ref[0])
    def _():
        x_scr[...] = x_ref[...].reshape(x_scr.shape)
        hm = _dot(x_scr[...].astype(BF16), w1_scr[...]) + b1_ref[...]
        glu = jnp.minimum(hm[:, :D_EXPERT], SWIGLU_LIMIT)
        lin = jnp.clip(hm[:, D_EXPERT:], -SWIGLU_LIMIT, SWIGLU_LIMIT)
        act = glu * _sigmoid(SWIGLU_ALPHA * glu) * (lin + 1.0)
        y = _dot(act.astype(BF16), w2_scr[...]) + b2_ref[...]
        o_ref[...] = y.reshape(o_ref.shape)

    @pl.when(i >= nused_ref[0])
    def _():
        o_ref[...] = jnp.zeros_like(o_ref)


def _moe_call(rows, blk_e, blk_first, n_used, w1, b1, w2, b2):
    npad, _, d = rows.shape
    tm = TM_MOE
    ne, _, dh2 = w1.shape
    de = w2.shape[1]
    row_spec = pl.BlockSpec((tm, 1, d), lambda i, be, fi, nu: (i, 0, 0))
    grid_spec = pltpu.PrefetchScalarGridSpec(
        num_scalar_prefetch=3,
        grid=(npad // tm,),
        in_specs=[
            row_spec,
            pl.BlockSpec((None, d, dh2), lambda i, be, fi, nu: (be[i], 0, 0)),
            pl.BlockSpec((None, 1, dh2), lambda i, be, fi, nu: (be[i], 0, 0)),
            pl.BlockSpec((None, de, d), lambda i, be, fi, nu: (be[i], 0, 0)),
            pl.BlockSpec((None, 1, d), lambda i, be, fi, nu: (be[i], 0, 0)),
        ],
        out_specs=row_spec,
        scratch_shapes=[pltpu.VMEM((d, dh2), BF16), pltpu.VMEM((de, d), BF16), pltpu.VMEM((tm, d), F32)],
    )
    return pl.pallas_call(
        _moe_kernel,
        grid_spec=grid_spec,
        out_shape=jax.ShapeDtypeStruct((npad, 1, d), F32),
        compiler_params=_cparams(("arbitrary",)),
        name="moe",
    )(blk_e, blk_first, n_used, rows, w1, b1.reshape(ne, 1, dh2), w2, b2.reshape(ne, 1, d))


def _combine_kernel(t_all, dest_ref, y_hbm, gate_ref, o_ref, buf, y_scr, sem):
    tm = TM_COMBINE
    t0 = pl.program_id(0) * tm

    def body(t, carry):
        for kq in range(TOP_K):
            pltpu.make_async_copy(y_hbm.at[dest_ref[kq * t_all + t0 + t]], buf.at[kq * tm + t], sem).start()
        return carry

    lax.fori_loop(0, tm, body, 0)
    pltpu.make_async_copy(y_hbm.at[pl.ds(0, TOP_K * tm)], buf, sem).wait()
    y_scr[...] = buf[...].reshape(y_scr.shape)
    gate = gate_ref[...]
    acc = gate[:, 0:1] * y_scr[0:tm, :]
    for kq in range(1, TOP_K):
        acc = acc + gate[:, kq:kq + 1] * y_scr[kq * tm:(kq + 1) * tm, :]
    o_ref[...] = acc


def _combine_call(y_rows, dest_flat, gate_tk):
    t = gate_tk.shape[0]
    d = y_rows.shape[-1]
    tm = TM_COMBINE
    grid_spec = pltpu.PrefetchScalarGridSpec(
        num_scalar_prefetch=1,
        grid=(t // tm,),
        in_specs=[pl.BlockSpec(memory_space=pl.ANY), pl.BlockSpec((tm, TOP_K), lambda i, dr: (i, 0))],
        out_specs=pl.BlockSpec((tm, d), lambda i, dr: (i, 0)),
        scratch_shapes=[pltpu.VMEM((TOP_K * tm, 1, d), F32), pltpu.VMEM((TOP_K * tm, d), F32),
                        pltpu.SemaphoreType.DMA(())],
    )
    return pl.pallas_call(
        functools.partial(_combine_kernel, t),
        grid_spec=grid_spec,
        out_shape=jax.ShapeDtypeStruct((t, d), F32),
        compiler_params=_cparams(("arbitrary",)),
        name="moe_combine",
    )(dest_flat, y_rows, gate_tk)


def _moe_layer(h3, e_t, gate_t, rank_t, counts, w1, b1, w2, b2):
    t, _, d = h3.shape
    tm = TM_MOE
    ne = w1.shape[0]
    n_assign = t * TOP_K
    n_blk = -(-(n_assign + ne * (tm - 1)) // tm)
    npad = n_blk * tm
    padded = (counts + tm - 1) // tm * tm
    pends = jnp.cumsum(padded)
    pstarts = pends - padded
    dest_flat = (pstarts[e_t] + rank_t).reshape(-1)
    n_used = (pends[-1] // tm).astype(jnp.int32)
    blk = jnp.arange(n_blk, dtype=jnp.int32)
    blk_e = jnp.minimum(jnp.searchsorted(pends, blk * tm, side="right"), ne - 1).astype(jnp.int32)
    blk_e = jnp.where(blk < n_used, blk_e, blk_e[jnp.maximum(n_used - 1, 0)])
    blk_first = jnp.concatenate([jnp.ones((1,), jnp.int32), (blk_e[1:] != blk_e[:-1]).astype(jnp.int32)])
    rows = _dispatch_call(h3, dest_flat, npad)
    y_rows = _moe_call(rows, blk_e, blk_first, n_used.reshape(1), w1, b1, w2, b2)
    return _combine_call(y_rows, dest_flat, gate_t.T)


def _final_kernel(x_ref, moe_ref, mod_ref, g_ref, o_ref):
    x = x_ref[...] + mod_ref[5:6, :] * moe_ref[...]
    ms = jnp.mean(x * x, axis=-1, keepdims=True)
    o_ref[...] = x * lax.rsqrt(ms + RMS_EPS) * g_ref[...]


def _final_call(x, moe, mod, row_of_tile, g_final):
    t, d = x.shape
    tm = TM_IN
    spec = pl.BlockSpec((tm, d), lambda i: (i, 0))
    return pl.pallas_call(
        _final_kernel,
        grid=(t // tm,),
        in_specs=[spec, spec, pl.BlockSpec((None, N_MOD, d), lambda i: (row_of_tile(i, tm), 0, 0)),
                  pl.BlockSpec((1, d), lambda i: (0, 0))],
        out_specs=spec,
        out_shape=jax.ShapeDtypeStruct((t, d), F32),
        compiler_params=_cparams(("arbitrary",)),
        name="final_norm",
    )(x, moe, mod, g_final.reshape(1, d))


def _permute_cols(w):
    o = _SRC_OFF
    return jnp.concatenate([
        w[..., o["rwkv"]:o["lora"]], w[..., o["conv"]:o["qkv"]], w[..., o["gates"]:],
        w[..., o["qkv"]:o["rwkv"]], w[..., o["lora"]:o["gates"]]], axis=-1)


def kernel(x_prompt, x_sample, cache_k, cache_v, state_rwkv_fwd, state_rwkv_bwd, c, c_ctx, w_ada, b_ada, g_norm1, g_norm2, w_in, conv_w, attn_sink, rwkv_w0, rwkv_w2, rwkv_a0, rwkv_a2, rwkv_g2, rwkv_k_k, rwkv_k_a, rwkv_r_k, rwkv_ln_g, rwkv_ln_b, w_branch_conv, w_branch_attn, w_branch_rwkv, w_out, router_w, router_b, moe_w1, moe_b1, moe_w2, moe_b2, g_final):
    bc, lc, d = x_prompt.shape
    bl, tl, _ = x_sample.shape
    depth = w_in.shape[0]
    n_ctx = bc * lc
    t_all = n_ctx + bl * tl
    assert lc % TM_MIX == 0 and tl % TM_IN == 0 and n_ctx % TM_IN == 0 and t_all % TM_ROUTE == 0

    def row_of_tile(i, tm):
        return jnp.where(i < n_ctx // tm, 0, 1 + (i - n_ctx // tm) // (tl // tm))

    x = jnp.concatenate([x_prompt.reshape(n_ctx, d), x_sample.reshape(bl * tl, d)], axis=0)
    n_cond = -(-(1 + bl) // 8) * 8
    cond = jnp.zeros((n_cond, d), F32).at[0].set(c_ctx).at[1:1 + bl].set(c)
    mods = _ada_call(cond, w_ada, b_ada).reshape(depth, n_cond, N_MOD, d)

    w_in_p = _permute_cols(w_in).astype(BF16)
    head_id = np.arange(RWKV_W) // RWKV_HEAD
    ones_head = jnp.asarray((head_id[:, None] == head_id[None, :]).astype(np.float32), BF16)
    mean_head = (ones_head.astype(F32) / RWKV_HEAD).astype(BF16)
    cos_t, sin_t = _rope_tables(tl)
    zeros_state = jnp.zeros((bc, 2 * N_PAIR, RWKV_HEAD, RWKV_HEAD), F32)
    tiles = (n_ctx // TM_MIX, lc // TM_MIX, tl // TM_MIX)

    new_k, new_v, new_sf, new_sb = [], [], [], []
    res = None
    for l in range(depth):
        mod = mods[l]
        lw = dict(
            w0=rwkv_w0[l], w2=rwkv_w2[l], a0=rwkv_a0[l], a2=rwkv_a2[l], g2=rwkv_g2[l],
            kk_w=rwkv_k_k[l].reshape(1, -1), ka_w=rwkv_k_a[l].reshape(1, -1), rk_w=rwkv_r_k[l].reshape(1, -1),
            ln_g=rwkv_ln_g[l].reshape(1, -1), ln_b=rwkv_ln_b[l].reshape(1, -1), conv_w=conv_w[l],
            ones_head=ones_head, mean_head=mean_head,
            wa=w_branch_conv[l].astype(BF16), wb=w_branch_attn[l].astype(BF16), wc=w_branch_rwkv[l].astype(BF16),
            wo=w_out[l].astype(BF16),
        )
        if res is None:
            z = _in_call(x, w_in_p[l], g_norm1[l], mod, row_of_tile)
        else:
            z, x = _in_call(x, w_in_p[l], g_norm1[l], mod, row_of_tile, res=res)

        kv_ctx = z[:n_ctx, COL_QKV + ATT_W:COL_QKV + ATT_W + 2 * KV_W]
        new_k.append(kv_ctx[:, :KV_W].reshape(bc, lc, N_KV_HEADS, HEAD_DIM))
        new_v.append(kv_ctx[:, KV_W:].reshape(bc, lc, N_KV_HEADS, HEAD_DIM))

        ya_c = _attn_ctx_call(z, attn_sink[l], bc, lc)
        ya_l = _attn_lat_call(z, attn_sink[l], cache_k[:, l].reshape(bl, -1, KV_W), cache_v[:, l].reshape(bl, -1, KV_W),
                              cos_t, sin_t, bl, tl, n_ctx)
        y_attn = jnp.concatenate([ya_c, ya_l], axis=0)

        of_c, ob_c, zf_c, zb_c = _rwkv_call(z, lw, _state_to_z(zeros_state), _state_to_z(zeros_state), bc, lc, 0)
        of_l, ob_l, _, _ = _rwkv_call(z, lw, _state_to_z(state_rwkv_fwd[:, l].astype(F32)),
                                      _state_to_z(state_rwkv_bwd[:, l].astype(F32)), bl, tl, n_ctx)
        new_sf.append(_z_to_state(zf_c))
        new_sb.append(_z_to_state(zb_c))
        o_f = jnp.concatenate([of_c, of_l], axis=0)
        o_b = jnp.concatenate([ob_c, ob_l], axis=0)

        x_mid = _mix_call(x, z, o_f, o_b, y_attn, mod, row_of_tile, lw, tiles)
        h2, e_t, gate_t, rank_t, cnt = _route_call(x_mid, mod, row_of_tile, g_norm2[l], router_w[l], router_b[l])
        moe = _moe_layer(h2, e_t, gate_t, rank_t, cnt[:, 0], moe_w1[l], moe_b1[l], moe_w2[l], moe_b2[l])
        x = x_mid
        res = (moe, mod)

    y = _final_call(x, res[0], res[1], row_of_tile, g_final)
    y_prompt = y[:n_ctx].reshape(bc, lc, d)
    y_sample = y[n_ctx:].reshape(bl, tl, d)
    dt = x_prompt.dtype
    return (y_prompt, y_sample, jnp.stack(new_k, axis=1), jnp.stack(new_v, axis=1),
            jnp.stack(new_sf, axis=1).astype(dt), jnp.stack(new_sb, axis=1).astype(dt))
```

```python
import functools

import numpy as np
import jax
import jax.numpy as jnp
from jax import lax
from jax.experimental import pallas as pl
from jax.experimental.pallas import tpu as pltpu

F32 = jnp.float32
BF16 = jnp.bfloat16

D_MODEL = 1024
N_MOD = 6
RMS_EPS = 1e-6
CONV_W = 512
N_HEADS = 8
N_KV_HEADS = 2
GQA_GROUP = N_HEADS // N_KV_HEADS
HEAD_DIM = 64
ATT_W = N_HEADS * HEAD_DIM
KV_W = N_KV_HEADS * HEAD_DIM
WINDOW = 128
Q_BLOCK = 128
ATTN_SCALE = HEAD_DIM ** -0.5
ROPE_THETA = 10000.0
GRID_W = 64
RWKV_HEAD = 64
RWKV_W = 512
DECAY_LORA = 64
ICLR_LORA = 64
GATE_LORA = 128
GN_EPS = 64e-5
N_EXPERTS = 32
TOP_K = 4
D_EXPERT = 1024
SWIGLU_LIMIT = 7.0
SWIGLU_ALPHA = 1.702
P_TOTAL = 7296

_SRC_OFF = dict(conv=0, qkv=1536, rwkv=2304, lora=3840, gates=4224)
COL_RWKV, COL_CONV, COL_GATES, COL_QKV, COL_LORA = 0, 1536, 3072, 6144, 6912
LORA_W = 2 * DECAY_LORA + 2 * ICLR_LORA + GATE_LORA

CHUNK = 64
PAIR = 2 * RWKV_HEAD
N_PAIR = RWKV_W // PAIR
NEG_BIG = -1e30
EXP_NEG_HALF = float(np.exp(-0.5))

TM_IN = 512
TN_IN = 2432
TM_MIX = 256
TM_ROUTE = 512
TM_MOE = 256
TM_DISPATCH = 512
TM_COMBINE = 256
VMEM_LIMIT = 56 * 1024 * 1024


def _cparams(sem, vmem=VMEM_LIMIT):
    return pltpu.CompilerParams(dimension_semantics=sem, vmem_limit_bytes=vmem)


def _dot(a, b, dims=(((1,), (0,)), ((), ()))):
    return lax.dot_general(a, b, dims, preferred_element_type=F32)


_NT = (((1,), (1,)), ((), ()))
_TN = (((0,), (0,)), ((), ()))


def _split(x):
    hi = x.astype(BF16)
    lo = (x - hi.astype(F32)).astype(BF16)
    return hi, lo


def _dot3(a, b, dims=(((1,), (0,)), ((), ()))):
    ah, al = _split(a)
    bh, bl = _split(b)
    return _dot(ah, bh, dims) + (_dot(ah, bl, dims) + _dot(al, bh, dims))


def _dot_exact_rhs(a, b_bf16, dims=(((1,), (0,)), ((), ()))):
    ah, al = _split(a)
    return _dot(ah, b_bf16, dims) + _dot(al, b_bf16, dims)


def _sigmoid(x):
    return 1.0 / (1.0 + jnp.exp(-x))


def _rms_mod(x, g, scale, shift):
    ms = jnp.mean(x * x, axis=-1, keepdims=True)
    return (x * lax.rsqrt(ms + RMS_EPS) * g) * (1.0 + scale) + shift


def _ada_kernel(cond_ref, w_ref, b_ref, o_ref):
    c = cond_ref[...]
    s = c * _sigmoid(c)
    o_ref[...] = _dot(s.astype(BF16), w_ref[...].astype(BF16)) + b_ref[...]


def _ada_call(cond, w_ada, b_ada):
    depth, d, n = w_ada.shape
    r = cond.shape[0]
    tn = 1536
    return pl.pallas_call(
        _ada_kernel,
        grid=(depth, n // tn),
        in_specs=[
            pl.BlockSpec((r, d), lambda l, j: (0, 0)),
            pl.BlockSpec((None, d, tn), lambda l, j: (l, 0, j)),
            pl.BlockSpec((None, 1, tn), lambda l, j: (l, 0, j)),
        ],
        out_specs=pl.BlockSpec((None, r, tn), lambda l, j: (l, 0, j)),
        out_shape=jax.ShapeDtypeStruct((depth, r, n), F32),
        compiler_params=_cparams(("arbitrary", "arbitrary")),
        name="ada",
    )(cond, w_ada, b_ada.reshape(depth, 1, n))


def _in_kernel(has_res, *refs):
    if has_res:
        x_ref, moe_ref, modp_ref, mod_ref, g_ref, w_ref, z_ref, xo_ref, h_scr = refs
    else:
        x_ref, mod_ref, g_ref, w_ref, z_ref, h_scr = refs

    @pl.when(pl.program_id(1) == 0)
    def _():
        x = x_ref[...]
        if has_res:
            x = x + modp_ref[5:6, :] * moe_ref[...]
            xo_ref[...] = x
        h_scr[...] = _rms_mod(x, g_ref[...], mod_ref[1:2, :], mod_ref[0:1, :]).astype(BF16)

    z_ref[...] = _dot(h_scr[...], w_ref[...])


def _in_call(x, w_bf16, g_norm, mod, row_of_tile, res=None):
    t, d = x.shape
    n = w_bf16.shape[1]
    tm, tn = TM_IN, TN_IN
    mod_spec = pl.BlockSpec((None, N_MOD, d), lambda i, j: (row_of_tile(i, tm), 0, 0))
    x_spec = pl.BlockSpec((tm, d), lambda i, j: (i, 0))
    in_specs = [x_spec]
    args = [x]
    if res is not None:
        moe, mod_prev = res
        in_specs += [x_spec, mod_spec]
        args += [moe, mod_prev]
    in_specs += [mod_spec, pl.BlockSpec((1, d), lambda i, j: (0, 0)), pl.BlockSpec((d, tn), lambda i, j: (0, j))]
    args += [mod, g_norm.reshape(1, d), w_bf16]
    z_spec = pl.BlockSpec((tm, tn), lambda i, j: (i, j))
    z_shape = jax.ShapeDtypeStruct((t, n), F32)
    if res is not None:
        out_specs, out_shape = [z_spec, x_spec], [z_shape, jax.ShapeDtypeStruct((t, d), F32)]
    else:
        out_specs, out_shape = z_spec, z_shape
    return pl.pallas_call(
        functools.partial(_in_kernel, res is not None),
        grid=(t // tm, n // tn),
        in_specs=in_specs,
        out_specs=out_specs,
        out_shape=out_shape,
        scratch_shapes=[pltpu.VMEM((tm, d), BF16)],
        compiler_params=_cparams(("arbitrary", "arbitrary")),
        name="in_proj",
    )(*args)


def _softmax_pv(scores, values, sink):
    m = sink
    for s in scores:
        m = jnp.maximum(m, jnp.max(s, axis=-1, keepdims=True))
    den = jnp.exp(sink - m)
    acc = None
    for s, v in zip(scores, values):
        p = jnp.exp(s - m)
        den = den + jnp.sum(p, axis=-1, keepdims=True)
        pv = _dot(p.astype(BF16), v)
        acc = pv if acc is None else acc + pv
    return acc / den


def _attn_ctx_kernel(sink_ref, q_ref, kv_ref, o_ref):
    q = (q_ref[...] * ATTN_SCALE).astype(BF16)
    kv = kv_ref[...].astype(BF16)
    outs = []
    for h in range(N_HEADS):
        g = h // GQA_GROUP
        qh = q[:, HEAD_DIM * h:HEAD_DIM * (h + 1)]
        kh = kv[:, HEAD_DIM * g:HEAD_DIM * (g + 1)]
        vh = kv[:, KV_W + HEAD_DIM * g:KV_W + HEAD_DIM * (g + 1)]
        outs.append(_softmax_pv([_dot(qh, kh, _NT)], [vh], sink_ref[h]))
    o_ref[...] = jnp.concatenate(outs, axis=-1).astype(BF16)


def _attn_ctx_call(z, sink, n_seq, seq_len):
    return pl.pallas_call(
        _attn_ctx_kernel,
        grid=(n_seq,),
        in_specs=[
            pl.BlockSpec(memory_space=pltpu.SMEM),
            pl.BlockSpec((seq_len, ATT_W), lambda s: (s, COL_QKV // ATT_W)),
            pl.BlockSpec((seq_len, 2 * KV_W), lambda s: (s, (COL_QKV + ATT_W) // (2 * KV_W))),
        ],
        out_specs=pl.BlockSpec((seq_len, ATT_W), lambda s: (s, 0)),
        out_shape=jax.ShapeDtypeStruct((n_seq * seq_len, ATT_W), BF16),
        compiler_params=_cparams(("arbitrary",)),
        name="attn_ctx",
    )(sink, z, z)


def _rope(x, cos, sin_signed):
    n = x.shape[-1]
    lane = lax.broadcasted_iota(jnp.int32, x.shape, 1)
    up = pltpu.roll(x, n - 16, 1)
    dn = pltpu.roll(x, 16, 1)
    partner = jnp.where((lane & 31) < 16, up, dn)
    return x * cos + partner * sin_signed


def _attn_lat_kernel(sink_ref, q_ref, kvp_ref, kvc_ref, kvn_ref, ck_ref, cv_ref,
                     cosq_ref, sinq_ref, cosp_ref, sinp_ref, cosn_ref, sinn_ref, o_ref):
    qb = pl.program_id(1)
    nb = pl.num_programs(1)
    cq, sq = cosq_ref[...], sinq_ref[...]
    q = _rope(q_ref[...], jnp.concatenate([cq] * 4, axis=1), jnp.concatenate([sq] * 4, axis=1))
    q = (q * ATTN_SCALE).astype(BF16)
    kvp, kvc, kvn = kvp_ref[...], kvc_ref[...], kvn_ref[...]
    kp = _rope(kvp[:, :KV_W], cosp_ref[...], sinp_ref[...]).astype(BF16)
    kc = _rope(kvc[:, :KV_W], cq, sq).astype(BF16)
    kn = _rope(kvn[:, :KV_W], cosn_ref[...], sinn_ref[...]).astype(BF16)
    vp, vc, vn = (t[:, KV_W:].astype(BF16) for t in (kvp, kvc, kvn))
    ck = ck_ref[...].astype(BF16)
    cv = cv_ref[...].astype(BF16)
    qi = lax.broadcasted_iota(jnp.int32, (Q_BLOCK, Q_BLOCK), 0)
    kj = lax.broadcasted_iota(jnp.int32, (Q_BLOCK, Q_BLOCK), 1)
    mask_p = (kj >= qi) & (qb > 0)
    mask_n = (kj <= qi) & (qb < nb - 1)
    outs = []
    for h in range(N_HEADS):
        g = h // GQA_GROUP
        hs = slice(HEAD_DIM * h, HEAD_DIM * (h + 1))
        gs = slice(HEAD_DIM * g, HEAD_DIM * (g + 1))
        qh = q[:, hs]
        s_p = jnp.where(mask_p, _dot(qh, kp[:, gs], _NT), NEG_BIG)
        s_c = _dot(qh, kc[:, gs], _NT)
        s_n = jnp.where(mask_n, _dot(qh, kn[:, gs], _NT), NEG_BIG)
        s_x = _dot(qh, ck[:, gs], _NT)
        outs.append(_softmax_pv([s_p, s_c, s_n, s_x], [vp[:, gs], vc[:, gs], vn[:, gs], cv[:, gs]], sink_ref[h]))
    o_ref[...] = jnp.concatenate(outs, axis=-1).astype(BF16)


def _attn_lat_call(z, sink, cache_k, cache_v, cos_t, sin_t, n_seq, seq_len, row0):
    nb = seq_len // Q_BLOCK
    base = row0 // Q_BLOCK
    past = cache_k.shape[1]

    def rows(off):
        return lambda b, i: (base + b * nb + jnp.clip(i + off, 0, nb - 1))

    def kv_spec(off):
        r = rows(off)
        return pl.BlockSpec((Q_BLOCK, 2 * KV_W), lambda b, i: (r(b, i), (COL_QKV + ATT_W) // (2 * KV_W)))

    def tab_spec(off):
        return pl.BlockSpec((Q_BLOCK, KV_W), lambda b, i: (jnp.clip(i + off, 0, nb - 1), 0))

    r0 = rows(0)
    return pl.pallas_call(
        _attn_lat_kernel,
        grid=(n_seq, nb),
        in_specs=[
            pl.BlockSpec(memory_space=pltpu.SMEM),
            pl.BlockSpec((Q_BLOCK, ATT_W), lambda b, i: (r0(b, i), COL_QKV // ATT_W)),
            kv_spec(-1), kv_spec(0), kv_spec(1),
            pl.BlockSpec((None, past, KV_W), lambda b, i: (b, 0, 0)),
            pl.BlockSpec((None, past, KV_W), lambda b, i: (b, 0, 0)),
            tab_spec(0), tab_spec(0), tab_spec(-1), tab_spec(-1), tab_spec(1), tab_spec(1),
        ],
        out_specs=pl.BlockSpec((Q_BLOCK, ATT_W), lambda b, i: (b * nb + i, 0)),
        out_shape=jax.ShapeDtypeStruct((n_seq * seq_len, ATT_W), BF16),
        compiler_params=_cparams(("arbitrary", "arbitrary")),
        name="attn_lat",
    )(sink, z, z, z, z, cache_k, cache_v, cos_t, sin_t, cos_t, sin_t, cos_t, sin_t)


def _rope_tables(seq_len):
    half = HEAD_DIM // 2
    pos = np.arange(seq_len)
    inv_freq = 1.0 / (ROPE_THETA ** (np.arange(0, half, 2, dtype=np.float32) / half))
    inv_freq = inv_freq.astype(np.float32)

    def part(p):
        ang = (p.astype(np.float32)[:, None] * inv_freq[None, :]).astype(np.float32)
        c, s = np.cos(ang), np.sin(ang)
        return np.concatenate([c, c], axis=1), np.concatenate([-s, s], axis=1)

    c_r, s_r = part(pos // GRID_W)
    c_c, s_c = part(pos % GRID_W)
    cos = np.concatenate([c_r, c_c] * N_KV_HEADS, axis=1).astype(np.float32)
    sin = np.concatenate([s_r, s_c] * N_KV_HEADS, axis=1).astype(np.float32)
    return jnp.asarray(cos), jnp.asarray(sin)


def _rwkv_dir_inputs(zm, zl, d, w0, w2, a0, a2, kk_w, ka_w, ones_head):
    r = zm[:, :RWKV_W]
    kraw = zm[:, RWKV_W:2 * RWKV_W]
    v = zm[:, 2 * RWKV_W:]
    wl = zl[:, DECAY_LORA * d:DECAY_LORA * (d + 1)]
    al = zl[:, 2 * DECAY_LORA + ICLR_LORA * d:2 * DECAY_LORA + ICLR_LORA * (d + 1)]
    xw = w0 + _dot(jnp.tanh(wl).astype(BF16), w2.astype(BF16))
    ld = -EXP_NEG_HALF * _sigmoid(xw)
    a = _sigmoid(a0 + _dot(al.astype(BF16), a2.astype(BF16)))
    k = kraw * (1.0 + (a - 1.0) * ka_w)
    kkr = kraw * kk_w
    n2 = _dot_exact_rhs(kkr * kkr, ones_head)
    kk = kkr / jnp.maximum(jnp.sqrt(n2), 1e-12)
    return r, v, kk, ld, a, k


def _tri_masks(rev):
    c = CHUNK
    ti = lax.broadcasted_iota(jnp.int32, (c, c), 0)
    si = lax.broadcasted_iota(jnp.int32, (c, c), 1)
    incl = (si >= ti) if rev else (si <= ti)
    strict = (si > ti) if rev else (si < ti)
    return incl, strict, (si == ti).astype(F32)


def _chunk_prepare(r, v, kk, ld, a, k, rev):
    incl, _, _ = _tri_masks(rev)
    m_incl = jnp.where(incl, 1.0, 0.0).astype(BF16)
    ldh, ldl = _split(ld)
    cin = _dot(m_incl, ldh) + _dot(m_incl, ldl)
    tot = jnp.sum(ld, axis=0, keepdims=True)
    e_neg = jnp.exp(-cin)
    e_end = jnp.exp(tot - cin)
    bb = kk * a
    return dict(
        a_m=kk * jnp.exp(cin - ld), r_m=r * jnp.exp(cin),
        b_m=(bb * e_neg).astype(BF16), k_m=(k * e_neg).astype(BF16),
        b_end=(bb * e_end).astype(BF16), k_end=(k * e_end).astype(BF16),
        v=v, e_tot=jnp.exp(tot), rev=rev)


def _chunk_problems(probs):
    c = CHUNK
    n = len(probs)
    lane = lax.broadcasted_iota(jnp.int32, (1, PAIR), 1)
    head_masks = (lane < RWKV_HEAD, lane >= RWKV_HEAD)
    masks = {rev: _tri_masks(rev) for rev in (False, True)}
    heads = [(i, h) for i in range(n) for h in range(2)]
    a_h = {(i, h): jnp.where(head_masks[h], probs[i]["a_m"], 0.0).astype(BF16) for i, h in heads}
    r_h = {(i, h): jnp.where(head_masks[h], probs[i]["r_m"], 0.0).astype(BF16) for i, h in heads}
    v_h = {(i, h): jnp.where(head_masks[h], probs[i]["v"], 0.0).astype(BF16) for i, h in heads}
    lhs = [jnp.concatenate([a_h[i, 0], a_h[i, 1], r_h[i, 0], r_h[i, 1]], axis=0) for i in range(n)]
    xb = [_dot(lhs[i], probs[i]["b_m"], _NT) for i in range(n)]
    xk = [_dot(lhs[i], probs[i]["k_m"], _NT) for i in range(n)]
    m_ab, m_ak, m_rb, m_rk = {}, {}, {}, {}
    for i, h in heads:
        incl, strict, _ = masks[probs[i]["rev"]]
        m_ab[i, h] = jnp.where(strict, xb[i][c * h:c * (h + 1)], 0.0)
        m_ak[i, h] = jnp.where(strict, xk[i][c * h:c * (h + 1)], 0.0).astype(BF16)
        m_rb[i, h] = jnp.where(incl, xb[i][c * (2 + h):c * (3 + h)], 0.0).astype(BF16)
        m_rk[i, h] = jnp.where(incl, xk[i][c * (2 + h):c * (3 + h)], 0.0).astype(BF16)
    mak_v = {q: _dot(m_ak[q], v_h[q]) for q in heads}
    mrk_v = {q: _dot(m_rk[q], v_h[q]) for q in heads}
    eye = masks[False][2]
    t_inv = {q: eye - m_ab[q] for q in heads}
    lp = dict(m_ab)
    for _ in range(5):
        lpb = {q: lp[q].astype(BF16) for q in heads}
        lp = {q: _dot(lpb[q], lpb[q]) for q in heads}
        t_inv = {q: t_inv[q] + _dot(t_inv[q].astype(BF16), lp[q].astype(BF16)) for q in heads}
    wu = {q: _dot(t_inv[q].astype(BF16), jnp.concatenate([a_h[q], mak_v[q].astype(BF16)], axis=1)) for q in heads}
    ro = {q: _dot(m_rb[q], wu[q].astype(BF16)) for q in heads}
    ri = lax.broadcasted_iota(jnp.int32, (PAIR, PAIR), 0)
    ci = lax.broadcasted_iota(jnp.int32, (PAIR, PAIR), 1)
    same_head = (ri < RWKV_HEAD) == (ci < RWKV_HEAD)
    wu_cat = [(wu[i, 0] + wu[i, 1]).astype(BF16) for i in range(n)]
    bt_wu = [_dot(probs[i]["b_end"], wu_cat[i], _TN) for i in range(n)]
    kt_v = [_dot(probs[i]["k_end"], probs[i]["v"].astype(BF16), _TN) for i in range(n)]
    results = []
    for i in range(n):
        ro_sum = ro[i, 0] + ro[i, 1]
        r_eff = probs[i]["r_m"] - ro_sum[:, :PAIR]
        o_loc = mrk_v[i, 0] + mrk_v[i, 1] - ro_sum[:, PAIR:]
        g_t = jnp.where(same_head, jnp.where(ri == ci, probs[i]["e_tot"], 0.0) - bt_wu[i][:, :PAIR], 0.0)
        h_t = jnp.where(same_head, kt_v[i] - bt_wu[i][:, PAIR:], 0.0)
        zh, zl = _split(probs[i]["z"])
        both = jnp.concatenate([r_eff, g_t], axis=0)
        bh, bl = _split(both)
        prod = _dot(bh, zh) + (_dot(bh, zl) + _dot(bl, zh))
        results.append((prod[:c] + o_loc, prod[c:] + h_t))
    return results


def _rwkv_kernel(zmf_ref, zlf_ref, zmb_ref, zlb_ref, w0_ref, w2_ref, a0_ref, a2_ref, kkw_ref, kaw_ref,
                 ones_ref, s0f_ref, s0b_ref, of_ref, ob_ref, zf_ref, zb_ref, zf_scr, zb_scr):
    @pl.when(pl.program_id(1) == 0)
    def _():
        zf_scr[...] = s0f_ref[...]
        zb_scr[...] = s0b_ref[...]

    ones_head = ones_ref[...]
    dirs = ((zmf_ref, zlf_ref, zf_scr, of_ref), (zmb_ref, zlb_ref, zb_scr, ob_ref))
    probs = []
    for d, (zm_ref, zl_ref, z_scr, _) in enumerate(dirs):
        r, v, kk, ld, a, k = _rwkv_dir_inputs(zm_ref[...], zl_ref[...], d, w0_ref[d:d + 1, :], w2_ref[d],
                                              a0_ref[d:d + 1, :], a2_ref[d], kkw_ref[...], kaw_ref[...], ones_head)
        full = _chunk_prepare(r, v, kk, ld, a, k, d == 1)
        for p in range(N_PAIR):
            ps = slice(PAIR * p, PAIR * (p + 1))
            prob = {key: (val[:, ps] if hasattr(val, "shape") else val) for key, val in full.items()}
            prob["z"] = z_scr[p]
            probs.append(prob)
    results = _chunk_problems(probs)
    for d, (_, _, z_scr, o_ref) in enumerate(dirs):
        for p in range(N_PAIR):
            out, z_new = results[d * N_PAIR + p]
            o_ref[:, PAIR * p:PAIR * (p + 1)] = out
            z_scr[p] = z_new
    zf_ref[...] = zf_scr[...]
    zb_ref[...] = zb_scr[...]


def _rwkv_call(z, lw, s0f, s0b, n_seq, seq_len, row0):
    nc = seq_len // CHUNK
    base = row0 // CHUNK

    def fwd(s, c):
        return base + s * nc + c

    def bwd(s, c):
        return base + s * nc + (nc - 1 - c)

    def zm_spec(f):
        return pl.BlockSpec((CHUNK, 3 * RWKV_W), lambda s, c: (f(s, c), COL_RWKV // (3 * RWKV_W)))

    def zl_spec(f):
        return pl.BlockSpec((CHUNK, LORA_W), lambda s, c: (f(s, c), COL_LORA // LORA_W))

    def full(shape):
        return pl.BlockSpec(shape, lambda s, c: (0,) * len(shape))

    st_spec = pl.BlockSpec((None, N_PAIR, PAIR, PAIR), lambda s, c: (s, 0, 0, 0))
    o_shape = jax.ShapeDtypeStruct((n_seq * seq_len, RWKV_W), F32)
    st_shape = jax.ShapeDtypeStruct((n_seq, N_PAIR, PAIR, PAIR), F32)
    return pl.pallas_call(
        _rwkv_kernel,
        grid=(n_seq, nc),
        in_specs=[
            zm_spec(fwd), zl_spec(fwd), zm_spec(bwd), zl_spec(bwd),
            full((2, RWKV_W)), full((2, DECAY_LORA, RWKV_W)), full((2, RWKV_W)), full((2, ICLR_LORA, RWKV_W)),
            full((1, RWKV_W)), full((1, RWKV_W)), full((RWKV_W, RWKV_W)),
            st_spec, st_spec,
        ],
        out_specs=[
            pl.BlockSpec((CHUNK, RWKV_W), lambda s, c: (s * nc + c, 0)),
            pl.BlockSpec((CHUNK, RWKV_W), lambda s, c: (s * nc + (nc - 1 - c), 0)),
            st_spec, st_spec,
        ],
        out_shape=[o_shape, o_shape, st_shape, st_shape],
        scratch_shapes=[pltpu.VMEM((N_PAIR, PAIR, PAIR), F32), pltpu.VMEM((N_PAIR, PAIR, PAIR), F32)],
        compiler_params=_cparams(("arbitrary", "arbitrary")),
        name="rwkv_scan",
    )(z, z, z, z, lw["w0"], lw["w2"], lw["a0"], lw["a2"], lw["kk_w"], lw["ka_w"], lw["ones_head"], s0f, s0b)


def _state_to_z(s):
    n = s.shape[0]
    st = jnp.swapaxes(s, -1, -2).reshape(n, N_PAIR, 2, RWKV_HEAD, RWKV_HEAD)
    zero = jnp.zeros_like(st[:, :, 0])
    top = jnp.concatenate([st[:, :, 0], zero], axis=-1)
    bot = jnp.concatenate([zero, st[:, :, 1]], axis=-1)
    return jnp.concatenate([top, bot], axis=-2)


def _z_to_state(z):
    n = z.shape[0]
    h0 = z[:, :, :RWKV_HEAD, :RWKV_HEAD]
    h1 = z[:, :, RWKV_HEAD:, RWKV_HEAD:]
    st = jnp.stack([h0, h1], axis=2).reshape(n, 2 * N_PAIR, RWKV_HEAD, RWKV_HEAD)
    return jnp.swapaxes(st, -1, -2)


def _mix_kernel(tiles, x_ref, zc_ref, zcp_ref, zcn_ref, zm_ref, zl_ref, zg_ref, of_ref, ob_ref, ya_ref, mod_ref,
                convw_ref, a0_ref, a2_ref, g2_ref, kaw_ref, rkw_ref, lng_ref, lnb_ref, ones_ref, mean_ref,
                wa_ref, wb_ref, wc_ref, wo_ref, o_ref):
    n_ctx_tiles, per_ctx, per_lat = tiles
    i = pl.program_id(0)
    pos = jnp.where(i < n_ctx_tiles, i % per_ctx, (i - n_ctx_tiles) % per_lat)
    per = jnp.where(i < n_ctx_tiles, per_ctx, per_lat)
    tm = x_ref.shape[0]

    zc = zc_ref[...]
    u = zc[:, 2 * CONV_W:] * zc[:, :CONV_W]
    zp = zcp_ref[...]
    zn = zcn_ref[...]
    u_prev_row = jnp.where(pos > 0, zp[7:8, 2 * CONV_W:] * zp[7:8, :CONV_W], 0.0)
    u_next_row = jnp.where(pos < per - 1, zn[0:1, 2 * CONV_W:] * zn[0:1, :CONV_W], 0.0)
    row = lax.broadcasted_iota(jnp.int32, (tm, CONV_W), 0)
    u_prev = jnp.where(row == 0, u_prev_row, pltpu.roll(u, 1, 0))
    u_next = jnp.where(row == tm - 1, u_next_row, pltpu.roll(u, tm - 1, 0))
    cw = convw_ref[...]
    y_conv = zc[:, CONV_W:2 * CONV_W] * (cw[0:1, :] * u_prev + cw[1:2, :] * u + cw[2:3, :] * u_next)

    zm = zm_ref[...]
    zl = zl_ref[...]
    r = zm[:, :RWKV_W]
    kraw = zm[:, RWKV_W:2 * RWKV_W]
    v = zm[:, 2 * RWKV_W:]
    ones_head = ones_ref[...]
    mean_head = mean_ref[...]
    o = of_ref[...] + ob_ref[...]
    mu = _dot_exact_rhs(o, mean_head)
    dlt = o - mu
    var = _dot_exact_rhs(dlt * dlt, mean_head)
    y = dlt * lax.rsqrt(var + GN_EPS) * lng_ref[...] + lnb_ref[...]
    for d in range(2):
        al = zl[:, 2 * DECAY_LORA + ICLR_LORA * d:2 * DECAY_LORA + ICLR_LORA * (d + 1)]
        a = _sigmoid(a0_ref[d:d + 1, :] + _dot(al.astype(BF16), a2_ref[d].astype(BF16)))
        k = kraw * (1.0 + (a - 1.0) * kaw_ref[...])
        y = y + _dot_exact_rhs(r * k * rkw_ref[...], ones_head) * v
    g1 = zl[:, 2 * DECAY_LORA + 2 * ICLR_LORA:]
    y_rwkv = y * _dot(_sigmoid(g1).astype(BF16), g2_ref[...].astype(BF16))

    zg = zg_ref[...]
    merged = (_sigmoid(zg[:, :D_MODEL]) * _dot(y_conv.astype(BF16), wa_ref[...])
              + _sigmoid(zg[:, D_MODEL:2 * D_MODEL]) * _dot(ya_ref[...], wb_ref[...])
              + _sigmoid(zg[:, 2 * D_MODEL:]) * _dot(y_rwkv.astype(BF16), wc_ref[...]))
    o_ref[...] = x_ref[...] + mod_ref[2:3, :] * _dot(merged.astype(BF16), wo_ref[...])


def _mix_call(x, z, o_f, o_b, y_attn, mod, row_of_tile, lw, tiles):
    t, d = x.shape
    tm = TM_MIX
    nt = t // tm
    hb = tm // 8

    def rows(w, col):
        return pl.BlockSpec((tm, w), lambda i: (i, col // w))

    def full(shape):
        return pl.BlockSpec(shape, lambda i: (0,) * len(shape))

    in_specs = [
        rows(d, 0),
        rows(3 * CONV_W, COL_CONV),
        pl.BlockSpec((8, 3 * CONV_W), lambda i: (jnp.maximum(i * hb - 1, 0), COL_CONV // (3 * CONV_W))),
        pl.BlockSpec((8, 3 * CONV_W), lambda i: (jnp.minimum((i + 1) * hb, nt * hb - 1), COL_CONV // (3 * CONV_W))),
        rows(3 * RWKV_W, COL_RWKV),
        rows(LORA_W, COL_LORA),
        rows(3 * D_MODEL, COL_GATES),
        rows(RWKV_W, 0), rows(RWKV_W, 0), rows(ATT_W, 0),
        pl.BlockSpec((None, N_MOD, d), lambda i: (row_of_tile(i, tm), 0, 0)),
        full((3, CONV_W)), full((2, RWKV_W)), full((2, ICLR_LORA, RWKV_W)), full((GATE_LORA, RWKV_W)),
        full((1, RWKV_W)), full((1, RWKV_W)), full((1, RWKV_W)), full((1, RWKV_W)),
        full((RWKV_W, RWKV_W)), full((RWKV_W, RWKV_W)),
        full((CONV_W, d)), full((ATT_W, d)), full((RWKV_W, d)), full((d, d)),
    ]
    return pl.pallas_call(
        functools.partial(_mix_kernel, tiles),
        grid=(nt,),
        in_specs=in_specs,
        out_specs=rows(d, 0),
        out_shape=jax.ShapeDtypeStruct((t, d), F32),
        compiler_params=_cparams(("arbitrary",)),
        name="mix",
    )(x, z, z, z, z, z, z, o_f, o_b, y_attn, mod,
      lw["conv_w"], lw["a0"], lw["a2"], lw["g2"], lw["ka_w"], lw["rk_w"], lw["ln_g"], lw["ln_b"],
      lw["ones_head"], lw["mean_head"], lw["wa"], lw["wb"], lw["wc"], lw["wo"])


def _route_kernel(x_ref, mod_ref, g_ref, rwt_ref, rb_ref, tri_ref, h_ref, e_ref, gate_ref, rank_ref, cnt_ref, cnt_scr):
    @pl.when(pl.program_id(0) == 0)
    def _():
        cnt_scr[...] = jnp.zeros_like(cnt_scr)

    h2 = _rms_mod(x_ref[...], g_ref[...], mod_ref[4:5, :], mod_ref[3:4, :])
    h_ref[...] = h2.reshape(h_ref.shape)
    logits = _dot3(rwt_ref[...], h2, _NT) + rb_ref[...]
    ne, tm = logits.shape
    ex = lax.broadcasted_iota(jnp.int32, (ne, tm), 0)
    work = logits
    vals, hots = [], []
    for kq in range(TOP_K):
        m = jnp.max(work, axis=0, keepdims=True)
        idx = jnp.min(jnp.where(work == m, ex, ne), axis=0, keepdims=True)
        hot = ex == idx
        vals.append(m)
        hots.append(hot)
        e_ref[kq:kq + 1, :] = idx
        work = jnp.where(hot, -jnp.inf, work)
    exps = [jnp.exp(vk - vals[0]) for vk in vals]
    den = exps[0] + exps[1] + exps[2] + exps[3]
    chosen = jnp.where(hots[0] | hots[1] | hots[2] | hots[3], 1.0, 0.0)
    before = cnt_scr[:, 0:1] + _dot(chosen.astype(BF16), tri_ref[...])
    for kq in range(TOP_K):
        gate_ref[kq:kq + 1, :] = exps[kq] / den
        rank_ref[kq:kq + 1, :] = jnp.sum(jnp.where(hots[kq], before, 0.0), axis=0, keepdims=True).astype(jnp.int32)
    cnt_scr[...] = cnt_scr[...] + jnp.sum(chosen, axis=1, keepdims=True)
    cnt_ref[...] = cnt_scr[...].astype(jnp.int32)


def _route_call(x, mod, row_of_tile, g_norm, router_w, router_b):
    t, d = x.shape
    tm = TM_ROUTE
    ne = router_w.shape[1]
    tri = jnp.triu(jnp.ones((tm, tm), F32), 1).astype(BF16)

    def full(shape):
        return pl.BlockSpec(shape, lambda i: (0,) * len(shape))

    kt_spec = pl.BlockSpec((TOP_K, tm), lambda i: (0, i))
    return pl.pallas_call(
        _route_kernel,
        grid=(t // tm,),
        in_specs=[
            pl.BlockSpec((tm, d), lambda i: (i, 0)),
            pl.BlockSpec((None, N_MOD, d), lambda i: (row_of_tile(i, tm), 0, 0)),
            full((1, d)), full((ne, d)), full((ne, 1)), full((tm, tm)),
        ],
        out_specs=[pl.BlockSpec((tm, 1, d), lambda i: (i, 0, 0)), kt_spec, kt_spec, kt_spec, full((ne, 128))],
        out_shape=[
            jax.ShapeDtypeStruct((t, 1, d), F32),
            jax.ShapeDtypeStruct((TOP_K, t), jnp.int32),
            jax.ShapeDtypeStruct((TOP_K, t), F32),
            jax.ShapeDtypeStruct((TOP_K, t), jnp.int32),
            jax.ShapeDtypeStruct((ne, 128), jnp.int32),
        ],
        scratch_shapes=[pltpu.VMEM((ne, 128), F32)],
        compiler_params=_cparams(("arbitrary",)),
        name="route",
    )(x, mod, g_norm.reshape(1, d), router_w.T, router_b.reshape(ne, 1), tri)


def _row_copy_wait(buf_hbm, n_rows, sem):
    view = buf_hbm.at[pl.ds(0, n_rows)]
    pltpu.make_async_copy(view, view, sem).wait()


def _dispatch_kernel(t_all, dest_ref, h_ref, rows_in_hbm, rows_hbm, sem):
    del rows_in_hbm
    tm = TM_DISPATCH
    t0 = pl.program_id(0) * tm

    def body(t, carry):
        for kq in range(TOP_K):
            pltpu.make_async_copy(h_ref.at[t], rows_hbm.at[dest_ref[kq * t_all + t0 + t]], sem).start()
        return carry

    lax.fori_loop(0, tm, body, 0)
    _row_copy_wait(rows_hbm, TOP_K * tm, sem)


def _dispatch_call(h3, dest_flat, npad):
    t, _, d = h3.shape
    grid_spec = pltpu.PrefetchScalarGridSpec(
        num_scalar_prefetch=1,
        grid=(t // TM_DISPATCH,),
        in_specs=[pl.BlockSpec((TM_DISPATCH, 1, d), lambda i, dr: (i, 0, 0)), pl.BlockSpec(memory_space=pl.ANY)],
        out_specs=pl.BlockSpec(memory_space=pl.ANY),
        scratch_shapes=[pltpu.SemaphoreType.DMA(())],
    )
    return pl.pallas_call(
        functools.partial(_dispatch_kernel, t),
        grid_spec=grid_spec,
        out_shape=jax.ShapeDtypeStruct((npad, 1, d), F32),
        input_output_aliases={2: 0},
        compiler_params=_cparams(("arbitrary",)),
        name="moe_dispatch",
    )(dest_flat, h3, jnp.zeros((npad, 1, d), F32))


def _moe_kernel(be_ref, first_ref, nused_ref, x_ref, w1_ref, b1_ref, w2_ref, b2_ref, o_ref, w1_scr, w2_scr, x_scr):
    i = pl.program_id(0)

    @pl.when(first_ref[i] == 1)
    def _():
        w1_scr[...] = w1_ref[...].astype(BF16)
        w2_scr[...] = w2_ref[...].astype(BF16)

    @pl.when(i < nused_ref[0])
    def _():
        x_scr[...] = x_ref[...].reshape(x_scr.shape)
        hm = _dot(x_scr[...].astype(BF16), w1_scr[...]) + b1_ref[...]
        glu = jnp.minimum(hm[:, :D_EXPERT], SWIGLU_LIMIT)
        lin = jnp.clip(hm[:, D_EXPERT:], -SWIGLU_LIMIT, SWIGLU_LIMIT)
        act = glu * _sigmoid(SWIGLU_ALPHA * glu) * (lin + 1.0)
        y = _dot(act.astype(BF16), w2_scr[...]) + b2_ref[...]
        o_ref[...] = y.reshape(o_ref.shape)

    @pl.when(i >= nused_ref[0])
    def _():
        o_ref[...] = jnp.zeros_like(o_ref)


def _moe_call(rows, blk_e, blk_first, n_used, w1, b1, w2, b2):
    npad, _, d = rows.shape
    tm = TM_MOE
    ne, _, dh2 = w1.shape
    de = w2.shape[1]
    row_spec = pl.BlockSpec((tm, 1, d), lambda i, be, fi, nu: (i, 0, 0))
    grid_spec = pltpu.PrefetchScalarGridSpec(
        num_scalar_prefetch=3,
        grid=(npad // tm,),
        in_specs=[
            row_spec,
            pl.BlockSpec((None, d, dh2), lambda i, be, fi, nu: (be[i], 0, 0)),
            pl.BlockSpec((None, 1, dh2), lambda i, be, fi, nu: (be[i], 0, 0)),
            pl.BlockSpec((None, de, d), lambda i, be, fi, nu: (be[i], 0, 0)),
            pl.BlockSpec((None, 1, d), lambda i, be, fi, nu: (be[i], 0, 0)),
        ],
        out_specs=row_spec,
        scratch_shapes=[pltpu.VMEM((d, dh2), BF16), pltpu.VMEM((de, d), BF16), pltpu.VMEM((tm, d), F32)],
    )
    return pl.pallas_call(
        _moe_kernel,
        grid_spec=grid_spec,
        out_shape=jax.ShapeDtypeStruct((npad, 1, d), F32),
        compiler_params=_cparams(("arbitrary",)),
        name="moe",
    )(blk_e, blk_first, n_used, rows, w1, b1.reshape(ne, 1, dh2), w2, b2.reshape(ne, 1, d))


def _combine_kernel(t_all, dest_ref, y_hbm, gate_ref, o_ref, buf, y_scr, sem):
    tm = TM_COMBINE
    t0 = pl.program_id(0) * tm

    def body(t, carry):
        for kq in range(TOP_K):
            pltpu.make_async_copy(y_hbm.at[dest_ref[kq * t_all + t0 + t]], buf.at[kq * tm + t], sem).start()
        return carry

    lax.fori_loop(0, tm, body, 0)
    pltpu.make_async_copy(y_hbm.at[pl.ds(0, TOP_K * tm)], buf, sem).wait()
    y_scr[...] = buf[...].reshape(y_scr.shape)
    gate = gate_ref[...]
    acc = gate[:, 0:1] * y_scr[0:tm, :]
    for kq in range(1, TOP_K):
        acc = acc + gate[:, kq:kq + 1] * y_scr[kq * tm:(kq + 1) * tm, :]
    o_ref[...] = acc


def _combine_call(y_rows, dest_flat, gate_tk):
    t = gate_tk.shape[0]
    d = y_rows.shape[-1]
    tm = TM_COMBINE
    grid_spec = pltpu.PrefetchScalarGridSpec(
        num_scalar_prefetch=1,
        grid=(t // tm,),
        in_specs=[pl.BlockSpec(memory_space=pl.ANY), pl.BlockSpec((tm, TOP_K), lambda i, dr: (i, 0))],
        out_specs=pl.BlockSpec((tm, d), lambda i, dr: (i, 0)),
        scratch_shapes=[pltpu.VMEM((TOP_K * tm, 1, d), F32), pltpu.VMEM((TOP_K * tm, d), F32),
                        pltpu.SemaphoreType.DMA(())],
    )
    return pl.pallas_call(
        functools.partial(_combine_kernel, t),
        grid_spec=grid_spec,
        out_shape=jax.ShapeDtypeStruct((t, d), F32),
        compiler_params=_cparams(("arbitrary",)),
        name="moe_combine",
    )(dest_flat, y_rows, gate_tk)


def _moe_layer(h3, e_t, gate_t, rank_t, counts, w1, b1, w2, b2):
    t, _, d = h3.shape
    tm = TM_MOE
    ne = w1.shape[0]
    n_assign = t * TOP_K
    n_blk = -(-(n_assign + ne * (tm - 1)) // tm)
    npad = n_blk * tm
    padded = (counts + tm - 1) // tm * tm
    pends = jnp.cumsum(padded)
    pstarts = pends - padded
    start_of = jnp.sum(jnp.where(e_t[..., None] == jnp.arange(ne, dtype=jnp.int32), pstarts.astype(jnp.int32), 0), axis=-1)
    dest_flat = (start_of + rank_t).reshape(-1)
    n_used = (pends[-1] // tm).astype(jnp.int32)
    blk = jnp.arange(n_blk, dtype=jnp.int32)
    blk_e = jnp.minimum(jnp.sum((blk[:, None] * tm >= pends[None, :]).astype(jnp.int32), axis=1), ne - 1)
    blk_e = jnp.where(blk < n_used, blk_e, blk_e[jnp.maximum(n_used - 1, 0)])
    blk_first = jnp.concatenate([jnp.ones((1,), jnp.int32), (blk_e[1:] != blk_e[:-1]).astype(jnp.int32)])
    rows = _dispatch_call(h3, dest_flat, npad)
    y_rows = _moe_call(rows, blk_e, blk_first, n_used.reshape(1), w1, b1, w2, b2)
    return _combine_call(y_rows, dest_flat, gate_t.T)


def _final_kernel(x_ref, moe_ref, mod_ref, g_ref, o_ref):
    x = x_ref[...] + mod_ref[5:6, :] * moe_ref[...]
    ms = jnp.mean(x * x, axis=-1, keepdims=True)
    o_ref[...] = x * lax.rsqrt(ms + RMS_EPS) * g_ref[...]


def _final_call(x, moe, mod, row_of_tile, g_final):
    t, d = x.shape
    tm = TM_IN
    spec = pl.BlockSpec((tm, d), lambda i: (i, 0))
    return pl.pallas_call(
        _final_kernel,
        grid=(t // tm,),
        in_specs=[spec, spec, pl.BlockSpec((None, N_MOD, d), lambda i: (row_of_tile(i, tm), 0, 0)),
                  pl.BlockSpec((1, d), lambda i: (0, 0))],
        out_specs=spec,
        out_shape=jax.ShapeDtypeStruct((t, d), F32),
        compiler_params=_cparams(("arbitrary",)),
        name="final_norm",
    )(x, moe, mod, g_final.reshape(1, d))


def _permute_cols(w):
    o = _SRC_OFF
    return jnp.concatenate([
        w[..., o["rwkv"]:o["lora"]], w[..., o["conv"]:o["qkv"]], w[..., o["gates"]:],
        w[..., o["qkv"]:o["rwkv"]], w[..., o["lora"]:o["gates"]]], axis=-1)


def kernel(x_prompt, x_sample, cache_k, cache_v, state_rwkv_fwd, state_rwkv_bwd, c, c_ctx, w_ada, b_ada, g_norm1, g_norm2, w_in, conv_w, attn_sink, rwkv_w0, rwkv_w2, rwkv_a0, rwkv_a2, rwkv_g2, rwkv_k_k, rwkv_k_a, rwkv_r_k, rwkv_ln_g, rwkv_ln_b, w_branch_conv, w_branch_attn, w_branch_rwkv, w_out, router_w, router_b, moe_w1, moe_b1, moe_w2, moe_b2, g_final):
    bc, lc, d = x_prompt.shape
    bl, tl, _ = x_sample.shape
    depth = w_in.shape[0]
    n_ctx = bc * lc
    t_all = n_ctx + bl * tl
    assert lc % TM_MIX == 0 and tl % TM_IN == 0 and n_ctx % TM_IN == 0 and t_all % TM_ROUTE == 0

    def row_of_tile(i, tm):
        return jnp.where(i < n_ctx // tm, 0, 1 + (i - n_ctx // tm) // (tl // tm))

    x = jnp.concatenate([x_prompt.reshape(n_ctx, d), x_sample.reshape(bl * tl, d)], axis=0)
    n_cond = -(-(1 + bl) // 8) * 8
    cond = jnp.zeros((n_cond, d), F32).at[0].set(c_ctx).at[1:1 + bl].set(c)
    mods = _ada_call(cond, w_ada, b_ada).reshape(depth, n_cond, N_MOD, d)

    w_in_p = _permute_cols(w_in).astype(BF16)
    head_id = np.arange(RWKV_W) // RWKV_HEAD
    ones_head = jnp.asarray((head_id[:, None] == head_id[None, :]).astype(np.float32), BF16)
    mean_head = (ones_head.astype(F32) / RWKV_HEAD).astype(BF16)
    cos_t, sin_t = _rope_tables(tl)
    zeros_state = jnp.zeros((bc, 2 * N_PAIR, RWKV_HEAD, RWKV_HEAD), F32)
    tiles = (n_ctx // TM_MIX, lc // TM_MIX, tl // TM_MIX)

    new_k, new_v, new_sf, new_sb = [], [], [], []
    res = None
    for l in range(depth):
        mod = mods[l]
        lw = dict(
            w0=rwkv_w0[l], w2=rwkv_w2[l], a0=rwkv_a0[l], a2=rwkv_a2[l], g2=rwkv_g2[l],
            kk_w=rwkv_k_k[l].reshape(1, -1), ka_w=rwkv_k_a[l].reshape(1, -1), rk_w=rwkv_r_k[l].reshape(1, -1),
            ln_g=rwkv_ln_g[l].reshape(1, -1), ln_b=rwkv_ln_b[l].reshape(1, -1), conv_w=conv_w[l],
            ones_head=ones_head, mean_head=mean_head,
            wa=w_branch_conv[l].astype(BF16), wb=w_branch_attn[l].astype(BF16), wc=w_branch_rwkv[l].astype(BF16),
            wo=w_out[l].astype(BF16),
        )
        if res is None:
            z = _in_call(x, w_in_p[l], g_norm1[l], mod, row_of_tile)
        else:
            z, x = _in_call(x, w_in_p[l], g_norm1[l], mod, row_of_tile, res=res)

        kv_ctx = z[:n_ctx, COL_QKV + ATT_W:COL_QKV + ATT_W + 2 * KV_W]
        new_k.append(kv_ctx[:, :KV_W].reshape(bc, lc, N_KV_HEADS, HEAD_DIM))
        new_v.append(kv_ctx[:, KV_W:].reshape(bc, lc, N_KV_HEADS, HEAD_DIM))

        ya_c = _attn_ctx_call(z, attn_sink[l], bc, lc)
        ya_l = _attn_lat_call(z, attn_sink[l], cache_k[:, l].reshape(bl, -1, KV_W), cache_v[:, l].reshape(bl, -1, KV_W),
                              cos_t, sin_t, bl, tl, n_ctx)
        y_attn = jnp.concatenate([ya_c, ya_l], axis=0)

        of_c, ob_c, zf_c, zb_c = _rwkv_call(z, lw, _state_to_z(zeros_state), _state_to_z(zeros_state), bc, lc, 0)
        of_l, ob_l, _, _ = _rwkv_call(z, lw, _state_to_z(state_rwkv_fwd[:, l].astype(F32)),
                                      _state_to_z(state_rwkv_bwd[:, l].astype(F32)), bl, tl, n_ctx)
        new_sf.append(_z_to_state(zf_c))
        new_sb.append(_z_to_state(zb_c))
        o_f = jnp.concatenate([of_c, of_l], axis=0)
        o_b = jnp.concatenate([ob_c, ob_l], axis=0)

        x_mid = _mix_call(x, z, o_f, o_b, y_attn, mod, row_of_tile, lw, tiles)
        h2, e_t, gate_t, rank_t, cnt = _route_call(x_mid, mod, row_of_tile, g_norm2[l], router_w[l], router_b[l])
        moe = _moe_layer(h2, e_t, gate_t, rank_t, cnt[:, 0], moe_w1[l], moe_b1[l], moe_w2[l], moe_b2[l])
        x = x_mid
        res = (moe, mod)

    y = _final_call(x, res[0], res[1], row_of_tile, g_final)
    y_prompt = y[:n_ctx].reshape(bc, lc, d)
    y_sample = y[n_ctx:].reshape(bl, tl, d)
    dt = x_prompt.dtype
    return (y_prompt, y_sample, jnp.stack(new_k, axis=1), jnp.stack(new_v, axis=1),
            jnp.stack(new_sf, axis=1).astype(dt), jnp.stack(new_sb, axis=1).astype(dt))
```

```python
import functools

import numpy as np
import jax
import jax.numpy as jnp
from jax import lax
from jax.experimental import pallas as pl
from jax.experimental.pallas import tpu as pltpu

F32 = jnp.float32
BF16 = jnp.bfloat16

D_MODEL = 1024
N_MOD = 6
RMS_EPS = 1e-6
CONV_W = 512
N_HEADS = 8
N_KV_HEADS = 2
GQA_GROUP = N_HEADS // N_KV_HEADS
HEAD_DIM = 64
ATT_W = N_HEADS * HEAD_DIM
KV_W = N_KV_HEADS * HEAD_DIM
WINDOW = 128
Q_BLOCK = 128
ATTN_SCALE = HEAD_DIM ** -0.5
ROPE_THETA = 10000.0
GRID_W = 64
RWKV_HEAD = 64
RWKV_W = 512
DECAY_LORA = 64
ICLR_LORA = 64
GATE_LORA = 128
GN_EPS = 64e-5
N_EXPERTS = 32
TOP_K = 4
D_EXPERT = 1024
SWIGLU_LIMIT = 7.0
SWIGLU_ALPHA = 1.702
P_TOTAL = 7296

_SRC_OFF = dict(conv=0, qkv=1536, rwkv=2304, lora=3840, gates=4224)
COL_RWKV, COL_CONV, COL_GATES, COL_QKV, COL_LORA = 0, 1536, 3072, 6144, 6912
LORA_W = 2 * DECAY_LORA + 2 * ICLR_LORA + GATE_LORA

CHUNK = 64
PAIR = 2 * RWKV_HEAD
N_PAIR = RWKV_W // PAIR
NEG_BIG = -1e30
EXP_NEG_HALF = float(np.exp(-0.5))

TM_IN = 512
TN_IN = 2432
TM_MIX = 256
TM_ROUTE = 512
TM_MOE = 512
TM_DISPATCH = 512
TM_COMBINE = 256
VMEM_LIMIT = 56 * 1024 * 1024


def _cparams(sem, vmem=VMEM_LIMIT):
    return pltpu.CompilerParams(dimension_semantics=sem, vmem_limit_bytes=vmem)


def _dot(a, b, dims=(((1,), (0,)), ((), ()))):
    return lax.dot_general(a, b, dims, preferred_element_type=F32)


_NT = (((1,), (1,)), ((), ()))
_TN = (((0,), (0,)), ((), ()))


def _split(x):
    hi = x.astype(BF16)
    lo = (x - hi.astype(F32)).astype(BF16)
    return hi, lo


def _dot3(a, b, dims=(((1,), (0,)), ((), ()))):
    ah, al = _split(a)
    bh, bl = _split(b)
    return _dot(ah, bh, dims) + (_dot(ah, bl, dims) + _dot(al, bh, dims))


def _head_sums(x, ones_pair):
    rows = x.shape[0]
    xs = jnp.concatenate([x[:, PAIR * p:PAIR * (p + 1)] for p in range(N_PAIR)], axis=0).astype(BF16)
    s = _dot(xs, ones_pair)
    return jnp.concatenate([s[rows * p:rows * (p + 1)] for p in range(N_PAIR)], axis=1)


def _sigmoid(x):
    return 1.0 / (1.0 + jnp.exp(-x))


def _rms_mod(x, g, scale, shift):
    ms = jnp.mean(x * x, axis=-1, keepdims=True)
    return (x * lax.rsqrt(ms + RMS_EPS) * g) * (1.0 + scale) + shift


def _ada_kernel(cond_ref, w_ref, b_ref, o_ref):
    c = cond_ref[...]
    s = c * _sigmoid(c)
    o_ref[...] = _dot(s.astype(BF16), w_ref[...].astype(BF16)) + b_ref[...]


def _ada_call(cond, w_ada, b_ada):
    depth, d, n = w_ada.shape
    r = cond.shape[0]
    tn = 1536
    return pl.pallas_call(
        _ada_kernel,
        grid=(depth, n // tn),
        in_specs=[
            pl.BlockSpec((r, d), lambda l, j: (0, 0)),
            pl.BlockSpec((None, d, tn), lambda l, j: (l, 0, j)),
            pl.BlockSpec((None, 1, tn), lambda l, j: (l, 0, j)),
        ],
        out_specs=pl.BlockSpec((None, r, tn), lambda l, j: (l, 0, j)),
        out_shape=jax.ShapeDtypeStruct((depth, r, n), F32),
        compiler_params=_cparams(("arbitrary", "arbitrary")),
        name="ada",
    )(cond, w_ada, b_ada.reshape(depth, 1, n))


def _in_kernel(has_res, *refs):
    if has_res:
        x_ref, moe_ref, modp_ref, mod_ref, g_ref, w_ref, z_ref, xo_ref, h_scr = refs
    else:
        x_ref, mod_ref, g_ref, w_ref, z_ref, h_scr = refs

    @pl.when(pl.program_id(1) == 0)
    def _():
        x = x_ref[...]
        if has_res:
            x = x + modp_ref[5:6, :] * moe_ref[...]
            xo_ref[...] = x
        h_scr[...] = _rms_mod(x, g_ref[...], mod_ref[1:2, :], mod_ref[0:1, :]).astype(BF16)

    z_ref[...] = _dot(h_scr[...], w_ref[...])


def _in_call(x, w_bf16, g_norm, mod, row_of_tile, res=None):
    t, d = x.shape
    n = w_bf16.shape[1]
    tm, tn = TM_IN, TN_IN
    mod_spec = pl.BlockSpec((None, N_MOD, d), lambda i, j: (row_of_tile(i, tm), 0, 0))
    x_spec = pl.BlockSpec((tm, d), lambda i, j: (i, 0))
    in_specs = [x_spec]
    args = [x]
    if res is not None:
        moe, mod_prev = res
        in_specs += [x_spec, mod_spec]
        args += [moe, mod_prev]
    in_specs += [mod_spec, pl.BlockSpec((1, d), lambda i, j: (0, 0)), pl.BlockSpec((d, tn), lambda i, j: (0, j))]
    args += [mod, g_norm.reshape(1, d), w_bf16]
    z_spec = pl.BlockSpec((tm, tn), lambda i, j: (i, j))
    z_shape = jax.ShapeDtypeStruct((t, n), F32)
    if res is not None:
        out_specs, out_shape = [z_spec, x_spec], [z_shape, jax.ShapeDtypeStruct((t, d), F32)]
    else:
        out_specs, out_shape = z_spec, z_shape
    return pl.pallas_call(
        functools.partial(_in_kernel, res is not None),
        grid=(t // tm, n // tn),
        in_specs=in_specs,
        out_specs=out_specs,
        out_shape=out_shape,
        scratch_shapes=[pltpu.VMEM((tm, d), BF16)],
        compiler_params=_cparams(("arbitrary", "arbitrary")),
        name="in_proj",
    )(*args)


def _softmax_pv(scores, values, sink):
    m = sink
    for s in scores:
        m = jnp.maximum(m, jnp.max(s, axis=-1, keepdims=True))
    den = jnp.exp(sink - m)
    acc = None
    for s, v in zip(scores, values):
        p = jnp.exp(s - m)
        den = den + jnp.sum(p, axis=-1, keepdims=True)
        pv = _dot(p.astype(BF16), v)
        acc = pv if acc is None else acc + pv
    return acc / den


def _attn_ctx_kernel(sink_ref, q_ref, kv_ref, o_ref):
    q = (q_ref[...] * ATTN_SCALE).astype(BF16)
    kv = kv_ref[...].astype(BF16)
    outs = []
    for h in range(N_HEADS):
        g = h // GQA_GROUP
        qh = q[:, HEAD_DIM * h:HEAD_DIM * (h + 1)]
        kh = kv[:, HEAD_DIM * g:HEAD_DIM * (g + 1)]
        vh = kv[:, KV_W + HEAD_DIM * g:KV_W + HEAD_DIM * (g + 1)]
        outs.append(_softmax_pv([_dot(qh, kh, _NT)], [vh], sink_ref[h]))
    o_ref[...] = jnp.concatenate(outs, axis=-1).astype(BF16)


def _attn_ctx_call(z, sink, n_seq, seq_len):
    return pl.pallas_call(
        _attn_ctx_kernel,
        grid=(n_seq,),
        in_specs=[
            pl.BlockSpec(memory_space=pltpu.SMEM),
            pl.BlockSpec((seq_len, ATT_W), lambda s: (s, COL_QKV // ATT_W)),
            pl.BlockSpec((seq_len, 2 * KV_W), lambda s: (s, (COL_QKV + ATT_W) // (2 * KV_W))),
        ],
        out_specs=pl.BlockSpec((seq_len, ATT_W), lambda s: (s, 0)),
        out_shape=jax.ShapeDtypeStruct((n_seq * seq_len, ATT_W), BF16),
        compiler_params=_cparams(("arbitrary",)),
        name="attn_ctx",
    )(sink, z, z)


def _rope(x, cos, sin_signed):
    n = x.shape[-1]
    lane = lax.broadcasted_iota(jnp.int32, x.shape, 1)
    up = pltpu.roll(x, n - 16, 1)
    dn = pltpu.roll(x, 16, 1)
    partner = jnp.where((lane & 31) < 16, up, dn)
    return x * cos + partner * sin_signed


def _attn_lat_kernel(sink_ref, q_ref, kvp_ref, kvc_ref, kvn_ref, ck_ref, cv_ref,
                     cosq_ref, sinq_ref, cosp_ref, sinp_ref, cosn_ref, sinn_ref, o_ref):
    qb = pl.program_id(1)
    nb = pl.num_programs(1)
    cq, sq = cosq_ref[...], sinq_ref[...]
    q = _rope(q_ref[...], jnp.concatenate([cq] * 4, axis=1), jnp.concatenate([sq] * 4, axis=1))
    q = (q * ATTN_SCALE).astype(BF16)
    kvp, kvc, kvn = kvp_ref[...], kvc_ref[...], kvn_ref[...]
    kp = _rope(kvp[:, :KV_W], cosp_ref[...], sinp_ref[...]).astype(BF16)
    kc = _rope(kvc[:, :KV_W], cq, sq).astype(BF16)
    kn = _rope(kvn[:, :KV_W], cosn_ref[...], sinn_ref[...]).astype(BF16)
    vp, vc, vn = (t[:, KV_W:].astype(BF16) for t in (kvp, kvc, kvn))
    ck = ck_ref[...].astype(BF16)
    cv = cv_ref[...].astype(BF16)
    k_cat = jnp.concatenate([kp, kc, kn, ck], axis=0)
    v_cat = jnp.concatenate([vp, vc, vn, cv], axis=0)
    nk = k_cat.shape[0]
    qi = lax.broadcasted_iota(jnp.int32, (Q_BLOCK, nk), 0)
    kj = lax.broadcasted_iota(jnp.int32, (Q_BLOCK, nk), 1)
    ok_p = (kj >= qi) & (qb > 0)
    ok_n = (kj - 2 * Q_BLOCK <= qi) & (qb < nb - 1)
    in_next = (kj >= 2 * Q_BLOCK) & (kj < 3 * Q_BLOCK)
    bias = jnp.where(kj < Q_BLOCK, jnp.where(ok_p, 0.0, NEG_BIG), jnp.where(in_next, jnp.where(ok_n, 0.0, NEG_BIG), 0.0))
    bias = jnp.concatenate([bias] * GQA_GROUP, axis=0)
    groups = range(N_KV_HEADS)
    heads = [range(GQA_GROUP * g, GQA_GROUP * (g + 1)) for g in groups]
    gsl = [slice(HEAD_DIM * g, HEAD_DIM * (g + 1)) for g in groups]
    q_g = [jnp.concatenate([q[:, HEAD_DIM * h:HEAD_DIM * (h + 1)] for h in heads[g]], axis=0) for g in groups]
    sink = [jnp.concatenate([jnp.full((Q_BLOCK, 1), sink_ref[h], F32) for h in heads[g]], axis=0) for g in groups]
    s = [_dot(q_g[g], k_cat[:, gsl[g]], _NT) + bias for g in groups]
    m = [jnp.maximum(jnp.max(s[g], axis=-1, keepdims=True), sink[g]) for g in groups]
    p = [jnp.exp(s[g] - m[g]) for g in groups]
    den = [jnp.sum(p[g], axis=-1, keepdims=True) + jnp.exp(sink[g] - m[g]) for g in groups]
    o = [_dot(p[g].astype(BF16), v_cat[:, gsl[g]]) / den[g] for g in groups]
    outs = [o[g][Q_BLOCK * i:Q_BLOCK * (i + 1)] for g in groups for i in range(GQA_GROUP)]
    o_ref[...] = jnp.concatenate(outs, axis=-1).astype(BF16)


def _attn_lat_call(z, sink, cache_k, cache_v, cos_t, sin_t, n_seq, seq_len, row0):
    nb = seq_len // Q_BLOCK
    base = row0 // Q_BLOCK
    past = cache_k.shape[1]

    def rows(off):
        return lambda b, i: (base + b * nb + jnp.clip(i + off, 0, nb - 1))

    def kv_spec(off):
        r = rows(off)
        return pl.BlockSpec((Q_BLOCK, 2 * KV_W), lambda b, i: (r(b, i), (COL_QKV + ATT_W) // (2 * KV_W)))

    def tab_spec(off):
        return pl.BlockSpec((Q_BLOCK, KV_W), lambda b, i: (jnp.clip(i + off, 0, nb - 1), 0))

    r0 = rows(0)
    return pl.pallas_call(
        _attn_lat_kernel,
        grid=(n_seq, nb),
        in_specs=[
            pl.BlockSpec(memory_space=pltpu.SMEM),
            pl.BlockSpec((Q_BLOCK, ATT_W), lambda b, i: (r0(b, i), COL_QKV // ATT_W)),
            kv_spec(-1), kv_spec(0), kv_spec(1),
            pl.BlockSpec((None, past, KV_W), lambda b, i: (b, 0, 0)),
            pl.BlockSpec((None, past, KV_W), lambda b, i: (b, 0, 0)),
            tab_spec(0), tab_spec(0), tab_spec(-1), tab_spec(-1), tab_spec(1), tab_spec(1),
        ],
        out_specs=pl.BlockSpec((Q_BLOCK, ATT_W), lambda b, i: (b * nb + i, 0)),
        out_shape=jax.ShapeDtypeStruct((n_seq * seq_len, ATT_W), BF16),
        compiler_params=_cparams(("arbitrary", "arbitrary")),
        name="attn_lat",
    )(sink, z, z, z, z, cache_k, cache_v, cos_t, sin_t, cos_t, sin_t, cos_t, sin_t)


def _rope_tables(seq_len):
    half = HEAD_DIM // 2
    pos = np.arange(seq_len)
    inv_freq = 1.0 / (ROPE_THETA ** (np.arange(0, half, 2, dtype=np.float32) / half))
    inv_freq = inv_freq.astype(np.float32)

    def part(p):
        ang = (p.astype(np.float32)[:, None] * inv_freq[None, :]).astype(np.float32)
        c, s = np.cos(ang), np.sin(ang)
        return np.concatenate([c, c], axis=1), np.concatenate([-s, s], axis=1)

    c_r, s_r = part(pos // GRID_W)
    c_c, s_c = part(pos % GRID_W)
    cos = np.concatenate([c_r, c_c] * N_KV_HEADS, axis=1).astype(np.float32)
    sin = np.concatenate([s_r, s_c] * N_KV_HEADS, axis=1).astype(np.float32)
    return jnp.asarray(cos), jnp.asarray(sin)


def _rwkv_dir_inputs(zm, zl, d, w0, w2, a0, a2, kk_w, ka_w, ones_pair):
    r = zm[:, :RWKV_W]
    kraw = zm[:, RWKV_W:2 * RWKV_W]
    v = zm[:, 2 * RWKV_W:]
    wl = zl[:, DECAY_LORA * d:DECAY_LORA * (d + 1)]
    al = zl[:, 2 * DECAY_LORA + ICLR_LORA * d:2 * DECAY_LORA + ICLR_LORA * (d + 1)]
    xw = w0 + _dot(jnp.tanh(wl).astype(BF16), w2.astype(BF16))
    ld = -EXP_NEG_HALF * _sigmoid(xw)
    a = _sigmoid(a0 + _dot(al.astype(BF16), a2.astype(BF16)))
    k = kraw * (1.0 + (a - 1.0) * ka_w)
    kkr = kraw * kk_w
    n2 = _head_sums(kkr * kkr, ones_pair)
    kk = kkr / jnp.maximum(jnp.sqrt(n2), 1e-12)
    return r, v, kk, ld, a, k


def _tri_masks(rev):
    c = CHUNK
    ti = lax.broadcasted_iota(jnp.int32, (c, c), 0)
    si = lax.broadcasted_iota(jnp.int32, (c, c), 1)
    incl = (si >= ti) if rev else (si <= ti)
    strict = (si > ti) if rev else (si < ti)
    return incl, strict, (si == ti).astype(F32)


def _chunk_prepare(r, v, kk, ld, a, k, rev):
    incl, _, _ = _tri_masks(rev)
    m_incl = jnp.where(incl, 1.0, 0.0).astype(BF16)
    ldh, ldl = _split(ld)
    cin = _dot(m_incl, ldh) + _dot(m_incl, ldl)
    tot = jnp.sum(ld, axis=0, keepdims=True)
    e_neg = jnp.exp(-cin)
    e_end = jnp.exp(tot - cin)
    bb = kk * a
    return dict(
        a_m=kk * jnp.exp(cin - ld), r_m=r * jnp.exp(cin),
        b_m=(bb * e_neg).astype(BF16), k_m=(k * e_neg).astype(BF16),
        b_end=(bb * e_end).astype(BF16), k_end=(k * e_end).astype(BF16),
        v=v, e_tot=jnp.exp(tot), rev=rev)


def _chunk_problems(probs):
    c = CHUNK
    n = len(probs)
    rng = range(n)
    lane = lax.broadcasted_iota(jnp.int32, (1, PAIR), 1)
    first_head = lane < RWKV_HEAD

    def bd(x):
        xb = x.astype(BF16)
        zero = jnp.zeros_like(xb)
        return jnp.concatenate([jnp.where(first_head, xb, zero), jnp.where(first_head, zero, xb)], axis=0)

    ti = lax.broadcasted_iota(jnp.int32, (c, PAIR), 0)
    si = lax.broadcasted_iota(jnp.int32, (c, PAIR), 1) & (RWKV_HEAD - 1)
    incl = {False: si <= ti, True: si >= ti}
    strict = {False: si < ti, True: si > ti}
    eye = (si == ti).astype(F32)
    rev = [p["rev"] for p in probs]

    lhs = [jnp.concatenate([probs[i]["a_m"], probs[i]["r_m"]], axis=0).astype(BF16) for i in rng]
    xb = [_dot(lhs[i], bd(probs[i]["b_m"]), _NT) for i in rng]
    xk = [_dot(lhs[i], bd(probs[i]["k_m"]), _NT) for i in rng]
    m_ab = [jnp.where(strict[rev[i]], xb[i][:c], 0.0) for i in rng]
    m_ak = [jnp.where(strict[rev[i]], xk[i][:c], 0.0).astype(BF16) for i in rng]
    m_rb = [jnp.where(incl[rev[i]], xb[i][c:], 0.0).astype(BF16) for i in rng]
    m_rk = [jnp.where(incl[rev[i]], xk[i][c:], 0.0).astype(BF16) for i in rng]
    v_bd = [bd(probs[i]["v"]) for i in rng]
    mak_v = [_dot(m_ak[i], v_bd[i]) for i in rng]
    mrk_v = [_dot(m_rk[i], v_bd[i]) for i in rng]
    t_inv = [eye - m_ab[i] for i in rng]
    lp = list(m_ab)
    for _ in range(5):
        lp = [_dot(lp[i].astype(BF16), bd(lp[i])) for i in rng]
        t_inv = [t_inv[i] + _dot(t_inv[i].astype(BF16), bd(lp[i])) for i in rng]
    wu = [_dot(t_inv[i].astype(BF16), jnp.concatenate([bd(probs[i]["a_m"]), bd(mak_v[i])], axis=1)) for i in rng]
    ro = [_dot(m_rb[i], jnp.concatenate([bd(wu[i][:, :PAIR]), bd(wu[i][:, PAIR:])], axis=1)) for i in rng]
    ri = lax.broadcasted_iota(jnp.int32, (PAIR, PAIR), 0)
    ci = lax.broadcasted_iota(jnp.int32, (PAIR, PAIR), 1)
    same_head = (ri < RWKV_HEAD) == (ci < RWKV_HEAD)
    bt_wu = [_dot(probs[i]["b_end"], wu[i].astype(BF16), _TN) for i in rng]
    kt_v = [_dot(probs[i]["k_end"], probs[i]["v"].astype(BF16), _TN) for i in rng]
    results = []
    for i in rng:
        r_eff = probs[i]["r_m"] - ro[i][:, :PAIR]
        o_loc = mrk_v[i] - ro[i][:, PAIR:]
        g_t = jnp.where(same_head, jnp.where(ri == ci, probs[i]["e_tot"], 0.0) - bt_wu[i][:, :PAIR], 0.0)
        h_t = jnp.where(same_head, kt_v[i] - bt_wu[i][:, PAIR:], 0.0)
        zh, zl = _split(probs[i]["z"])
        both = jnp.concatenate([r_eff, g_t], axis=0)
        bh, bl = _split(both)
        prod = _dot(bh, zh) + (_dot(bh, zl) + _dot(bl, zh))
        results.append((prod[:c] + o_loc, prod[c:] + h_t))
    return results


def _rwkv_kernel(zmf_ref, zlf_ref, zmb_ref, zlb_ref, w0_ref, w2_ref, a0_ref, a2_ref, kkw_ref, kaw_ref,
                 ones_ref, s0f_ref, s0b_ref, of_ref, ob_ref, zf_ref, zb_ref, zf_scr, zb_scr):
    @pl.when(pl.program_id(1) == 0)
    def _():
        zf_scr[...] = s0f_ref[...]
        zb_scr[...] = s0b_ref[...]

    ones_pair = ones_ref[...]
    dirs = ((zmf_ref, zlf_ref, zf_scr, of_ref), (zmb_ref, zlb_ref, zb_scr, ob_ref))
    probs = []
    for d, (zm_ref, zl_ref, z_scr, _) in enumerate(dirs):
        r, v, kk, ld, a, k = _rwkv_dir_inputs(zm_ref[...], zl_ref[...], d, w0_ref[d:d + 1, :], w2_ref[d],
                                              a0_ref[d:d + 1, :], a2_ref[d], kkw_ref[...], kaw_ref[...], ones_pair)
        full = _chunk_prepare(r, v, kk, ld, a, k, d == 1)
        for p in range(N_PAIR):
            ps = slice(PAIR * p, PAIR * (p + 1))
            prob = {key: (val[:, ps] if hasattr(val, "shape") else val) for key, val in full.items()}
            prob["z"] = z_scr[p]
            probs.append(prob)
    results = _chunk_problems(probs)
    for d, (_, _, z_scr, o_ref) in enumerate(dirs):
        for p in range(N_PAIR):
            out, z_new = results[d * N_PAIR + p]
            o_ref[:, PAIR * p:PAIR * (p + 1)] = out
            z_scr[p] = z_new
    zf_ref[...] = zf_scr[...]
    zb_ref[...] = zb_scr[...]


def _rwkv_call(z, lw, s0f, s0b, n_seq, seq_len, row0):
    nc = seq_len // CHUNK
    base = row0 // CHUNK

    def fwd(s, c):
        return base + s * nc + c

    def bwd(s, c):
        return base + s * nc + (nc - 1 - c)

    def zm_spec(f):
        return pl.BlockSpec((CHUNK, 3 * RWKV_W), lambda s, c: (f(s, c), COL_RWKV // (3 * RWKV_W)))

    def zl_spec(f):
        return pl.BlockSpec((CHUNK, LORA_W), lambda s, c: (f(s, c), COL_LORA // LORA_W))

    def full(shape):
        return pl.BlockSpec(shape, lambda s, c: (0,) * len(shape))

    st_spec = pl.BlockSpec((None, N_PAIR, PAIR, PAIR), lambda s, c: (s, 0, 0, 0))
    o_shape = jax.ShapeDtypeStruct((n_seq * seq_len, RWKV_W), F32)
    st_shape = jax.ShapeDtypeStruct((n_seq, N_PAIR, PAIR, PAIR), F32)
    return pl.pallas_call(
        _rwkv_kernel,
        grid=(n_seq, nc),
        in_specs=[
            zm_spec(fwd), zl_spec(fwd), zm_spec(bwd), zl_spec(bwd),
            full((2, RWKV_W)), full((2, DECAY_LORA, RWKV_W)), full((2, RWKV_W)), full((2, ICLR_LORA, RWKV_W)),
            full((1, RWKV_W)), full((1, RWKV_W)), full((PAIR, PAIR)),
            st_spec, st_spec,
        ],
        out_specs=[
            pl.BlockSpec((CHUNK, RWKV_W), lambda s, c: (s * nc + c, 0)),
            pl.BlockSpec((CHUNK, RWKV_W), lambda s, c: (s * nc + (nc - 1 - c), 0)),
            st_spec, st_spec,
        ],
        out_shape=[o_shape, o_shape, st_shape, st_shape],
        scratch_shapes=[pltpu.VMEM((N_PAIR, PAIR, PAIR), F32), pltpu.VMEM((N_PAIR, PAIR, PAIR), F32)],
        compiler_params=_cparams(("arbitrary", "arbitrary")),
        name="rwkv_scan",
    )(z, z, z, z, lw["w0"], lw["w2"], lw["a0"], lw["a2"], lw["kk_w"], lw["ka_w"], lw["ones_pair"], s0f, s0b)


def _state_to_z(s):
    n = s.shape[0]
    st = jnp.swapaxes(s, -1, -2).reshape(n, N_PAIR, 2, RWKV_HEAD, RWKV_HEAD)
    zero = jnp.zeros_like(st[:, :, 0])
    top = jnp.concatenate([st[:, :, 0], zero], axis=-1)
    bot = jnp.concatenate([zero, st[:, :, 1]], axis=-1)
    return jnp.concatenate([top, bot], axis=-2)


def _z_to_state(z):
    n = z.shape[0]
    h0 = z[:, :, :RWKV_HEAD, :RWKV_HEAD]
    h1 = z[:, :, RWKV_HEAD:, RWKV_HEAD:]
    st = jnp.stack([h0, h1], axis=2).reshape(n, 2 * N_PAIR, RWKV_HEAD, RWKV_HEAD)
    return jnp.swapaxes(st, -1, -2)


def _mix_kernel(tiles, x_ref, zc_ref, zcp_ref, zcn_ref, zm_ref, zl_ref, zg_ref, of_ref, ob_ref, ya_ref, mod_ref,
                convw_ref, a0_ref, a2_ref, g2_ref, kaw_ref, rkw_ref, lng_ref, lnb_ref, ones_ref,
                wa_ref, wb_ref, wc_ref, wo_ref, o_ref):
    n_ctx_tiles, per_ctx, per_lat = tiles
    i = pl.program_id(0)
    pos = jnp.where(i < n_ctx_tiles, i % per_ctx, (i - n_ctx_tiles) % per_lat)
    per = jnp.where(i < n_ctx_tiles, per_ctx, per_lat)
    tm = x_ref.shape[0]

    zc = zc_ref[...]
    u = zc[:, 2 * CONV_W:] * zc[:, :CONV_W]
    zp = zcp_ref[...]
    zn = zcn_ref[...]
    u_prev_row = jnp.where(pos > 0, zp[7:8, 2 * CONV_W:] * zp[7:8, :CONV_W], 0.0)
    u_next_row = jnp.where(pos < per - 1, zn[0:1, 2 * CONV_W:] * zn[0:1, :CONV_W], 0.0)
    row = lax.broadcasted_iota(jnp.int32, (tm, CONV_W), 0)
    u_prev = jnp.where(row == 0, u_prev_row, pltpu.roll(u, 1, 0))
    u_next = jnp.where(row == tm - 1, u_next_row, pltpu.roll(u, tm - 1, 0))
    cw = convw_ref[...]
    y_conv = zc[:, CONV_W:2 * CONV_W] * (cw[0:1, :] * u_prev + cw[1:2, :] * u + cw[2:3, :] * u_next)

    zm = zm_ref[...]
    zl = zl_ref[...]
    r = zm[:, :RWKV_W]
    kraw = zm[:, RWKV_W:2 * RWKV_W]
    v = zm[:, 2 * RWKV_W:]
    ones_pair = ones_ref[...]
    o = of_ref[...] + ob_ref[...]
    mu = _head_sums(o, ones_pair) * (1.0 / RWKV_HEAD)
    dlt = o - mu
    var = _head_sums(dlt * dlt, ones_pair) * (1.0 / RWKV_HEAD)
    y = dlt * lax.rsqrt(var + GN_EPS) * lng_ref[...] + lnb_ref[...]
    for d in range(2):
        al = zl[:, 2 * DECAY_LORA + ICLR_LORA * d:2 * DECAY_LORA + ICLR_LORA * (d + 1)]
        a = _sigmoid(a0_ref[d:d + 1, :] + _dot(al.astype(BF16), a2_ref[d].astype(BF16)))
        k = kraw * (1.0 + (a - 1.0) * kaw_ref[...])
        y = y + _head_sums(r * k * rkw_ref[...], ones_pair) * v
    g1 = zl[:, 2 * DECAY_LORA + 2 * ICLR_LORA:]
    y_rwkv = y * _dot(_sigmoid(g1).astype(BF16), g2_ref[...].astype(BF16))

    zg = zg_ref[...]
    merged = (_sigmoid(zg[:, :D_MODEL]) * _dot(y_conv.astype(BF16), wa_ref[...])
              + _sigmoid(zg[:, D_MODEL:2 * D_MODEL]) * _dot(ya_ref[...], wb_ref[...])
              + _sigmoid(zg[:, 2 * D_MODEL:]) * _dot(y_rwkv.astype(BF16), wc_ref[...]))
    o_ref[...] = x_ref[...] + mod_ref[2:3, :] * _dot(merged.astype(BF16), wo_ref[...])


def _mix_call(x, z, o_f, o_b, y_attn, mod, row_of_tile, lw, tiles):
    t, d = x.shape
    tm = TM_MIX
    nt = t // tm
    hb = tm // 8

    def rows(w, col):
        return pl.BlockSpec((tm, w), lambda i: (i, col // w))

    def full(shape):
        return pl.BlockSpec(shape, lambda i: (0,) * len(shape))

    in_specs = [
        rows(d, 0),
        rows(3 * CONV_W, COL_CONV),
        pl.BlockSpec((8, 3 * CONV_W), lambda i: (jnp.maximum(i * hb - 1, 0), COL_CONV // (3 * CONV_W))),
        pl.BlockSpec((8, 3 * CONV_W), lambda i: (jnp.minimum((i + 1) * hb, nt * hb - 1), COL_CONV // (3 * CONV_W))),
        rows(3 * RWKV_W, COL_RWKV),
        rows(LORA_W, COL_LORA),
        rows(3 * D_MODEL, COL_GATES),
        rows(RWKV_W, 0), rows(RWKV_W, 0), rows(ATT_W, 0),
        pl.BlockSpec((None, N_MOD, d), lambda i: (row_of_tile(i, tm), 0, 0)),
        full((3, CONV_W)), full((2, RWKV_W)), full((2, ICLR_LORA, RWKV_W)), full((GATE_LORA, RWKV_W)),
        full((1, RWKV_W)), full((1, RWKV_W)), full((1, RWKV_W)), full((1, RWKV_W)),
        full((PAIR, PAIR)),
        full((CONV_W, d)), full((ATT_W, d)), full((RWKV_W, d)), full((d, d)),
    ]
    return pl.pallas_call(
        functools.partial(_mix_kernel, tiles),
        grid=(nt,),
        in_specs=in_specs,
        out_specs=rows(d, 0),
        out_shape=jax.ShapeDtypeStruct((t, d), F32),
        compiler_params=_cparams(("arbitrary",)),
        name="mix",
    )(x, z, z, z, z, z, z, o_f, o_b, y_attn, mod,
      lw["conv_w"], lw["a0"], lw["a2"], lw["g2"], lw["ka_w"], lw["rk_w"], lw["ln_g"], lw["ln_b"],
      lw["ones_pair"], lw["wa"], lw["wb"], lw["wc"], lw["wo"])


def _route_kernel(x_ref, mod_ref, g_ref, rwt_ref, rb_ref, tri_ref, h_ref, e_ref, gate_ref, rank_ref, cnt_ref, cnt_scr):
    @pl.when(pl.program_id(0) == 0)
    def _():
        cnt_scr[...] = jnp.zeros_like(cnt_scr)

    h2 = _rms_mod(x_ref[...], g_ref[...], mod_ref[4:5, :], mod_ref[3:4, :])
    h_ref[...] = h2.reshape(h_ref.shape)
    logits = _dot3(rwt_ref[...], h2, _NT) + rb_ref[...]
    ne, tm = logits.shape
    ex = lax.broadcasted_iota(jnp.int32, (ne, tm), 0)
    work = logits
    vals, hots = [], []
    for kq in range(TOP_K):
        m = jnp.max(work, axis=0, keepdims=True)
        idx = jnp.min(jnp.where(work == m, ex, ne), axis=0, keepdims=True)
        hot = ex == idx
        vals.append(m)
        hots.append(hot)
        e_ref[kq:kq + 1, :] = idx
        work = jnp.where(hot, -jnp.inf, work)
    exps = [jnp.exp(vk - vals[0]) for vk in vals]
    den = exps[0] + exps[1] + exps[2] + exps[3]
    chosen = jnp.where(hots[0] | hots[1] | hots[2] | hots[3], 1.0, 0.0)
    before = cnt_scr[:, 0:1] + _dot(chosen.astype(BF16), tri_ref[...])
    for kq in range(TOP_K):
        gate_ref[kq:kq + 1, :] = exps[kq] / den
        rank_ref[kq:kq + 1, :] = jnp.sum(jnp.where(hots[kq], before, 0.0), axis=0, keepdims=True).astype(jnp.int32)
    cnt_scr[...] = cnt_scr[...] + jnp.sum(chosen, axis=1, keepdims=True)
    cnt_ref[...] = cnt_scr[...].astype(jnp.int32)


def _route_call(x, mod, row_of_tile, g_norm, router_w, router_b):
    t, d = x.shape
    tm = TM_ROUTE
    ne = router_w.shape[1]
    tri = jnp.triu(jnp.ones((tm, tm), F32), 1).astype(BF16)

    def full(shape):
        return pl.BlockSpec(shape, lambda i: (0,) * len(shape))

    kt_spec = pl.BlockSpec((TOP_K, tm), lambda i: (0, i))
    return pl.pallas_call(
        _route_kernel,
        grid=(t // tm,),
        in_specs=[
            pl.BlockSpec((tm, d), lambda i: (i, 0)),
            pl.BlockSpec((None, N_MOD, d), lambda i: (row_of_tile(i, tm), 0, 0)),
            full((1, d)), full((ne, d)), full((ne, 1)), full((tm, tm)),
        ],
        out_specs=[pl.BlockSpec((tm, 1, d), lambda i: (i, 0, 0)), kt_spec, kt_spec, kt_spec, full((ne, 128))],
        out_shape=[
            jax.ShapeDtypeStruct((t, 1, d), F32),
            jax.ShapeDtypeStruct((TOP_K, t), jnp.int32),
            jax.ShapeDtypeStruct((TOP_K, t), F32),
            jax.ShapeDtypeStruct((TOP_K, t), jnp.int32),
            jax.ShapeDtypeStruct((ne, 128), jnp.int32),
        ],
        scratch_shapes=[pltpu.VMEM((ne, 128), F32)],
        compiler_params=_cparams(("arbitrary",)),
        name="route",
    )(x, mod, g_norm.reshape(1, d), router_w.T, router_b.reshape(ne, 1), tri)


def _row_copy_wait(buf_hbm, n_rows, sem):
    view = buf_hbm.at[pl.ds(0, n_rows)]
    pltpu.make_async_copy(view, view, sem).wait()


def _dispatch_kernel(t_all, dest_ref, h_ref, rows_in_hbm, rows_hbm, sem):
    del rows_in_hbm
    tm = TM_DISPATCH
    t0 = pl.program_id(0) * tm

    def body(t, carry):
        for kq in range(TOP_K):
            pltpu.make_async_copy(h_ref.at[t], rows_hbm.at[dest_ref[kq * t_all + t0 + t]], sem).start(priority=kq % 2)
        return carry

    lax.fori_loop(0, tm, body, 0)
    _row_copy_wait(rows_hbm, TOP_K * tm, sem)


def _dispatch_call(h3, dest_flat, npad):
    t, _, d = h3.shape
    grid_spec = pltpu.PrefetchScalarGridSpec(
        num_scalar_prefetch=1,
        grid=(t // TM_DISPATCH,),
        in_specs=[pl.BlockSpec((TM_DISPATCH, 1, d), lambda i, dr: (i, 0, 0)), pl.BlockSpec(memory_space=pl.ANY)],
        out_specs=pl.BlockSpec(memory_space=pl.ANY),
        scratch_shapes=[pltpu.SemaphoreType.DMA(())],
    )
    return pl.pallas_call(
        functools.partial(_dispatch_kernel, t),
        grid_spec=grid_spec,
        out_shape=jax.ShapeDtypeStruct((npad, 1, d), F32),
        input_output_aliases={2: 0},
        compiler_params=_cparams(("arbitrary",)),
        name="moe_dispatch",
    )(dest_flat, h3, jnp.zeros((npad, 1, d), F32))


def _moe_kernel(be_ref, first_ref, nused_ref, x_ref, w1_ref, b1_ref, w2_ref, b2_ref, o_ref, w1_scr, w2_scr, x_scr):
    i = pl.program_id(0)

    @pl.when(first_ref[i] == 1)
    def _():
        w1_scr[...] = w1_ref[...].astype(BF16)
        w2_scr[...] = w2_ref[...].astype(BF16)

    @pl.when(i < nused_ref[0])
    def _():
        x_scr[...] = x_ref[...].reshape(x_scr.shape)
        hm = _dot(x_scr[...].astype(BF16), w1_scr[...]) + b1_ref[...]
        glu = jnp.minimum(hm[:, :D_EXPERT], SWIGLU_LIMIT)
        lin = jnp.clip(hm[:, D_EXPERT:], -SWIGLU_LIMIT, SWIGLU_LIMIT)
        act = glu * _sigmoid(SWIGLU_ALPHA * glu) * (lin + 1.0)
        y = _dot(act.astype(BF16), w2_scr[...]) + b2_ref[...]
        o_ref[...] = y.reshape(o_ref.shape)

    @pl.when(i >= nused_ref[0])
    def _():
        o_ref[...] = jnp.zeros_like(o_ref)


def _moe_call(rows, blk_e, blk_first, n_used, w1, b1, w2, b2):
    npad, _, d = rows.shape
    tm = TM_MOE
    ne, _, dh2 = w1.shape
    de = w2.shape[1]
    row_spec = pl.BlockSpec((tm, 1, d), lambda i, be, fi, nu: (i, 0, 0))
    grid_spec = pltpu.PrefetchScalarGridSpec(
        num_scalar_prefetch=3,
        grid=(npad // tm,),
        in_specs=[
            row_spec,
            pl.BlockSpec((None, d, dh2), lambda i, be, fi, nu: (be[i], 0, 0)),
            pl.BlockSpec((None, 1, dh2), lambda i, be, fi, nu: (be[i], 0, 0)),
            pl.BlockSpec((None, de, d), lambda i, be, fi, nu: (be[i], 0, 0)),
            pl.BlockSpec((None, 1, d), lambda i, be, fi, nu: (be[i], 0, 0)),
        ],
        out_specs=row_spec,
        scratch_shapes=[pltpu.VMEM((d, dh2), BF16), pltpu.VMEM((de, d), BF16), pltpu.VMEM((tm, d), F32)],
    )
    return pl.pallas_call(
        _moe_kernel,
        grid_spec=grid_spec,
        out_shape=jax.ShapeDtypeStruct((npad, 1, d), F32),
        compiler_params=_cparams(("arbitrary",)),
        name="moe",
    )(blk_e, blk_first, n_used, rows, w1, b1.reshape(ne, 1, dh2), w2, b2.reshape(ne, 1, d))


def _combine_kernel(t_all, dest_ref, y_hbm, gate_ref, o_ref, buf, y_scr, sem):
    tm = TM_COMBINE
    t0 = pl.program_id(0) * tm

    def body(t, carry):
        for kq in range(TOP_K):
            pltpu.make_async_copy(y_hbm.at[dest_ref[kq * t_all + t0 + t]], buf.at[kq * tm + t], sem).start(priority=kq % 2)
        return carry

    lax.fori_loop(0, tm, body, 0)
    pltpu.make_async_copy(y_hbm.at[pl.ds(0, TOP_K * tm)], buf, sem).wait()
    y_scr[...] = buf[...].reshape(y_scr.shape)
    gate = gate_ref[...]
    acc = gate[:, 0:1] * y_scr[0:tm, :]
    for kq in range(1, TOP_K):
        acc = acc + gate[:, kq:kq + 1] * y_scr[kq * tm:(kq + 1) * tm, :]
    o_ref[...] = acc


def _combine_call(y_rows, dest_flat, gate_tk):
    t = gate_tk.shape[0]
    d = y_rows.shape[-1]
    tm = TM_COMBINE
    grid_spec = pltpu.PrefetchScalarGridSpec(
        num_scalar_prefetch=1,
        grid=(t // tm,),
        in_specs=[pl.BlockSpec(memory_space=pl.ANY), pl.BlockSpec((tm, TOP_K), lambda i, dr: (i, 0))],
        out_specs=pl.BlockSpec((tm, d), lambda i, dr: (i, 0)),
        scratch_shapes=[pltpu.VMEM((TOP_K * tm, 1, d), F32), pltpu.VMEM((TOP_K * tm, d), F32),
                        pltpu.SemaphoreType.DMA(())],
    )
    return pl.pallas_call(
        functools.partial(_combine_kernel, t),
        grid_spec=grid_spec,
        out_shape=jax.ShapeDtypeStruct((t, d), F32),
        compiler_params=_cparams(("arbitrary",)),
        name="moe_combine",
    )(dest_flat, y_rows, gate_tk)


def _moe_layer(h3, e_t, gate_t, rank_t, counts, w1, b1, w2, b2):
    t, _, d = h3.shape
    tm = TM_MOE
    ne = w1.shape[0]
    n_assign = t * TOP_K
    n_blk = -(-(n_assign + ne * (tm - 1)) // tm)
    npad = n_blk * tm
    padded = (counts + tm - 1) // tm * tm
    pends = jnp.cumsum(padded)
    pstarts = pends - padded
    start_of = jnp.sum(jnp.where(e_t[..., None] == jnp.arange(ne, dtype=jnp.int32), pstarts.astype(jnp.int32), 0), axis=-1)
    dest_flat = (start_of + rank_t).reshape(-1)
    n_used = (pends[-1] // tm).astype(jnp.int32)
    blk = jnp.arange(n_blk, dtype=jnp.int32)
    blk_e = jnp.minimum(jnp.sum((blk[:, None] * tm >= pends[None, :]).astype(jnp.int32), axis=1), ne - 1)
    blk_e = jnp.where(blk < n_used, blk_e, blk_e[jnp.maximum(n_used - 1, 0)])
    blk_first = jnp.concatenate([jnp.ones((1,), jnp.int32), (blk_e[1:] != blk_e[:-1]).astype(jnp.int32)])
    rows = _dispatch_call(h3, dest_flat, npad)
    y_rows = _moe_call(rows, blk_e, blk_first, n_used.reshape(1), w1, b1, w2, b2)
    return _combine_call(y_rows, dest_flat, gate_t.T)


def _final_kernel(x_ref, moe_ref, mod_ref, g_ref, o_ref):
    x = x_ref[...] + mod_ref[5:6, :] * moe_ref[...]
    ms = jnp.mean(x * x, axis=-1, keepdims=True)
    o_ref[...] = x * lax.rsqrt(ms + RMS_EPS) * g_ref[...]


def _final_call(x, moe, mod, row_of_tile, g_final):
    t, d = x.shape
    tm = TM_IN
    spec = pl.BlockSpec((tm, d), lambda i: (i, 0))
    return pl.pallas_call(
        _final_kernel,
        grid=(t // tm,),
        in_specs=[spec, spec, pl.BlockSpec((None, N_MOD, d), lambda i: (row_of_tile(i, tm), 0, 0)),
                  pl.BlockSpec((1, d), lambda i: (0, 0))],
        out_specs=spec,
        out_shape=jax.ShapeDtypeStruct((t, d), F32),
        compiler_params=_cparams(("arbitrary",)),
        name="final_norm",
    )(x, moe, mod, g_final.reshape(1, d))


def _permute_cols(w):
    o = _SRC_OFF
    return jnp.concatenate([
        w[..., o["rwkv"]:o["lora"]], w[..., o["conv"]:o["qkv"]], w[..., o["gates"]:],
        w[..., o["qkv"]:o["rwkv"]], w[..., o["lora"]:o["gates"]]], axis=-1)


def kernel(x_prompt, x_sample, cache_k, cache_v, state_rwkv_fwd, state_rwkv_bwd, c, c_ctx, w_ada, b_ada, g_norm1, g_norm2, w_in, conv_w, attn_sink, rwkv_w0, rwkv_w2, rwkv_a0, rwkv_a2, rwkv_g2, rwkv_k_k, rwkv_k_a, rwkv_r_k, rwkv_ln_g, rwkv_ln_b, w_branch_conv, w_branch_attn, w_branch_rwkv, w_out, router_w, router_b, moe_w1, moe_b1, moe_w2, moe_b2, g_final):
    bc, lc, d = x_prompt.shape
    bl, tl, _ = x_sample.shape
    depth = w_in.shape[0]
    n_ctx = bc * lc
    t_all = n_ctx + bl * tl
    assert lc % TM_MIX == 0 and tl % TM_IN == 0 and n_ctx % TM_IN == 0 and t_all % TM_ROUTE == 0

    def row_of_tile(i, tm):
        return jnp.where(i < n_ctx // tm, 0, 1 + (i - n_ctx // tm) // (tl // tm))

    x = jnp.concatenate([x_prompt.reshape(n_ctx, d), x_sample.reshape(bl * tl, d)], axis=0)
    n_cond = -(-(1 + bl) // 8) * 8
    cond = jnp.zeros((n_cond, d), F32).at[0].set(c_ctx).at[1:1 + bl].set(c)
    mods = _ada_call(cond, w_ada, b_ada).reshape(depth, n_cond, N_MOD, d)

    w_in_p = _permute_cols(w_in).astype(BF16)
    head_id = np.arange(PAIR) // RWKV_HEAD
    ones_pair = jnp.asarray((head_id[:, None] == head_id[None, :]).astype(np.float32), BF16)
    cos_t, sin_t = _rope_tables(tl)
    zeros_state = jnp.zeros((bc, 2 * N_PAIR, RWKV_HEAD, RWKV_HEAD), F32)
    tiles = (n_ctx // TM_MIX, lc // TM_MIX, tl // TM_MIX)

    new_k, new_v, new_sf, new_sb = [], [], [], []
    res = None
    for l in range(depth):
        mod = mods[l]
        lw = dict(
            w0=rwkv_w0[l], w2=rwkv_w2[l], a0=rwkv_a0[l], a2=rwkv_a2[l], g2=rwkv_g2[l],
            kk_w=rwkv_k_k[l].reshape(1, -1), ka_w=rwkv_k_a[l].reshape(1, -1), rk_w=rwkv_r_k[l].reshape(1, -1),
            ln_g=rwkv_ln_g[l].reshape(1, -1), ln_b=rwkv_ln_b[l].reshape(1, -1), conv_w=conv_w[l],
            ones_pair=ones_pair,
            wa=w_branch_conv[l].astype(BF16), wb=w_branch_attn[l].astype(BF16), wc=w_branch_rwkv[l].astype(BF16),
            wo=w_out[l].astype(BF16),
        )
        if res is None:
            z = _in_call(x, w_in_p[l], g_norm1[l], mod, row_of_tile)
        else:
            z, x = _in_call(x, w_in_p[l], g_norm1[l], mod, row_of_tile, res=res)

        kv_ctx = z[:n_ctx, COL_QKV + ATT_W:COL_QKV + ATT_W + 2 * KV_W]
        new_k.append(kv_ctx[:, :KV_W].reshape(bc, lc, N_KV_HEADS, HEAD_DIM))
        new_v.append(kv_ctx[:, KV_W:].reshape(bc, lc, N_KV_HEADS, HEAD_DIM))

        ya_c = _attn_ctx_call(z, attn_sink[l], bc, lc)
        ya_l = _attn_lat_call(z, attn_sink[l], cache_k[:, l].reshape(bl, -1, KV_W), cache_v[:, l].reshape(bl, -1, KV_W),
                              cos_t, sin_t, bl, tl, n_ctx)
        y_attn = jnp.concatenate([ya_c, ya_l], axis=0)

        of_c, ob_c, zf_c, zb_c = _rwkv_call(z, lw, _state_to_z(zeros_state), _state_to_z(zeros_state), bc, lc, 0)
        of_l, ob_l, _, _ = _rwkv_call(z, lw, _state_to_z(state_rwkv_fwd[:, l].astype(F32)),
                                      _state_to_z(state_rwkv_bwd[:, l].astype(F32)), bl, tl, n_ctx)
        new_sf.append(_z_to_state(zf_c))
        new_sb.append(_z_to_state(zb_c))
        o_f = jnp.concatenate([of_c, of_l], axis=0)
        o_b = jnp.concatenate([ob_c, ob_l], axis=0)

        x_mid = _mix_call(x, z, o_f, o_b, y_attn, mod, row_of_tile, lw, tiles)
        h2, e_t, gate_t, rank_t, cnt = _route_call(x_mid, mod, row_of_tile, g_norm2[l], router_w[l], router_b[l])
        moe = _moe_layer(h2, e_t, gate_t, rank_t, cnt[:, 0], moe_w1[l], moe_b1[l], moe_w2[l], moe_b2[l])
        x = x_mid
        res = (moe, mod)

    y = _final_call(x, res[0], res[1], row_of_tile, g_final)
    y_prompt = y[:n_ctx].reshape(bc, lc, d)
    y_sample = y[n_ctx:].reshape(bl, tl, d)
    dt = x_prompt.dtype
    return (y_prompt, y_sample, jnp.stack(new_k, axis=1), jnp.stack(new_v, axis=1),
            jnp.stack(new_sf, axis=1).astype(dt), jnp.stack(new_sb, axis=1).astype(dt))
```

```python
import functools

import numpy as np
import jax
import jax.numpy as jnp
from jax import lax
from jax.experimental import pallas as pl
from jax.experimental.pallas import tpu as pltpu

F32 = jnp.float32
BF16 = jnp.bfloat16

D_MODEL = 1024
N_MOD = 6
RMS_EPS = 1e-6
CONV_W = 512
N_HEADS = 8
N_KV_HEADS = 2
GQA_GROUP = N_HEADS // N_KV_HEADS
HEAD_DIM = 64
ATT_W = N_HEADS * HEAD_DIM
KV_W = N_KV_HEADS * HEAD_DIM
WINDOW = 128
Q_BLOCK = 128
ATTN_SCALE = HEAD_DIM ** -0.5
ROPE_THETA = 10000.0
GRID_W = 64
RWKV_HEAD = 64
RWKV_W = 512
DECAY_LORA = 64
ICLR_LORA = 64
GATE_LORA = 128
GN_EPS = 64e-5
N_EXPERTS = 32
TOP_K = 4
D_EXPERT = 1024
SWIGLU_LIMIT = 7.0
SWIGLU_ALPHA = 1.702
P_TOTAL = 7296

_SRC_OFF = dict(conv=0, qkv=1536, rwkv=2304, lora=3840, gates=4224)
COL_RWKV, COL_CONV, COL_GATES, COL_QKV, COL_LORA = 0, 1536, 3072, 6144, 6912
LORA_W = 2 * DECAY_LORA + 2 * ICLR_LORA + GATE_LORA

CHUNK = 64
PAIR = 2 * RWKV_HEAD
N_PAIR = RWKV_W // PAIR
NEG_BIG = -1e30
HALO = 16
EXP_NEG_HALF = float(np.exp(-0.5))

TM_IN = 512
TN_IN = 2432
TM_MIX = 256
TM_ROUTE = 512
TM_MOE = 512
TM_DISPATCH = 512
TM_COMBINE = 256
VMEM_LIMIT = 56 * 1024 * 1024


def _cparams(sem, vmem=VMEM_LIMIT):
    return pltpu.CompilerParams(dimension_semantics=sem, vmem_limit_bytes=vmem)


def _dot(a, b, dims=(((1,), (0,)), ((), ()))):
    return lax.dot_general(a, b, dims, preferred_element_type=F32)


_NT = (((1,), (1,)), ((), ()))
_TN = (((0,), (0,)), ((), ()))


def _split(x):
    hi = x.astype(BF16)
    lo = (x - hi.astype(F32)).astype(BF16)
    return hi, lo


def _dot3(a, b, dims=(((1,), (0,)), ((), ()))):
    ah, al = _split(a)
    bh, bl = _split(b)
    return _dot(ah, bh, dims) + (_dot(ah, bl, dims) + _dot(al, bh, dims))


def _head_sums(x, ones_pair):
    rows = x.shape[0]
    xs = jnp.concatenate([x[:, PAIR * p:PAIR * (p + 1)] for p in range(N_PAIR)], axis=0).astype(BF16)
    s = _dot(xs, ones_pair)
    return jnp.concatenate([s[rows * p:rows * (p + 1)] for p in range(N_PAIR)], axis=1)


def _sigmoid(x):
    return 1.0 / (1.0 + jnp.exp(-x))


def _rms_mod(x, g, scale, shift):
    ms = jnp.mean(x * x, axis=-1, keepdims=True)
    return (x * lax.rsqrt(ms + RMS_EPS) * g) * (1.0 + scale) + shift


def _ada_kernel(cond_ref, w_ref, b_ref, o_ref):
    c = cond_ref[...]
    s = c * _sigmoid(c)
    o_ref[...] = _dot(s.astype(BF16), w_ref[...].astype(BF16)) + b_ref[...]


def _ada_call(cond, w_ada, b_ada):
    depth, d, n = w_ada.shape
    r = cond.shape[0]
    tn = 1536
    return pl.pallas_call(
        _ada_kernel,
        grid=(depth, n // tn),
        in_specs=[
            pl.BlockSpec((r, d), lambda l, j: (0, 0)),
            pl.BlockSpec((None, d, tn), lambda l, j: (l, 0, j)),
            pl.BlockSpec((None, 1, tn), lambda l, j: (l, 0, j)),
        ],
        out_specs=pl.BlockSpec((None, r, tn), lambda l, j: (l, 0, j)),
        out_shape=jax.ShapeDtypeStruct((depth, r, n), F32),
        compiler_params=_cparams(("arbitrary", "arbitrary")),
        name="ada",
    )(cond, w_ada, b_ada.reshape(depth, 1, n))


def _in_kernel(n_first, *refs):
    if n_first is None:
        x_ref, mod_ref, g_ref, w_ref, z_ref = refs
        x = x_ref[...]
    else:
        xa_ref, xb_ref, mod_ref, g_ref, w_ref, z_ref = refs
        x = jnp.where(pl.program_id(1) < n_first, xa_ref[...], xb_ref[...])
    h = _rms_mod(x, g_ref[...], mod_ref[1:2, :], mod_ref[0:1, :]).astype(BF16)
    z_ref[...] = _dot(h, w_ref[...]).astype(BF16)


def _in_call(xs, w_bf16, layer, g_norm, mod, row_of_tile):
    d = xs[0].shape[1]
    t = sum(x.shape[0] for x in xs)
    n = w_bf16.shape[2]
    tm, tn = TM_IN, TN_IN
    if len(xs) == 1:
        n_first = None
        x_specs = [pl.BlockSpec((tm, d), lambda j, i: (i, 0))]
    else:
        n_first = xs[0].shape[0] // tm
        x_specs = [pl.BlockSpec((tm, d), lambda j, i: (jnp.minimum(i, n_first - 1), 0)),
                   pl.BlockSpec((tm, d), lambda j, i: (jnp.maximum(i - n_first, 0), 0))]
    return pl.pallas_call(
        functools.partial(_in_kernel, n_first),
        grid=(n // tn, t // tm),
        in_specs=x_specs + [
            pl.BlockSpec((None, N_MOD, d), lambda j, i: (row_of_tile(i, tm), 0, 0)),
            pl.BlockSpec((1, d), lambda j, i: (0, 0)),
            pl.BlockSpec((None, d, tn), lambda j, i: (layer, 0, j)),
        ],
        out_specs=pl.BlockSpec((tm, tn), lambda j, i: (i, j)),
        out_shape=jax.ShapeDtypeStruct((t, n), BF16),
        compiler_params=_cparams(("arbitrary", "arbitrary")),
        name="in_proj",
    )(*xs, mod, g_norm.reshape(1, d), w_bf16)


def _softmax_pv(scores, values, sink):
    m = sink
    for s in scores:
        m = jnp.maximum(m, jnp.max(s, axis=-1, keepdims=True))
    den = jnp.exp(sink - m)
    acc = None
    for s, v in zip(scores, values):
        p = jnp.exp(s - m)
        den = den + jnp.sum(p, axis=-1, keepdims=True)
        pv = _dot(p.astype(BF16), v)
        acc = pv if acc is None else acc + pv
    return acc / den


def _attn_ctx_kernel(sink_ref, q_ref, kv_ref, o_ref):
    q = (q_ref[...] * ATTN_SCALE).astype(BF16)
    kv = kv_ref[...].astype(BF16)
    outs = []
    for h in range(N_HEADS):
        g = h // GQA_GROUP
        qh = q[:, HEAD_DIM * h:HEAD_DIM * (h + 1)]
        kh = kv[:, HEAD_DIM * g:HEAD_DIM * (g + 1)]
        vh = kv[:, KV_W + HEAD_DIM * g:KV_W + HEAD_DIM * (g + 1)]
        outs.append(_softmax_pv([_dot(qh, kh, _NT)], [vh], sink_ref[h]))
    o_ref[...] = jnp.concatenate(outs, axis=-1).astype(BF16)


def _attn_ctx_call(z, sink, n_seq, seq_len):
    return pl.pallas_call(
        _attn_ctx_kernel,
        grid=(n_seq,),
        in_specs=[
            pl.BlockSpec(memory_space=pltpu.SMEM),
            pl.BlockSpec((seq_len, ATT_W), lambda s: (s, COL_QKV // ATT_W)),
            pl.BlockSpec((seq_len, 2 * KV_W), lambda s: (s, (COL_QKV + ATT_W) // (2 * KV_W))),
        ],
        out_specs=pl.BlockSpec((seq_len, ATT_W), lambda s: (s, 0)),
        out_shape=jax.ShapeDtypeStruct((n_seq * seq_len, ATT_W), BF16),
        compiler_params=_cparams(("arbitrary",)),
        name="attn_ctx",
    )(sink, z, z)


def _rope(x, cos, sin_signed):
    n = x.shape[-1]
    lane = lax.broadcasted_iota(jnp.int32, x.shape, 1)
    up = pltpu.roll(x, n - 16, 1)
    dn = pltpu.roll(x, 16, 1)
    partner = jnp.where((lane & 31) < 16, up, dn)
    return x * cos + partner * sin_signed


def _attn_lat_kernel(sink_ref, q_ref, kvp_ref, kvc_ref, kvn_ref, ck_ref, cv_ref,
                     cosq_ref, sinq_ref, cosp_ref, sinp_ref, cosn_ref, sinn_ref, o_ref):
    qb = pl.program_id(1)
    nb = pl.num_programs(1)
    cq, sq = cosq_ref[...], sinq_ref[...]
    q = _rope(q_ref[...].astype(F32), jnp.concatenate([cq] * 4, axis=1), jnp.concatenate([sq] * 4, axis=1))
    q = (q * ATTN_SCALE).astype(BF16)
    kvp, kvc, kvn = (ref[...].astype(F32) for ref in (kvp_ref, kvc_ref, kvn_ref))
    kp = _rope(kvp[:, :KV_W], cosp_ref[...], sinp_ref[...]).astype(BF16)
    kc = _rope(kvc[:, :KV_W], cq, sq).astype(BF16)
    kn = _rope(kvn[:, :KV_W], cosn_ref[...], sinn_ref[...]).astype(BF16)
    vp, vc, vn = (t[:, KV_W:].astype(BF16) for t in (kvp, kvc, kvn))
    ck = ck_ref[...].astype(BF16)
    cv = cv_ref[...].astype(BF16)
    k_cat = jnp.concatenate([kp, kc, kn, ck], axis=0)
    v_cat = jnp.concatenate([vp, vc, vn, cv], axis=0)
    nk = k_cat.shape[0]
    qi = lax.broadcasted_iota(jnp.int32, (Q_BLOCK, nk), 0)
    kj = lax.broadcasted_iota(jnp.int32, (Q_BLOCK, nk), 1)
    ok_p = (kj >= qi) & (qb > 0)
    ok_n = (kj - 2 * Q_BLOCK <= qi) & (qb < nb - 1)
    in_next = (kj >= 2 * Q_BLOCK) & (kj < 3 * Q_BLOCK)
    bias = jnp.where(kj < Q_BLOCK, jnp.where(ok_p, 0.0, NEG_BIG), jnp.where(in_next, jnp.where(ok_n, 0.0, NEG_BIG), 0.0))
    bias = jnp.concatenate([bias] * GQA_GROUP, axis=0)
    groups = range(N_KV_HEADS)
    heads = [range(GQA_GROUP * g, GQA_GROUP * (g + 1)) for g in groups]
    gsl = [slice(HEAD_DIM * g, HEAD_DIM * (g + 1)) for g in groups]
    q_g = [jnp.concatenate([q[:, HEAD_DIM * h:HEAD_DIM * (h + 1)] for h in heads[g]], axis=0) for g in groups]
    sink = [jnp.concatenate([jnp.full((Q_BLOCK, 1), sink_ref[h], F32) for h in heads[g]], axis=0) for g in groups]
    s = [_dot(q_g[g], k_cat[:, gsl[g]], _NT) + bias for g in groups]
    m = [jnp.maximum(jnp.max(s[g], axis=-1, keepdims=True), sink[g]) for g in groups]
    p = [jnp.exp(s[g] - m[g]) for g in groups]
    den = [jnp.sum(p[g], axis=-1, keepdims=True) + jnp.exp(sink[g] - m[g]) for g in groups]
    o = [_dot(p[g].astype(BF16), v_cat[:, gsl[g]]) / den[g] for g in groups]
    outs = [o[g][Q_BLOCK * i:Q_BLOCK * (i + 1)] for g in groups for i in range(GQA_GROUP)]
    o_ref[...] = jnp.concatenate(outs, axis=-1).astype(BF16)


def _attn_lat_call(z, sink, cache_k, cache_v, cos_t, sin_t, n_seq, seq_len, row0):
    nb = seq_len // Q_BLOCK
    base = row0 // Q_BLOCK
    past = cache_k.shape[1]

    def rows(off):
        return lambda b, i: (base + b * nb + jnp.clip(i + off, 0, nb - 1))

    def kv_spec(off):
        r = rows(off)
        return pl.BlockSpec((Q_BLOCK, 2 * KV_W), lambda b, i: (r(b, i), (COL_QKV + ATT_W) // (2 * KV_W)))

    def tab_spec(off):
        return pl.BlockSpec((Q_BLOCK, KV_W), lambda b, i: (jnp.clip(i + off, 0, nb - 1), 0))

    r0 = rows(0)
    return pl.pallas_call(
        _attn_lat_kernel,
        grid=(n_seq, nb),
        in_specs=[
            pl.BlockSpec(memory_space=pltpu.SMEM),
            pl.BlockSpec((Q_BLOCK, ATT_W), lambda b, i: (r0(b, i), COL_QKV // ATT_W)),
            kv_spec(-1), kv_spec(0), kv_spec(1),
            pl.BlockSpec((None, past, KV_W), lambda b, i: (b, 0, 0)),
            pl.BlockSpec((None, past, KV_W), lambda b, i: (b, 0, 0)),
            tab_spec(0), tab_spec(0), tab_spec(-1), tab_spec(-1), tab_spec(1), tab_spec(1),
        ],
        out_specs=pl.BlockSpec((Q_BLOCK, ATT_W), lambda b, i: (b * nb + i, 0)),
        out_shape=jax.ShapeDtypeStruct((n_seq * seq_len, ATT_W), BF16),
        compiler_params=_cparams(("arbitrary", "arbitrary")),
        name="attn_lat",
    )(sink, z, z, z, z, cache_k, cache_v, cos_t, sin_t, cos_t, sin_t, cos_t, sin_t)


def _rope_tables(seq_len):
    half = HEAD_DIM // 2
    pos = np.arange(seq_len)
    inv_freq = 1.0 / (ROPE_THETA ** (np.arange(0, half, 2, dtype=np.float32) / half))
    inv_freq = inv_freq.astype(np.float32)

    def part(p):
        ang = (p.astype(np.float32)[:, None] * inv_freq[None, :]).astype(np.float32)
        c, s = np.cos(ang), np.sin(ang)
        return np.concatenate([c, c], axis=1), np.concatenate([-s, s], axis=1)

    c_r, s_r = part(pos // GRID_W)
    c_c, s_c = part(pos % GRID_W)
    cos = np.concatenate([c_r, c_c] * N_KV_HEADS, axis=1).astype(np.float32)
    sin = np.concatenate([s_r, s_c] * N_KV_HEADS, axis=1).astype(np.float32)
    return jnp.asarray(cos), jnp.asarray(sin)


def _rwkv_dir_inputs(zm, zl, d, w0, w2, a0, a2, kk_w, ka_w, ones_pair):
    r = zm[:, :RWKV_W]
    kraw = zm[:, RWKV_W:2 * RWKV_W]
    v = zm[:, 2 * RWKV_W:]
    wl = zl[:, DECAY_LORA * d:DECAY_LORA * (d + 1)]
    al = zl[:, 2 * DECAY_LORA + ICLR_LORA * d:2 * DECAY_LORA + ICLR_LORA * (d + 1)]
    xw = w0 + _dot(jnp.tanh(wl).astype(BF16), w2.astype(BF16))
    ld = -EXP_NEG_HALF * _sigmoid(xw)
    a = _sigmoid(a0 + _dot(al.astype(BF16), a2.astype(BF16)))
    k = kraw * (1.0 + (a - 1.0) * ka_w)
    kkr = kraw * kk_w
    n2 = _head_sums(kkr * kkr, ones_pair)
    kk = kkr / jnp.maximum(jnp.sqrt(n2), 1e-12)
    return r, v, kk, ld, a, k


def _tri_masks(rev):
    c = CHUNK
    ti = lax.broadcasted_iota(jnp.int32, (c, c), 0)
    si = lax.broadcasted_iota(jnp.int32, (c, c), 1)
    incl = (si >= ti) if rev else (si <= ti)
    strict = (si > ti) if rev else (si < ti)
    return incl, strict, (si == ti).astype(F32)


def _chunk_prepare(r, v, kk, ld, a, k, rev):
    incl, _, _ = _tri_masks(rev)
    m_incl = jnp.where(incl, 1.0, 0.0).astype(BF16)
    ldh, ldl = _split(ld)
    cin = _dot(m_incl, ldh) + _dot(m_incl, ldl)
    tot = jnp.sum(ld, axis=0, keepdims=True)
    e_neg = jnp.exp(-cin)
    e_end = jnp.exp(tot - cin)
    bb = kk * a
    return dict(
        a_m=kk * jnp.exp(cin - ld), r_m=r * jnp.exp(cin),
        b_m=(bb * e_neg).astype(BF16), k_m=(k * e_neg).astype(BF16),
        b_end=(bb * e_end).astype(BF16), k_end=(k * e_end).astype(BF16),
        v=v, e_tot=jnp.exp(tot), rev=rev)


def _chunk_problems(probs):
    c = CHUNK
    n = len(probs)
    rng = range(n)
    lane = lax.broadcasted_iota(jnp.int32, (1, PAIR), 1)
    first_head = lane < RWKV_HEAD

    def bd(x):
        xb = x.astype(BF16)
        zero = jnp.zeros_like(xb)
        return jnp.concatenate([jnp.where(first_head, xb, zero), jnp.where(first_head, zero, xb)], axis=0)

    ti = lax.broadcasted_iota(jnp.int32, (c, PAIR), 0)
    si = lax.broadcasted_iota(jnp.int32, (c, PAIR), 1) & (RWKV_HEAD - 1)
    incl = {False: si <= ti, True: si >= ti}
    strict = {False: si < ti, True: si > ti}
    eye = (si == ti).astype(F32)
    rev = [p["rev"] for p in probs]

    lhs = [jnp.concatenate([probs[i]["a_m"], probs[i]["r_m"]], axis=0).astype(BF16) for i in rng]
    xb = [_dot(lhs[i], bd(probs[i]["b_m"]), _NT) for i in rng]
    xk = [_dot(lhs[i], bd(probs[i]["k_m"]), _NT) for i in rng]
    m_ab = [jnp.where(strict[rev[i]], xb[i][:c], 0.0) for i in rng]
    m_ak = [jnp.where(strict[rev[i]], xk[i][:c], 0.0).astype(BF16) for i in rng]
    m_rb = [jnp.where(incl[rev[i]], xb[i][c:], 0.0).astype(BF16) for i in rng]
    m_rk = [jnp.where(incl[rev[i]], xk[i][c:], 0.0).astype(BF16) for i in rng]
    v_bd = [bd(probs[i]["v"]) for i in rng]
    mak_v = [_dot(m_ak[i], v_bd[i]) for i in rng]
    mrk_v = [_dot(m_rk[i], v_bd[i]) for i in rng]
    t_inv = [eye - m_ab[i] for i in rng]
    lp = list(m_ab)
    for _ in range(5):
        lp = [_dot(lp[i].astype(BF16), bd(lp[i])) for i in rng]
        t_inv = [t_inv[i] + _dot(t_inv[i].astype(BF16), bd(lp[i])) for i in rng]
    wu = [_dot(t_inv[i].astype(BF16), jnp.concatenate([bd(probs[i]["a_m"]), bd(mak_v[i])], axis=1)) for i in rng]
    ro = [_dot(m_rb[i], jnp.concatenate([bd(wu[i][:, :PAIR]), bd(wu[i][:, PAIR:])], axis=1)) for i in rng]
    ri = lax.broadcasted_iota(jnp.int32, (PAIR, PAIR), 0)
    ci = lax.broadcasted_iota(jnp.int32, (PAIR, PAIR), 1)
    same_head = (ri < RWKV_HEAD) == (ci < RWKV_HEAD)
    bt_wu = [_dot(probs[i]["b_end"], wu[i].astype(BF16), _TN) for i in rng]
    kt_v = [_dot(probs[i]["k_end"], probs[i]["v"].astype(BF16), _TN) for i in rng]
    results = []
    for i in rng:
        r_eff = probs[i]["r_m"] - ro[i][:, :PAIR]
        o_loc = mrk_v[i] - ro[i][:, PAIR:]
        g_t = jnp.where(same_head, jnp.where(ri == ci, probs[i]["e_tot"], 0.0) - bt_wu[i][:, :PAIR], 0.0)
        h_t = jnp.where(same_head, kt_v[i] - bt_wu[i][:, PAIR:], 0.0)
        zh, zl = _split(probs[i]["z"])
        both = jnp.concatenate([r_eff, g_t], axis=0)
        bh, bl = _split(both)
        prod = _dot(bh, zh) + (_dot(bh, zl) + _dot(bl, zh))
        results.append((prod[:c] + o_loc, prod[c:] + h_t))
    return results


def _rwkv_kernel(zmf_ref, zlf_ref, zmb_ref, zlb_ref, w0_ref, w2_ref, a0_ref, a2_ref, kkw_ref, kaw_ref,
                 ones_ref, s0f_ref, s0b_ref, of_ref, ob_ref, zf_ref, zb_ref, zf_scr, zb_scr):
    @pl.when(pl.program_id(1) == 0)
    def _():
        zf_scr[...] = s0f_ref[...]
        zb_scr[...] = s0b_ref[...]

    ones_pair = ones_ref[...]
    dirs = ((zmf_ref, zlf_ref, zf_scr, of_ref), (zmb_ref, zlb_ref, zb_scr, ob_ref))
    probs = []
    for d, (zm_ref, zl_ref, z_scr, _) in enumerate(dirs):
        r, v, kk, ld, a, k = _rwkv_dir_inputs(zm_ref[...].astype(F32), zl_ref[...].astype(F32), d, w0_ref[d:d + 1, :], w2_ref[d],
                                              a0_ref[d:d + 1, :], a2_ref[d], kkw_ref[...], kaw_ref[...], ones_pair)
        full = _chunk_prepare(r, v, kk, ld, a, k, d == 1)
        for p in range(N_PAIR):
            ps = slice(PAIR * p, PAIR * (p + 1))
            prob = {key: (val[:, ps] if hasattr(val, "shape") else val) for key, val in full.items()}
            prob["z"] = z_scr[p]
            probs.append(prob)
    results = _chunk_problems(probs)
    for d, (_, _, z_scr, o_ref) in enumerate(dirs):
        for p in range(N_PAIR):
            out, z_new = results[d * N_PAIR + p]
            o_ref[:, PAIR * p:PAIR * (p + 1)] = out
            z_scr[p] = z_new
    zf_ref[...] = zf_scr[...]
    zb_ref[...] = zb_scr[...]


def _rwkv_call(z, lw, s0f, s0b, n_seq, seq_len, row0):
    nc = seq_len // CHUNK
    base = row0 // CHUNK

    def fwd(s, c):
        return base + s * nc + c

    def bwd(s, c):
        return base + s * nc + (nc - 1 - c)

    def zm_spec(f):
        return pl.BlockSpec((CHUNK, 3 * RWKV_W), lambda s, c: (f(s, c), COL_RWKV // (3 * RWKV_W)))

    def zl_spec(f):
        return pl.BlockSpec((CHUNK, LORA_W), lambda s, c: (f(s, c), COL_LORA // LORA_W))

    def full(shape):
        return pl.BlockSpec(shape, lambda s, c: (0,) * len(shape))

    st_spec = pl.BlockSpec((None, N_PAIR, PAIR, PAIR), lambda s, c: (s, 0, 0, 0))
    o_shape = jax.ShapeDtypeStruct((n_seq * seq_len, RWKV_W), F32)
    st_shape = jax.ShapeDtypeStruct((n_seq, N_PAIR, PAIR, PAIR), F32)
    return pl.pallas_call(
        _rwkv_kernel,
        grid=(n_seq, nc),
        in_specs=[
            zm_spec(fwd), zl_spec(fwd), zm_spec(bwd), zl_spec(bwd),
            full((2, RWKV_W)), full((2, DECAY_LORA, RWKV_W)), full((2, RWKV_W)), full((2, ICLR_LORA, RWKV_W)),
            full((1, RWKV_W)), full((1, RWKV_W)), full((PAIR, PAIR)),
            st_spec, st_spec,
        ],
        out_specs=[
            pl.BlockSpec((CHUNK, RWKV_W), lambda s, c: (s * nc + c, 0)),
            pl.BlockSpec((CHUNK, RWKV_W), lambda s, c: (s * nc + (nc - 1 - c), 0)),
            st_spec, st_spec,
        ],
        out_shape=[o_shape, o_shape, st_shape, st_shape],
        scratch_shapes=[pltpu.VMEM((N_PAIR, PAIR, PAIR), F32), pltpu.VMEM((N_PAIR, PAIR, PAIR), F32)],
        compiler_params=_cparams(("arbitrary", "arbitrary")),
        name="rwkv_scan",
    )(z, z, z, z, lw["w0"], lw["w2"], lw["a0"], lw["a2"], lw["kk_w"], lw["ka_w"], lw["ones_pair"], s0f, s0b)


def _state_to_z(s):
    n = s.shape[0]
    st = jnp.swapaxes(s, -1, -2).reshape(n, N_PAIR, 2, RWKV_HEAD, RWKV_HEAD)
    zero = jnp.zeros_like(st[:, :, 0])
    top = jnp.concatenate([st[:, :, 0], zero], axis=-1)
    bot = jnp.concatenate([zero, st[:, :, 1]], axis=-1)
    return jnp.concatenate([top, bot], axis=-2)


def _z_to_state(z):
    n = z.shape[0]
    h0 = z[:, :, :RWKV_HEAD, :RWKV_HEAD]
    h1 = z[:, :, RWKV_HEAD:, RWKV_HEAD:]
    st = jnp.stack([h0, h1], axis=2).reshape(n, 2 * N_PAIR, RWKV_HEAD, RWKV_HEAD)
    return jnp.swapaxes(st, -1, -2)


def _mix_kernel(tiles, n_x, *refs):
    n_ctx_tiles, per_ctx, per_lat = tiles
    i = pl.program_id(0)
    in_ctx = i < n_ctx_tiles

    def pick(pair):
        return jnp.where(in_ctx, pair[0][...], pair[1][...])

    x_refs, refs = refs[:n_x], refs[n_x:]
    (zc_ref, zcp_ref, zcn_ref, zm_ref, zl_ref, zg_ref, ofc_ref, ofl_ref, obc_ref, obl_ref, yac_ref, yal_ref, mod_ref,
     convw_ref, a0_ref, a2_ref, g2_ref, kaw_ref, rkw_ref, lng_ref, lnb_ref, ones_ref,
     wa_ref, wb_ref, wc_ref, wo_ref, o_ref) = refs
    x_in = x_refs[0][...] if n_x == 1 else pick(x_refs)
    pos = jnp.where(i < n_ctx_tiles, i % per_ctx, (i - n_ctx_tiles) % per_lat)
    per = jnp.where(i < n_ctx_tiles, per_ctx, per_lat)
    tm = o_ref.shape[0]

    zc = zc_ref[...].astype(F32)
    u = zc[:, 2 * CONV_W:] * zc[:, :CONV_W]
    zp = zcp_ref[...].astype(F32)
    zn = zcn_ref[...].astype(F32)
    u_prev_row = jnp.where(pos > 0, zp[HALO - 1:HALO, 2 * CONV_W:] * zp[HALO - 1:HALO, :CONV_W], 0.0)
    u_next_row = jnp.where(pos < per - 1, zn[0:1, 2 * CONV_W:] * zn[0:1, :CONV_W], 0.0)
    row = lax.broadcasted_iota(jnp.int32, (tm, CONV_W), 0)
    u_prev = jnp.where(row == 0, u_prev_row, pltpu.roll(u, 1, 0))
    u_next = jnp.where(row == tm - 1, u_next_row, pltpu.roll(u, tm - 1, 0))
    cw = convw_ref[...]
    y_conv = zc[:, CONV_W:2 * CONV_W] * (cw[0:1, :] * u_prev + cw[1:2, :] * u + cw[2:3, :] * u_next)

    zm = zm_ref[...].astype(F32)
    zl = zl_ref[...].astype(F32)
    r = zm[:, :RWKV_W]
    kraw = zm[:, RWKV_W:2 * RWKV_W]
    v = zm[:, 2 * RWKV_W:]
    ones_pair = ones_ref[...]
    o = pick((ofc_ref, ofl_ref)) + pick((obc_ref, obl_ref))
    mu = _head_sums(o, ones_pair) * (1.0 / RWKV_HEAD)
    dlt = o - mu
    var = _head_sums(dlt * dlt, ones_pair) * (1.0 / RWKV_HEAD)
    y = dlt * lax.rsqrt(var + GN_EPS) * lng_ref[...] + lnb_ref[...]
    for d in range(2):
        al = zl[:, 2 * DECAY_LORA + ICLR_LORA * d:2 * DECAY_LORA + ICLR_LORA * (d + 1)]
        a = _sigmoid(a0_ref[d:d + 1, :] + _dot(al.astype(BF16), a2_ref[d].astype(BF16)))
        k = kraw * (1.0 + (a - 1.0) * kaw_ref[...])
        y = y + _head_sums(r * k * rkw_ref[...], ones_pair) * v
    g1 = zl[:, 2 * DECAY_LORA + 2 * ICLR_LORA:]
    y_rwkv = y * _dot(_sigmoid(g1).astype(BF16), g2_ref[...].astype(BF16))

    zg = zg_ref[...].astype(F32)
    merged = (_sigmoid(zg[:, :D_MODEL]) * _dot(y_conv.astype(BF16), wa_ref[...])
              + _sigmoid(zg[:, D_MODEL:2 * D_MODEL]) * _dot(pick((yac_ref, yal_ref)), wb_ref[...])
              + _sigmoid(zg[:, 2 * D_MODEL:]) * _dot(y_rwkv.astype(BF16), wc_ref[...]))
    o_ref[...] = x_in + mod_ref[2:3, :] * _dot(merged.astype(BF16), wo_ref[...])


def _mix_call(xs, z, o_f, o_b, y_attn, mod, row_of_tile, lw, tiles):
    t, d = z.shape[0], xs[0].shape[1]
    tm = TM_MIX
    nt = t // tm
    hb = tm // HALO
    n_ctx_tiles = tiles[0]

    def rows(w, col):
        return pl.BlockSpec((tm, w), lambda i: (i, col // w))

    def pair(w):
        return [pl.BlockSpec((tm, w), lambda i: (jnp.minimum(i, n_ctx_tiles - 1), 0)),
                pl.BlockSpec((tm, w), lambda i: (jnp.maximum(i - n_ctx_tiles, 0), 0))]

    def full(shape):
        return pl.BlockSpec(shape, lambda i: (0,) * len(shape))

    in_specs = (pair(d) if len(xs) == 2 else [rows(d, 0)]) + [
        rows(3 * CONV_W, COL_CONV),
        pl.BlockSpec((HALO, 3 * CONV_W), lambda i: (jnp.maximum(i * hb - 1, 0), COL_CONV // (3 * CONV_W))),
        pl.BlockSpec((HALO, 3 * CONV_W), lambda i: (jnp.minimum((i + 1) * hb, nt * hb - 1), COL_CONV // (3 * CONV_W))),
        rows(3 * RWKV_W, COL_RWKV),
        rows(LORA_W, COL_LORA),
        rows(3 * D_MODEL, COL_GATES),
    ] + pair(RWKV_W) + pair(RWKV_W) + pair(ATT_W) + [
        pl.BlockSpec((None, N_MOD, d), lambda i: (row_of_tile(i, tm), 0, 0)),
        full((3, CONV_W)), full((2, RWKV_W)), full((2, ICLR_LORA, RWKV_W)), full((GATE_LORA, RWKV_W)),
        full((1, RWKV_W)), full((1, RWKV_W)), full((1, RWKV_W)), full((1, RWKV_W)),
        full((PAIR, PAIR)),
        full((CONV_W, d)), full((ATT_W, d)), full((RWKV_W, d)), full((d, d)),
    ]
    return pl.pallas_call(
        functools.partial(_mix_kernel, tiles, len(xs)),
        grid=(nt,),
        in_specs=in_specs,
        out_specs=rows(d, 0),
        out_shape=jax.ShapeDtypeStruct((t, d), F32),
        compiler_params=_cparams(("arbitrary",)),
        name="mix",
    )(*xs, z, z, z, z, z, z, *o_f, *o_b, *y_attn, mod,
      lw["conv_w"], lw["a0"], lw["a2"], lw["g2"], lw["ka_w"], lw["rk_w"], lw["ln_g"], lw["ln_b"],
      lw["ones_pair"], lw["wa"], lw["wb"], lw["wc"], lw["wo"])


def _route_kernel(x_ref, mod_ref, g_ref, rwt_ref, rb_ref, tri_ref, h_ref, e_ref, gate_ref, rank_ref, cnt_ref, cnt_scr):
    @pl.when(pl.program_id(0) == 0)
    def _():
        cnt_scr[...] = jnp.zeros_like(cnt_scr)

    h2 = _rms_mod(x_ref[...], g_ref[...], mod_ref[4:5, :], mod_ref[3:4, :])
    h_ref[...] = h2.reshape(h_ref.shape)
    logits = _dot3(rwt_ref[...], h2, _NT) + rb_ref[...]
    ne, tm = logits.shape
    ex = lax.broadcasted_iota(jnp.int32, (ne, tm), 0)
    work = logits
    vals, hots = [], []
    for kq in range(TOP_K):
        m = jnp.max(work, axis=0, keepdims=True)
        idx = jnp.min(jnp.where(work == m, ex, ne), axis=0, keepdims=True)
        hot = ex == idx
        vals.append(m)
        hots.append(hot)
        e_ref[kq:kq + 1, :] = idx
        work = jnp.where(hot, -jnp.inf, work)
    exps = [jnp.exp(vk - vals[0]) for vk in vals]
    den = exps[0] + exps[1] + exps[2] + exps[3]
    chosen = jnp.where(hots[0] | hots[1] | hots[2] | hots[3], 1.0, 0.0)
    before = cnt_scr[:, 0:1] + _dot(chosen.astype(BF16), tri_ref[...])
    for kq in range(TOP_K):
        gate_ref[kq:kq + 1, :] = exps[kq] / den
        rank_ref[kq:kq + 1, :] = jnp.sum(jnp.where(hots[kq], before, 0.0), axis=0, keepdims=True).astype(jnp.int32)
    cnt_scr[...] = cnt_scr[...] + jnp.sum(chosen, axis=1, keepdims=True)
    cnt_ref[...] = cnt_scr[...].astype(jnp.int32)


def _route_call(x, mod, row_of_tile, g_norm, router_w, router_b):
    t, d = x.shape
    tm = TM_ROUTE
    ne = router_w.shape[1]
    tri = jnp.triu(jnp.ones((tm, tm), F32), 1).astype(BF16)

    def full(shape):
        return pl.BlockSpec(shape, lambda i: (0,) * len(shape))

    kt_spec = pl.BlockSpec((TOP_K, tm), lambda i: (0, i))
    return pl.pallas_call(
        _route_kernel,
        grid=(t // tm,),
        in_specs=[
            pl.BlockSpec((tm, d), lambda i: (i, 0)),
            pl.BlockSpec((None, N_MOD, d), lambda i: (row_of_tile(i, tm), 0, 0)),
            full((1, d)), full((ne, d)), full((ne, 1)), full((tm, tm)),
        ],
        out_specs=[pl.BlockSpec((tm, 1, d), lambda i: (i, 0, 0)), kt_spec, kt_spec, kt_spec, full((ne, 128))],
        out_shape=[
            jax.ShapeDtypeStruct((t, 1, d), F32),
            jax.ShapeDtypeStruct((TOP_K, t), jnp.int32),
            jax.ShapeDtypeStruct((TOP_K, t), F32),
            jax.ShapeDtypeStruct((TOP_K, t), jnp.int32),
            jax.ShapeDtypeStruct((ne, 128), jnp.int32),
        ],
        scratch_shapes=[pltpu.VMEM((ne, 128), F32)],
        compiler_params=_cparams(("arbitrary",)),
        name="route",
    )(x, mod, g_norm.reshape(1, d), router_w.T, router_b.reshape(ne, 1), tri)


def _row_copy_wait(buf_hbm, n_rows, sem):
    view = buf_hbm.at[pl.ds(0, n_rows)]
    pltpu.make_async_copy(view, view, sem).wait()


def _dispatch_kernel(t_all, dest_ref, h_ref, rows_hbm, sem):
    tm = TM_DISPATCH
    t0 = pl.program_id(0) * tm

    def body(t, carry):
        for kq in range(TOP_K):
            pltpu.make_async_copy(h_ref.at[t], rows_hbm.at[dest_ref[kq * t_all + t0 + t]], sem).start(priority=kq % 2)
        return carry

    lax.fori_loop(0, tm, body, 0)
    _row_copy_wait(rows_hbm, TOP_K * tm, sem)


def _dispatch_call(h3, dest_flat):
    t, _, d = h3.shape
    grid_spec = pltpu.PrefetchScalarGridSpec(
        num_scalar_prefetch=1,
        grid=(t // TM_DISPATCH,),
        in_specs=[pl.BlockSpec((TM_DISPATCH, 1, d), lambda i, dr: (i, 0, 0))],
        out_specs=pl.BlockSpec(memory_space=pl.ANY),
        scratch_shapes=[pltpu.SemaphoreType.DMA(())],
    )
    return pl.pallas_call(
        functools.partial(_dispatch_kernel, t),
        grid_spec=grid_spec,
        out_shape=jax.ShapeDtypeStruct((TOP_K * t, 1, d), F32),
        compiler_params=_cparams(("arbitrary",)),
        name="moe_dispatch",
    )(dest_flat, h3)


def _moe_kernel(blk_ref, exp_ref, lo_ref, hi_ref, fblk_ref, fexp_ref, nit_ref,
                x_ref, w1_ref, b1_ref, w2_ref, b2_ref, o_ref, w1_scr, w2_scr, x_scr, acc_scr):
    del blk_ref, exp_ref
    it = pl.program_id(0)

    @pl.when(fexp_ref[it] == 1)
    def _():
        w1_scr[...] = w1_ref[...].astype(BF16)
        w2_scr[...] = w2_ref[...].astype(BF16)

    @pl.when(it < nit_ref[0])
    def _():
        x_scr[...] = x_ref[...].reshape(x_scr.shape)
        hm = _dot(x_scr[...].astype(BF16), w1_scr[...]) + b1_ref[...]
        glu = jnp.minimum(hm[:, :D_EXPERT], SWIGLU_LIMIT)
        lin = jnp.clip(hm[:, D_EXPERT:], -SWIGLU_LIMIT, SWIGLU_LIMIT)
        act = glu * _sigmoid(SWIGLU_ALPHA * glu) * (lin + 1.0)
        y = _dot(act.astype(BF16), w2_scr[...]) + b2_ref[...]
        row = lax.broadcasted_iota(jnp.int32, (y.shape[0], 1), 0)
        y = jnp.where((row >= lo_ref[it]) & (row < hi_ref[it]), y, 0.0)

        @pl.when(fblk_ref[it] == 1)
        def _():
            acc_scr[...] = y

        @pl.when(fblk_ref[it] == 0)
        def _():
            acc_scr[...] = acc_scr[...] + y

        o_ref[...] = acc_scr[...].reshape(o_ref.shape)


def _moe_call(rows, plan, layer, w1, b1, w2, b2):
    n_rows, _, d = rows.shape
    tm = TM_MOE
    depth, ne, _, dh2 = w1.shape
    de = w2.shape[2]
    n_items = plan[0].shape[0]

    def row_map(it, blk, ex, lo, hi, fb, fe, nit):
        return (blk[it], 0, 0)

    def w_map(it, blk, ex, lo, hi, fb, fe, nit):
        return (layer, ex[it], 0, 0)

    grid_spec = pltpu.PrefetchScalarGridSpec(
        num_scalar_prefetch=7,
        grid=(n_items,),
        in_specs=[
            pl.BlockSpec((tm, 1, d), row_map),
            pl.BlockSpec((None, None, d, dh2), w_map),
            pl.BlockSpec((None, None, 1, dh2), w_map),
            pl.BlockSpec((None, None, de, d), w_map),
            pl.BlockSpec((None, None, 1, d), w_map),
        ],
        out_specs=pl.BlockSpec((tm, 1, d), row_map),
        scratch_shapes=[pltpu.VMEM((d, dh2), BF16), pltpu.VMEM((de, d), BF16), pltpu.VMEM((tm, d), F32),
                        pltpu.VMEM((tm, d), F32)],
    )
    return pl.pallas_call(
        _moe_kernel,
        grid_spec=grid_spec,
        out_shape=jax.ShapeDtypeStruct((n_rows, 1, d), F32),
        compiler_params=_cparams(("arbitrary",)),
        name="moe",
    )(*plan, rows, w1, b1.reshape(depth, ne, 1, dh2), w2, b2.reshape(depth, ne, 1, d))


def _moe_plan(counts, n_rows):
    tm = TM_MOE
    ne = counts.shape[0]
    n_items = n_rows // tm + ne - 1
    counts = counts.astype(jnp.int32)
    ends = jnp.cumsum(counts)
    starts = ends - counts
    first_blk = starts // tm
    last_blk = jnp.maximum(ends - 1, 0) // tm
    per_exp = jnp.where(counts > 0, last_blk - first_blk + 1, 0)
    item_end = jnp.cumsum(per_exp)
    item_off = item_end - per_exp
    total = item_end[-1]
    it = jnp.arange(n_items, dtype=jnp.int32)
    itc = jnp.minimum(it, total - 1)
    ex = jnp.minimum(jnp.sum((itc[:, None] >= item_end[None, :]).astype(jnp.int32), axis=1), ne - 1)
    hot = ex[:, None] == jnp.arange(ne, dtype=jnp.int32)[None, :]

    def pick(v):
        return jnp.sum(jnp.where(hot, v[None, :], 0), axis=1)

    blk = pick(first_blk) + itc - pick(item_off)
    lo = jnp.maximum(pick(starts), blk * tm) - blk * tm
    hi = jnp.where(it < total, jnp.minimum(pick(ends), (blk + 1) * tm) - blk * tm, lo)
    one = jnp.ones((1,), jnp.int32)
    f_blk = jnp.concatenate([one, (blk[1:] != blk[:-1]).astype(jnp.int32)])
    f_exp = jnp.concatenate([one, (ex[1:] != ex[:-1]).astype(jnp.int32)])
    return (blk, ex, lo, hi, f_blk, f_exp, total.reshape(1)), starts


def _combine_kernel(t_all, dest_ref, y_hbm, gate_ref, x_ref, mod_ref, o_ref, buf, y_scr, sem):
    tm = TM_COMBINE
    t0 = pl.program_id(0) * tm

    def body(t, carry):
        for kq in range(TOP_K):
            pltpu.make_async_copy(y_hbm.at[dest_ref[kq * t_all + t0 + t]], buf.at[kq * tm + t], sem).start(priority=kq % 2)
        return carry

    lax.fori_loop(0, tm, body, 0)
    pltpu.make_async_copy(y_hbm.at[pl.ds(0, TOP_K * tm)], buf, sem).wait()
    y_scr[...] = buf[...].reshape(y_scr.shape)
    gate = gate_ref[...]
    acc = gate[:, 0:1] * y_scr[0:tm, :]
    for kq in range(1, TOP_K):
        acc = acc + gate[:, kq:kq + 1] * y_scr[kq * tm:(kq + 1) * tm, :]
    o_ref[...] = x_ref[...] + mod_ref[5:6, :] * acc


def _combine_call(y_rows, dest_flat, gate_tk, x, mod, row_of_tile):
    t = gate_tk.shape[0]
    d = y_rows.shape[-1]
    tm = TM_COMBINE
    grid_spec = pltpu.PrefetchScalarGridSpec(
        num_scalar_prefetch=1,
        grid=(t // tm,),
        in_specs=[pl.BlockSpec(memory_space=pl.ANY), pl.BlockSpec((tm, TOP_K), lambda i, dr: (i, 0)),
                  pl.BlockSpec((tm, d), lambda i, dr: (i, 0)),
                  pl.BlockSpec((None, N_MOD, d), lambda i, dr: (row_of_tile(i, tm), 0, 0))],
        out_specs=pl.BlockSpec((tm, d), lambda i, dr: (i, 0)),
        scratch_shapes=[pltpu.VMEM((TOP_K * tm, 1, d), F32), pltpu.VMEM((TOP_K * tm, d), F32),
                        pltpu.SemaphoreType.DMA(())],
    )
    return pl.pallas_call(
        functools.partial(_combine_kernel, t),
        grid_spec=grid_spec,
        out_shape=jax.ShapeDtypeStruct((t, d), F32),
        compiler_params=_cparams(("arbitrary",)),
        name="moe_combine",
    )(dest_flat, y_rows, gate_tk, x, mod)


def _moe_layer(x_mid, mod, row_of_tile, h3, e_t, gate_t, rank_t, counts, layer, w1, b1, w2, b2):
    t = h3.shape[0]
    ne = w1.shape[1]
    plan, starts = _moe_plan(counts, TOP_K * t)
    start_of = jnp.sum(jnp.where(e_t[..., None] == jnp.arange(ne, dtype=jnp.int32), starts, 0), axis=-1)
    dest_flat = (start_of + rank_t).reshape(-1)
    rows = _dispatch_call(h3, dest_flat)
    y_rows = _moe_call(rows, plan, layer, w1, b1, w2, b2)
    return _combine_call(y_rows, dest_flat, gate_t.T, x_mid, mod, row_of_tile)


def _final_kernel(x_ref, g_ref, o_ref):
    x = x_ref[...]
    ms = jnp.mean(x * x, axis=-1, keepdims=True)
    o_ref[...] = x * lax.rsqrt(ms + RMS_EPS) * g_ref[...]


def _final_call(x, g_final):
    t, d = x.shape
    tm = TM_IN
    spec = pl.BlockSpec((tm, d), lambda i: (i, 0))
    return pl.pallas_call(
        _final_kernel,
        grid=(t // tm,),
        in_specs=[spec, pl.BlockSpec((1, d), lambda i: (0, 0))],
        out_specs=spec,
        out_shape=jax.ShapeDtypeStruct((t, d), F32),
        compiler_params=_cparams(("arbitrary",)),
        name="final_norm",
    )(x, g_final.reshape(1, d))


def _permute_cols(w):
    o = _SRC_OFF
    return jnp.concatenate([
        w[..., o["rwkv"]:o["lora"]], w[..., o["conv"]:o["qkv"]], w[..., o["gates"]:],
        w[..., o["qkv"]:o["rwkv"]], w[..., o["lora"]:o["gates"]]], axis=-1)


def kernel(x_prompt, x_sample, cache_k, cache_v, state_rwkv_fwd, state_rwkv_bwd, c, c_ctx, w_ada, b_ada, g_norm1, g_norm2, w_in, conv_w, attn_sink, rwkv_w0, rwkv_w2, rwkv_a0, rwkv_a2, rwkv_g2, rwkv_k_k, rwkv_k_a, rwkv_r_k, rwkv_ln_g, rwkv_ln_b, w_branch_conv, w_branch_attn, w_branch_rwkv, w_out, router_w, router_b, moe_w1, moe_b1, moe_w2, moe_b2, g_final):
    bc, lc, d = x_prompt.shape
    bl, tl, _ = x_sample.shape
    depth = w_in.shape[0]
    n_ctx = bc * lc
    t_all = n_ctx + bl * tl
    assert lc % TM_MIX == 0 and tl % TM_IN == 0 and n_ctx % TM_IN == 0 and t_all % TM_ROUTE == 0

    def row_of_tile(i, tm):
        return jnp.where(i < n_ctx // tm, 0, 1 + (i - n_ctx // tm) // (tl // tm))

    xs = (x_prompt.reshape(n_ctx, d), x_sample.reshape(bl * tl, d))
    n_cond = -(-(1 + bl) // 8) * 8
    cond = jnp.zeros((n_cond, d), F32).at[0].set(c_ctx).at[1:1 + bl].set(c)
    mods = _ada_call(cond, w_ada, b_ada).reshape(depth, n_cond, N_MOD, d)

    w_in_p = _permute_cols(w_in).astype(BF16)
    head_id = np.arange(PAIR) // RWKV_HEAD
    ones_pair = jnp.asarray((head_id[:, None] == head_id[None, :]).astype(np.float32), BF16)
    cos_t, sin_t = _rope_tables(tl)
    zeros_state = jnp.zeros((bc, 2 * N_PAIR, RWKV_HEAD, RWKV_HEAD), F32)
    tiles = (n_ctx // TM_MIX, lc // TM_MIX, tl // TM_MIX)

    new_k, new_v, new_sf, new_sb = [], [], [], []
    for l in range(depth):
        mod = mods[l]
        lw = dict(
            w0=rwkv_w0[l], w2=rwkv_w2[l], a0=rwkv_a0[l], a2=rwkv_a2[l], g2=rwkv_g2[l],
            kk_w=rwkv_k_k[l].reshape(1, -1), ka_w=rwkv_k_a[l].reshape(1, -1), rk_w=rwkv_r_k[l].reshape(1, -1),
            ln_g=rwkv_ln_g[l].reshape(1, -1), ln_b=rwkv_ln_b[l].reshape(1, -1), conv_w=conv_w[l],
            ones_pair=ones_pair,
            wa=w_branch_conv[l].astype(BF16), wb=w_branch_attn[l].astype(BF16), wc=w_branch_rwkv[l].astype(BF16),
            wo=w_out[l].astype(BF16),
        )
        z = _in_call(xs, w_in_p, l, g_norm1[l], mod, row_of_tile)

        kv_ctx = z[:n_ctx, COL_QKV + ATT_W:COL_QKV + ATT_W + 2 * KV_W].astype(F32)
        new_k.append(kv_ctx[:, :KV_W].reshape(bc, lc, N_KV_HEADS, HEAD_DIM))
        new_v.append(kv_ctx[:, KV_W:].reshape(bc, lc, N_KV_HEADS, HEAD_DIM))

        ya_c = _attn_ctx_call(z, attn_sink[l], bc, lc)
        ya_l = _attn_lat_call(z, attn_sink[l], cache_k[:, l].reshape(bl, -1, KV_W), cache_v[:, l].reshape(bl, -1, KV_W),
                              cos_t, sin_t, bl, tl, n_ctx)

        of_c, ob_c, zf_c, zb_c = _rwkv_call(z, lw, _state_to_z(zeros_state), _state_to_z(zeros_state), bc, lc, 0)
        of_l, ob_l, _, _ = _rwkv_call(z, lw, _state_to_z(state_rwkv_fwd[:, l].astype(F32)),
                                      _state_to_z(state_rwkv_bwd[:, l].astype(F32)), bl, tl, n_ctx)
        new_sf.append(_z_to_state(zf_c))
        new_sb.append(_z_to_state(zb_c))

        x_mid = _mix_call(xs, z, (of_c, of_l), (ob_c, ob_l), (ya_c, ya_l), mod, row_of_tile, lw, tiles)
        h2, e_t, gate_t, rank_t, cnt = _route_call(x_mid, mod, row_of_tile, g_norm2[l], router_w[l], router_b[l])
        xs = (_moe_layer(x_mid, mod, row_of_tile, h2, e_t, gate_t, rank_t, cnt[:, 0], l, moe_w1, moe_b1, moe_w2, moe_b2),)

    y = _final_call(xs[0], g_final)
    y_prompt = y[:n_ctx].reshape(bc, lc, d)
    y_sample = y[n_ctx:].reshape(bl, tl, d)
    dt = x_prompt.dtype
    return (y_prompt, y_sample, jnp.stack(new_k, axis=1), jnp.stack(new_v, axis=1),
            jnp.stack(new_sf, axis=1).astype(dt), jnp.stack(new_sb, axis=1).astype(dt))
```

```python
import functools

import numpy as np
import jax
import jax.numpy as jnp
from jax import lax
from jax.experimental import pallas as pl
from jax.experimental.pallas import tpu as pltpu

F32 = jnp.float32
BF16 = jnp.bfloat16

D_MODEL = 1024
N_MOD = 6
RMS_EPS = 1e-6
CONV_W = 512
N_HEADS = 8
N_KV_HEADS = 2
GQA_GROUP = N_HEADS // N_KV_HEADS
HEAD_DIM = 64
ATT_W = N_HEADS * HEAD_DIM
KV_W = N_KV_HEADS * HEAD_DIM
WINDOW = 128
Q_BLOCK = 128
ATTN_SCALE = HEAD_DIM ** -0.5
ROPE_THETA = 10000.0
GRID_W = 64
RWKV_HEAD = 64
RWKV_W = 512
DECAY_LORA = 64
ICLR_LORA = 64
GATE_LORA = 128
GN_EPS = 64e-5
N_EXPERTS = 32
TOP_K = 4
D_EXPERT = 1024
SWIGLU_LIMIT = 7.0
SWIGLU_ALPHA = 1.702
P_TOTAL = 7296

_SRC_OFF = dict(conv=0, qkv=1536, rwkv=2304, lora=3840, gates=4224)
COL_RWKV, COL_CONV, COL_GATES, COL_QKV, COL_LORA = 0, 1536, 3072, 6144, 6912
LORA_W = 2 * DECAY_LORA + 2 * ICLR_LORA + GATE_LORA

CHUNK = 64
PAIR = 2 * RWKV_HEAD
N_PAIR = RWKV_W // PAIR
NEG_BIG = -1e30
HALO = 16
EXP_NEG_HALF = float(np.exp(-0.5))

TM_IN = 512
TN_IN = 2432
TM_MIX = 256
TM_ROUTE = 512
TM_MOE = 512
TM_DISPATCH = 512
TM_COMBINE = 256
ISSUE_UNROLL = 8
VMEM_LIMIT = 56 * 1024 * 1024


def _cparams(sem, vmem=VMEM_LIMIT):
    return pltpu.CompilerParams(dimension_semantics=sem, vmem_limit_bytes=vmem)


def _dot(a, b, dims=(((1,), (0,)), ((), ()))):
    return lax.dot_general(a, b, dims, preferred_element_type=F32)


_NT = (((1,), (1,)), ((), ()))
_TN = (((0,), (0,)), ((), ()))


def _split(x):
    hi = x.astype(BF16)
    lo = (x - hi.astype(F32)).astype(BF16)
    return hi, lo


def _dot3(a, b, dims=(((1,), (0,)), ((), ()))):
    ah, al = _split(a)
    bh, bl = _split(b)
    return _dot(ah, bh, dims) + (_dot(ah, bl, dims) + _dot(al, bh, dims))


def _head_sums(x, ones_pair):
    rows = x.shape[0]
    xs = jnp.concatenate([x[:, PAIR * p:PAIR * (p + 1)] for p in range(N_PAIR)], axis=0).astype(BF16)
    s = _dot(xs, ones_pair)
    return jnp.concatenate([s[rows * p:rows * (p + 1)] for p in range(N_PAIR)], axis=1)


def _sigmoid(x):
    return 1.0 / (1.0 + jnp.exp(-x))


def _rms_mod(x, g, scale, shift):
    ms = jnp.mean(x * x, axis=-1, keepdims=True)
    return (x * lax.rsqrt(ms + RMS_EPS) * g) * (1.0 + scale) + shift


def _ada_kernel(cond_ref, w_ref, b_ref, o_ref):
    c = cond_ref[...]
    s = c * _sigmoid(c)
    o_ref[...] = _dot(s.astype(BF16), w_ref[...].astype(BF16)) + b_ref[...]


def _ada_call(cond, w_ada, b_ada):
    depth, d, n = w_ada.shape
    r = cond.shape[0]
    tn = 1536
    return pl.pallas_call(
        _ada_kernel,
        grid=(depth, n // tn),
        in_specs=[
            pl.BlockSpec((r, d), lambda l, j: (0, 0)),
            pl.BlockSpec((None, d, tn), lambda l, j: (l, 0, j)),
            pl.BlockSpec((None, 1, tn), lambda l, j: (l, 0, j)),
        ],
        out_specs=pl.BlockSpec((None, r, tn), lambda l, j: (l, 0, j)),
        out_shape=jax.ShapeDtypeStruct((depth, r, n), F32),
        compiler_params=_cparams(("arbitrary", "arbitrary")),
        name="ada",
    )(cond, w_ada, b_ada.reshape(depth, 1, n))


def _in_kernel(n_first, *refs):
    if n_first is None:
        x_ref, mod_ref, g_ref, w_ref, z_ref = refs
        x = x_ref[...]
    else:
        xa_ref, xb_ref, mod_ref, g_ref, w_ref, z_ref = refs
        x = jnp.where(pl.program_id(1) < n_first, xa_ref[...], xb_ref[...])
    h = _rms_mod(x, g_ref[...], mod_ref[1:2, :], mod_ref[0:1, :]).astype(BF16)
    z_ref[...] = _dot(h, w_ref[...]).astype(BF16)


def _in_call(xs, w_bf16, layer, g_norm, mod, row_of_tile):
    d = xs[0].shape[1]
    t = sum(x.shape[0] for x in xs)
    n = w_bf16.shape[2]
    tm, tn = TM_IN, TN_IN
    if len(xs) == 1:
        n_first = None
        x_specs = [pl.BlockSpec((tm, d), lambda j, i: (i, 0))]
    else:
        n_first = xs[0].shape[0] // tm
        x_specs = [pl.BlockSpec((tm, d), lambda j, i: (jnp.minimum(i, n_first - 1), 0)),
                   pl.BlockSpec((tm, d), lambda j, i: (jnp.maximum(i - n_first, 0), 0))]
    return pl.pallas_call(
        functools.partial(_in_kernel, n_first),
        grid=(n // tn, t // tm),
        in_specs=x_specs + [
            pl.BlockSpec((None, N_MOD, d), lambda j, i: (row_of_tile(i, tm), 0, 0)),
            pl.BlockSpec((1, d), lambda j, i: (0, 0)),
            pl.BlockSpec((None, d, tn), lambda j, i: (layer, 0, j)),
        ],
        out_specs=pl.BlockSpec((tm, tn), lambda j, i: (i, j)),
        out_shape=jax.ShapeDtypeStruct((t, n), BF16),
        compiler_params=_cparams(("arbitrary", "arbitrary")),
        name="in_proj",
    )(*xs, mod, g_norm.reshape(1, d), w_bf16)


def _softmax_pv(scores, values, sink):
    m = sink
    for s in scores:
        m = jnp.maximum(m, jnp.max(s, axis=-1, keepdims=True))
    den = jnp.exp(sink - m)
    acc = None
    for s, v in zip(scores, values):
        p = jnp.exp(s - m)
        den = den + jnp.sum(p, axis=-1, keepdims=True)
        pv = _dot(p.astype(BF16), v)
        acc = pv if acc is None else acc + pv
    return acc / den


def _attn_ctx_kernel(sink_ref, q_ref, kv_ref, o_ref):
    q = (q_ref[...] * ATTN_SCALE).astype(BF16)
    kv = kv_ref[...].astype(BF16)
    outs = []
    for h in range(N_HEADS):
        g = h // GQA_GROUP
        qh = q[:, HEAD_DIM * h:HEAD_DIM * (h + 1)]
        kh = kv[:, HEAD_DIM * g:HEAD_DIM * (g + 1)]
        vh = kv[:, KV_W + HEAD_DIM * g:KV_W + HEAD_DIM * (g + 1)]
        outs.append(_softmax_pv([_dot(qh, kh, _NT)], [vh], sink_ref[h]))
    o_ref[...] = jnp.concatenate(outs, axis=-1).astype(BF16)


def _attn_ctx_call(z, sink, n_seq, seq_len):
    return pl.pallas_call(
        _attn_ctx_kernel,
        grid=(n_seq,),
        in_specs=[
            pl.BlockSpec(memory_space=pltpu.SMEM),
            pl.BlockSpec((seq_len, ATT_W), lambda s: (s, COL_QKV // ATT_W)),
            pl.BlockSpec((seq_len, 2 * KV_W), lambda s: (s, (COL_QKV + ATT_W) // (2 * KV_W))),
        ],
        out_specs=pl.BlockSpec((seq_len, ATT_W), lambda s: (s, 0)),
        out_shape=jax.ShapeDtypeStruct((n_seq * seq_len, ATT_W), BF16),
        compiler_params=_cparams(("arbitrary",)),
        name="attn_ctx",
    )(sink, z, z)


def _rope(x, cos, sin_signed):
    n = x.shape[-1]
    lane = lax.broadcasted_iota(jnp.int32, x.shape, 1)
    up = pltpu.roll(x, n - 16, 1)
    dn = pltpu.roll(x, 16, 1)
    partner = jnp.where((lane & 31) < 16, up, dn)
    return x * cos + partner * sin_signed


def _attn_lat_kernel(sink_ref, q_ref, kvp_ref, kvc_ref, kvn_ref, ck_ref, cv_ref,
                     cosq_ref, sinq_ref, cosp_ref, sinp_ref, cosn_ref, sinn_ref, o_ref):
    qb = pl.program_id(1)
    nb = pl.num_programs(1)
    cq, sq = cosq_ref[...], sinq_ref[...]
    q = _rope(q_ref[...].astype(F32), jnp.concatenate([cq] * 4, axis=1), jnp.concatenate([sq] * 4, axis=1))
    q = (q * ATTN_SCALE).astype(BF16)
    kvp, kvc, kvn = (ref[...].astype(F32) for ref in (kvp_ref, kvc_ref, kvn_ref))
    kp = _rope(kvp[:, :KV_W], cosp_ref[...], sinp_ref[...]).astype(BF16)
    kc = _rope(kvc[:, :KV_W], cq, sq).astype(BF16)
    kn = _rope(kvn[:, :KV_W], cosn_ref[...], sinn_ref[...]).astype(BF16)
    vp, vc, vn = (t[:, KV_W:].astype(BF16) for t in (kvp, kvc, kvn))
    ck = ck_ref[...].astype(BF16)
    cv = cv_ref[...].astype(BF16)
    k_cat = jnp.concatenate([kp, kc, kn, ck], axis=0)
    v_cat = jnp.concatenate([vp, vc, vn, cv], axis=0)
    nk = k_cat.shape[0]
    qi = lax.broadcasted_iota(jnp.int32, (Q_BLOCK, nk), 0)
    kj = lax.broadcasted_iota(jnp.int32, (Q_BLOCK, nk), 1)
    ok_p = (kj >= qi) & (qb > 0)
    ok_n = (kj - 2 * Q_BLOCK <= qi) & (qb < nb - 1)
    in_next = (kj >= 2 * Q_BLOCK) & (kj < 3 * Q_BLOCK)
    bias = jnp.where(kj < Q_BLOCK, jnp.where(ok_p, 0.0, NEG_BIG), jnp.where(in_next, jnp.where(ok_n, 0.0, NEG_BIG), 0.0))
    bias = jnp.concatenate([bias] * GQA_GROUP, axis=0)
    groups = range(N_KV_HEADS)
    heads = [range(GQA_GROUP * g, GQA_GROUP * (g + 1)) for g in groups]
    gsl = [slice(HEAD_DIM * g, HEAD_DIM * (g + 1)) for g in groups]
    q_g = [jnp.concatenate([q[:, HEAD_DIM * h:HEAD_DIM * (h + 1)] for h in heads[g]], axis=0) for g in groups]
    sink = [jnp.concatenate([jnp.full((Q_BLOCK, 1), sink_ref[h], F32) for h in heads[g]], axis=0) for g in groups]
    s = [_dot(q_g[g], k_cat[:, gsl[g]], _NT) + bias for g in groups]
    m = [jnp.maximum(jnp.max(s[g], axis=-1, keepdims=True), sink[g]) for g in groups]
    p = [jnp.exp(s[g] - m[g]) for g in groups]
    den = [jnp.sum(p[g], axis=-1, keepdims=True) + jnp.exp(sink[g] - m[g]) for g in groups]
    o = [_dot(p[g].astype(BF16), v_cat[:, gsl[g]]) / den[g] for g in groups]
    outs = [o[g][Q_BLOCK * i:Q_BLOCK * (i + 1)] for g in groups for i in range(GQA_GROUP)]
    o_ref[...] = jnp.concatenate(outs, axis=-1).astype(BF16)


def _attn_lat_call(z, sink, cache_k, cache_v, cos_t, sin_t, n_seq, seq_len, row0):
    nb = seq_len // Q_BLOCK
    base = row0 // Q_BLOCK
    past = cache_k.shape[1]

    def rows(off):
        return lambda b, i: (base + b * nb + jnp.clip(i + off, 0, nb - 1))

    def kv_spec(off):
        r = rows(off)
        return pl.BlockSpec((Q_BLOCK, 2 * KV_W), lambda b, i: (r(b, i), (COL_QKV + ATT_W) // (2 * KV_W)))

    def tab_spec(off):
        return pl.BlockSpec((Q_BLOCK, KV_W), lambda b, i: (jnp.clip(i + off, 0, nb - 1), 0))

    r0 = rows(0)
    return pl.pallas_call(
        _attn_lat_kernel,
        grid=(n_seq, nb),
        in_specs=[
            pl.BlockSpec(memory_space=pltpu.SMEM),
            pl.BlockSpec((Q_BLOCK, ATT_W), lambda b, i: (r0(b, i), COL_QKV // ATT_W)),
            kv_spec(-1), kv_spec(0), kv_spec(1),
            pl.BlockSpec((None, past, KV_W), lambda b, i: (b, 0, 0)),
            pl.BlockSpec((None, past, KV_W), lambda b, i: (b, 0, 0)),
            tab_spec(0), tab_spec(0), tab_spec(-1), tab_spec(-1), tab_spec(1), tab_spec(1),
        ],
        out_specs=pl.BlockSpec((Q_BLOCK, ATT_W), lambda b, i: (b * nb + i, 0)),
        out_shape=jax.ShapeDtypeStruct((n_seq * seq_len, ATT_W), BF16),
        compiler_params=_cparams(("arbitrary", "arbitrary")),
        name="attn_lat",
    )(sink, z, z, z, z, cache_k, cache_v, cos_t, sin_t, cos_t, sin_t, cos_t, sin_t)


def _rope_tables(seq_len):
    half = HEAD_DIM // 2
    pos = np.arange(seq_len)
    inv_freq = 1.0 / (ROPE_THETA ** (np.arange(0, half, 2, dtype=np.float32) / half))
    inv_freq = inv_freq.astype(np.float32)

    def part(p):
        ang = (p.astype(np.float32)[:, None] * inv_freq[None, :]).astype(np.float32)
        c, s = np.cos(ang), np.sin(ang)
        return np.concatenate([c, c], axis=1), np.concatenate([-s, s], axis=1)

    c_r, s_r = part(pos // GRID_W)
    c_c, s_c = part(pos % GRID_W)
    cos = np.concatenate([c_r, c_c] * N_KV_HEADS, axis=1).astype(np.float32)
    sin = np.concatenate([s_r, s_c] * N_KV_HEADS, axis=1).astype(np.float32)
    return jnp.asarray(cos), jnp.asarray(sin)


def _rwkv_dir_inputs(zm, zl, d, w0, w2, a0, a2, kk_w, ka_w, ones_pair):
    r = zm[:, :RWKV_W]
    kraw = zm[:, RWKV_W:2 * RWKV_W]
    v = zm[:, 2 * RWKV_W:]
    wl = zl[:, DECAY_LORA * d:DECAY_LORA * (d + 1)]
    al = zl[:, 2 * DECAY_LORA + ICLR_LORA * d:2 * DECAY_LORA + ICLR_LORA * (d + 1)]
    xw = w0 + _dot(jnp.tanh(wl).astype(BF16), w2.astype(BF16))
    ld = -EXP_NEG_HALF * _sigmoid(xw)
    a = _sigmoid(a0 + _dot(al.astype(BF16), a2.astype(BF16)))
    k = kraw * (1.0 + (a - 1.0) * ka_w)
    kkr = kraw * kk_w
    n2 = _head_sums(kkr * kkr, ones_pair)
    kk = kkr / jnp.maximum(jnp.sqrt(n2), 1e-12)
    return r, v, kk, ld, a, k


def _tri_masks(rev):
    c = CHUNK
    ti = lax.broadcasted_iota(jnp.int32, (c, c), 0)
    si = lax.broadcasted_iota(jnp.int32, (c, c), 1)
    incl = (si >= ti) if rev else (si <= ti)
    strict = (si > ti) if rev else (si < ti)
    return incl, strict, (si == ti).astype(F32)


def _chunk_prepare(r, v, kk, ld, a, k, rev):
    incl, _, _ = _tri_masks(rev)
    m_incl = jnp.where(incl, 1.0, 0.0).astype(BF16)
    ldh, ldl = _split(ld)
    cin = _dot(m_incl, ldh) + _dot(m_incl, ldl)
    tot = jnp.sum(ld, axis=0, keepdims=True)
    e_neg = jnp.exp(-cin)
    e_end = jnp.exp(tot - cin)
    bb = kk * a
    return dict(
        a_m=kk * jnp.exp(cin - ld), r_m=r * jnp.exp(cin),
        b_m=(bb * e_neg).astype(BF16), k_m=(k * e_neg).astype(BF16),
        b_end=(bb * e_end).astype(BF16), k_end=(k * e_end).astype(BF16),
        v=v, e_tot=jnp.exp(tot), rev=rev)


def _chunk_problems(probs):
    c = CHUNK
    n = len(probs)
    rng = range(n)
    lane = lax.broadcasted_iota(jnp.int32, (1, PAIR), 1)
    first_head = lane < RWKV_HEAD

    def bd(x):
        xb = x.astype(BF16)
        zero = jnp.zeros_like(xb)
        return jnp.concatenate([jnp.where(first_head, xb, zero), jnp.where(first_head, zero, xb)], axis=0)

    ti = lax.broadcasted_iota(jnp.int32, (c, PAIR), 0)
    si = lax.broadcasted_iota(jnp.int32, (c, PAIR), 1) & (RWKV_HEAD - 1)
    incl = {False: si <= ti, True: si >= ti}
    strict = {False: si < ti, True: si > ti}
    eye = (si == ti).astype(F32)
    rev = [p["rev"] for p in probs]

    lhs = [jnp.concatenate([probs[i]["a_m"], probs[i]["r_m"]], axis=0).astype(BF16) for i in rng]
    xb = [_dot(lhs[i], bd(probs[i]["b_m"]), _NT) for i in rng]
    xk = [_dot(lhs[i], bd(probs[i]["k_m"]), _NT) for i in rng]
    m_ab = [jnp.where(strict[rev[i]], xb[i][:c], 0.0) for i in rng]
    m_ak = [jnp.where(strict[rev[i]], xk[i][:c], 0.0).astype(BF16) for i in rng]
    m_rb = [jnp.where(incl[rev[i]], xb[i][c:], 0.0).astype(BF16) for i in rng]
    m_rk = [jnp.where(incl[rev[i]], xk[i][c:], 0.0).astype(BF16) for i in rng]
    v_bd = [bd(probs[i]["v"]) for i in rng]
    mv = [_dot(jnp.concatenate([m_ak[i], m_rk[i]], axis=0), v_bd[i]) for i in rng]
    mak_v = [mv[i][:c] for i in rng]
    mrk_v = [mv[i][c:] for i in rng]
    t_inv = [eye - m_ab[i] for i in rng]
    lp = [_dot(m_ab[i].astype(BF16), bd(m_ab[i])) for i in rng]
    for _ in range(4):
        both = [_dot(jnp.concatenate([lp[i], t_inv[i]], axis=0).astype(BF16), bd(lp[i])) for i in rng]
        t_inv = [t_inv[i] + both[i][c:] for i in rng]
        lp = [both[i][:c] for i in rng]
    t_inv = [t_inv[i] + _dot(t_inv[i].astype(BF16), bd(lp[i])) for i in rng]
    wu = [_dot(t_inv[i].astype(BF16), jnp.concatenate([bd(probs[i]["a_m"]), bd(mak_v[i])], axis=1)) for i in rng]
    ro = [_dot(m_rb[i], jnp.concatenate([bd(wu[i][:, :PAIR]), bd(wu[i][:, PAIR:])], axis=1)) for i in rng]
    ri = lax.broadcasted_iota(jnp.int32, (PAIR, PAIR), 0)
    ci = lax.broadcasted_iota(jnp.int32, (PAIR, PAIR), 1)
    same_head = (ri < RWKV_HEAD) == (ci < RWKV_HEAD)
    bt_wu = [_dot(probs[i]["b_end"], wu[i].astype(BF16), _TN) for i in rng]
    kt_v = [_dot(probs[i]["k_end"], probs[i]["v"].astype(BF16), _TN) for i in rng]
    results = []
    for i in rng:
        r_eff = probs[i]["r_m"] - ro[i][:, :PAIR]
        o_loc = mrk_v[i] - ro[i][:, PAIR:]
        g_t = jnp.where(same_head, jnp.where(ri == ci, probs[i]["e_tot"], 0.0) - bt_wu[i][:, :PAIR], 0.0)
        h_t = jnp.where(same_head, kt_v[i] - bt_wu[i][:, PAIR:], 0.0)
        zh, zl = _split(probs[i]["z"])
        both = jnp.concatenate([r_eff, g_t], axis=0)
        bh, bl = _split(both)
        prod = _dot(bh, zh) + (_dot(bh, zl) + _dot(bl, zh))
        results.append((prod[:c] + o_loc, prod[c:] + h_t))
    return results


def _rwkv_kernel(zmf_ref, zlf_ref, zmb_ref, zlb_ref, w0_ref, w2_ref, a0_ref, a2_ref, kkw_ref, kaw_ref,
                 ones_ref, s0f_ref, s0b_ref, of_ref, ob_ref, zf_ref, zb_ref, zf_scr, zb_scr):
    @pl.when(pl.program_id(1) == 0)
    def _():
        zf_scr[...] = s0f_ref[...]
        zb_scr[...] = s0b_ref[...]

    ones_pair = ones_ref[...]
    dirs = ((zmf_ref, zlf_ref, zf_scr, of_ref), (zmb_ref, zlb_ref, zb_scr, ob_ref))
    probs = []
    for d, (zm_ref, zl_ref, z_scr, _) in enumerate(dirs):
        r, v, kk, ld, a, k = _rwkv_dir_inputs(zm_ref[...].astype(F32), zl_ref[...].astype(F32), d, w0_ref[d:d + 1, :], w2_ref[d],
                                              a0_ref[d:d + 1, :], a2_ref[d], kkw_ref[...], kaw_ref[...], ones_pair)
        full = _chunk_prepare(r, v, kk, ld, a, k, d == 1)
        for p in range(N_PAIR):
            ps = slice(PAIR * p, PAIR * (p + 1))
            prob = {key: (val[:, ps] if hasattr(val, "shape") else val) for key, val in full.items()}
            prob["z"] = z_scr[p]
            probs.append(prob)
    results = _chunk_problems(probs)
    for d, (_, _, z_scr, o_ref) in enumerate(dirs):
        for p in range(N_PAIR):
            out, z_new = results[d * N_PAIR + p]
            o_ref[:, PAIR * p:PAIR * (p + 1)] = out
            z_scr[p] = z_new
    zf_ref[...] = zf_scr[...]
    zb_ref[...] = zb_scr[...]


def _rwkv_call(z, lw, s0f, s0b, n_seq, seq_len, row0):
    nc = seq_len // CHUNK
    base = row0 // CHUNK

    def fwd(s, c):
        return base + s * nc + c

    def bwd(s, c):
        return base + s * nc + (nc - 1 - c)

    def zm_spec(f):
        return pl.BlockSpec((CHUNK, 3 * RWKV_W), lambda s, c: (f(s, c), COL_RWKV // (3 * RWKV_W)))

    def zl_spec(f):
        return pl.BlockSpec((CHUNK, LORA_W), lambda s, c: (f(s, c), COL_LORA // LORA_W))

    def full(shape):
        return pl.BlockSpec(shape, lambda s, c: (0,) * len(shape))

    st_spec = pl.BlockSpec((None, N_PAIR, PAIR, PAIR), lambda s, c: (s, 0, 0, 0))
    o_shape = jax.ShapeDtypeStruct((n_seq * seq_len, RWKV_W), F32)
    st_shape = jax.ShapeDtypeStruct((n_seq, N_PAIR, PAIR, PAIR), F32)
    return pl.pallas_call(
        _rwkv_kernel,
        grid=(n_seq, nc),
        in_specs=[
            zm_spec(fwd), zl_spec(fwd), zm_spec(bwd), zl_spec(bwd),
            full((2, RWKV_W)), full((2, DECAY_LORA, RWKV_W)), full((2, RWKV_W)), full((2, ICLR_LORA, RWKV_W)),
            full((1, RWKV_W)), full((1, RWKV_W)), full((PAIR, PAIR)),
            st_spec, st_spec,
        ],
        out_specs=[
            pl.BlockSpec((CHUNK, RWKV_W), lambda s, c: (s * nc + c, 0)),
            pl.BlockSpec((CHUNK, RWKV_W), lambda s, c: (s * nc + (nc - 1 - c), 0)),
            st_spec, st_spec,
        ],
        out_shape=[o_shape, o_shape, st_shape, st_shape],
        scratch_shapes=[pltpu.VMEM((N_PAIR, PAIR, PAIR), F32), pltpu.VMEM((N_PAIR, PAIR, PAIR), F32)],
        compiler_params=_cparams(("arbitrary", "arbitrary")),
        name="rwkv_scan",
    )(z, z, z, z, lw["w0"], lw["w2"], lw["a0"], lw["a2"], lw["kk_w"], lw["ka_w"], lw["ones_pair"], s0f, s0b)


def _state_to_z(s):
    n = s.shape[0]
    st = jnp.swapaxes(s, -1, -2).reshape(n, N_PAIR, 2, RWKV_HEAD, RWKV_HEAD)
    zero = jnp.zeros_like(st[:, :, 0])
    top = jnp.concatenate([st[:, :, 0], zero], axis=-1)
    bot = jnp.concatenate([zero, st[:, :, 1]], axis=-1)
    return jnp.concatenate([top, bot], axis=-2)


def _z_to_state(z):
    n = z.shape[0]
    h0 = z[:, :, :RWKV_HEAD, :RWKV_HEAD]
    h1 = z[:, :, RWKV_HEAD:, RWKV_HEAD:]
    st = jnp.stack([h0, h1], axis=2).reshape(n, 2 * N_PAIR, RWKV_HEAD, RWKV_HEAD)
    return jnp.swapaxes(st, -1, -2)


def _mix_kernel(tiles, n_x, *refs):
    n_ctx_tiles, per_ctx, per_lat = tiles
    i = pl.program_id(0)
    in_ctx = i < n_ctx_tiles

    def pick(pair):
        return jnp.where(in_ctx, pair[0][...], pair[1][...])

    x_refs, refs = refs[:n_x], refs[n_x:]
    (zc_ref, zcp_ref, zcn_ref, zm_ref, zl_ref, zg_ref, ofc_ref, ofl_ref, obc_ref, obl_ref, yac_ref, yal_ref, mod_ref,
     convw_ref, a0_ref, a2_ref, g2_ref, kaw_ref, rkw_ref, lng_ref, lnb_ref, ones_ref,
     wa_ref, wb_ref, wc_ref, wo_ref, o_ref) = refs
    x_in = x_refs[0][...] if n_x == 1 else pick(x_refs)
    pos = jnp.where(i < n_ctx_tiles, i % per_ctx, (i - n_ctx_tiles) % per_lat)
    per = jnp.where(i < n_ctx_tiles, per_ctx, per_lat)
    tm = o_ref.shape[0]

    zc = zc_ref[...].astype(F32)
    u = zc[:, 2 * CONV_W:] * zc[:, :CONV_W]
    zp = zcp_ref[...].astype(F32)
    zn = zcn_ref[...].astype(F32)
    u_prev_row = jnp.where(pos > 0, zp[HALO - 1:HALO, 2 * CONV_W:] * zp[HALO - 1:HALO, :CONV_W], 0.0)
    u_next_row = jnp.where(pos < per - 1, zn[0:1, 2 * CONV_W:] * zn[0:1, :CONV_W], 0.0)
    row = lax.broadcasted_iota(jnp.int32, (tm, CONV_W), 0)
    u_prev = jnp.where(row == 0, u_prev_row, pltpu.roll(u, 1, 0))
    u_next = jnp.where(row == tm - 1, u_next_row, pltpu.roll(u, tm - 1, 0))
    cw = convw_ref[...]
    y_conv = zc[:, CONV_W:2 * CONV_W] * (cw[0:1, :] * u_prev + cw[1:2, :] * u + cw[2:3, :] * u_next)

    zm = zm_ref[...].astype(F32)
    zl = zl_ref[...].astype(F32)
    r = zm[:, :RWKV_W]
    kraw = zm[:, RWKV_W:2 * RWKV_W]
    v = zm[:, 2 * RWKV_W:]
    ones_pair = ones_ref[...]
    o = pick((ofc_ref, ofl_ref)) + pick((obc_ref, obl_ref))
    mu = _head_sums(o, ones_pair) * (1.0 / RWKV_HEAD)
    dlt = o - mu
    var = _head_sums(dlt * dlt, ones_pair) * (1.0 / RWKV_HEAD)
    y = dlt * lax.rsqrt(var + GN_EPS) * lng_ref[...] + lnb_ref[...]
    for d in range(2):
        al = zl[:, 2 * DECAY_LORA + ICLR_LORA * d:2 * DECAY_LORA + ICLR_LORA * (d + 1)]
        a = _sigmoid(a0_ref[d:d + 1, :] + _dot(al.astype(BF16), a2_ref[d].astype(BF16)))
        k = kraw * (1.0 + (a - 1.0) * kaw_ref[...])
        y = y + _head_sums(r * k * rkw_ref[...], ones_pair) * v
    g1 = zl[:, 2 * DECAY_LORA + 2 * ICLR_LORA:]
    y_rwkv = y * _dot(_sigmoid(g1).astype(BF16), g2_ref[...].astype(BF16))

    zg = zg_ref[...].astype(F32)
    merged = (_sigmoid(zg[:, :D_MODEL]) * _dot(y_conv.astype(BF16), wa_ref[...])
              + _sigmoid(zg[:, D_MODEL:2 * D_MODEL]) * _dot(pick((yac_ref, yal_ref)), wb_ref[...])
              + _sigmoid(zg[:, 2 * D_MODEL:]) * _dot(y_rwkv.astype(BF16), wc_ref[...]))
    o_ref[...] = x_in + mod_ref[2:3, :] * _dot(merged.astype(BF16), wo_ref[...])


def _mix_call(xs, z, o_f, o_b, y_attn, mod, row_of_tile, lw, tiles):
    t, d = z.shape[0], xs[0].shape[1]
    tm = TM_MIX
    nt = t // tm
    hb = tm // HALO
    n_ctx_tiles = tiles[0]

    def rows(w, col):
        return pl.BlockSpec((tm, w), lambda i: (i, col // w))

    def pair(w):
        return [pl.BlockSpec((tm, w), lambda i: (jnp.minimum(i, n_ctx_tiles - 1), 0)),
                pl.BlockSpec((tm, w), lambda i: (jnp.maximum(i - n_ctx_tiles, 0), 0))]

    def full(shape):
        return pl.BlockSpec(shape, lambda i: (0,) * len(shape))

    in_specs = (pair(d) if len(xs) == 2 else [rows(d, 0)]) + [
        rows(3 * CONV_W, COL_CONV),
        pl.BlockSpec((HALO, 3 * CONV_W), lambda i: (jnp.maximum(i * hb - 1, 0), COL_CONV // (3 * CONV_W))),
        pl.BlockSpec((HALO, 3 * CONV_W), lambda i: (jnp.minimum((i + 1) * hb, nt * hb - 1), COL_CONV // (3 * CONV_W))),
        rows(3 * RWKV_W, COL_RWKV),
        rows(LORA_W, COL_LORA),
        rows(3 * D_MODEL, COL_GATES),
    ] + pair(RWKV_W) + pair(RWKV_W) + pair(ATT_W) + [
        pl.BlockSpec((None, N_MOD, d), lambda i: (row_of_tile(i, tm), 0, 0)),
        full((3, CONV_W)), full((2, RWKV_W)), full((2, ICLR_LORA, RWKV_W)), full((GATE_LORA, RWKV_W)),
        full((1, RWKV_W)), full((1, RWKV_W)), full((1, RWKV_W)), full((1, RWKV_W)),
        full((PAIR, PAIR)),
        full((CONV_W, d)), full((ATT_W, d)), full((RWKV_W, d)), full((d, d)),
    ]
    return pl.pallas_call(
        functools.partial(_mix_kernel, tiles, len(xs)),
        grid=(nt,),
        in_specs=in_specs,
        out_specs=rows(d, 0),
        out_shape=jax.ShapeDtypeStruct((t, d), F32),
        compiler_params=_cparams(("arbitrary",)),
        name="mix",
    )(*xs, z, z, z, z, z, z, *o_f, *o_b, *y_attn, mod,
      lw["conv_w"], lw["a0"], lw["a2"], lw["g2"], lw["ka_w"], lw["rk_w"], lw["ln_g"], lw["ln_b"],
      lw["ones_pair"], lw["wa"], lw["wb"], lw["wc"], lw["wo"])


def _route_kernel(x_ref, mod_ref, g_ref, rwt_ref, rb_ref, tri_ref, h_ref, e_ref, gate_ref, rank_ref, cnt_ref, cnt_scr):
    @pl.when(pl.program_id(0) == 0)
    def _():
        cnt_scr[...] = jnp.zeros_like(cnt_scr)

    h2 = _rms_mod(x_ref[...], g_ref[...], mod_ref[4:5, :], mod_ref[3:4, :])
    h_ref[...] = h2.reshape(h_ref.shape)
    logits = _dot3(rwt_ref[...], h2, _NT) + rb_ref[...]
    ne, tm = logits.shape
    ex = lax.broadcasted_iota(jnp.int32, (ne, tm), 0)
    work = logits
    vals, hots = [], []
    for kq in range(TOP_K):
        m = jnp.max(work, axis=0, keepdims=True)
        idx = jnp.min(jnp.where(work == m, ex, ne), axis=0, keepdims=True)
        hot = ex == idx
        vals.append(m)
        hots.append(hot)
        e_ref[kq:kq + 1, :] = idx
        work = jnp.where(hot, -jnp.inf, work)
    exps = [jnp.exp(vk - vals[0]) for vk in vals]
    den = exps[0] + exps[1] + exps[2] + exps[3]
    chosen = jnp.where(hots[0] | hots[1] | hots[2] | hots[3], 1.0, 0.0)
    before = cnt_scr[:, 0:1] + _dot(chosen.astype(BF16), tri_ref[...])
    for kq in range(TOP_K):
        gate_ref[kq:kq + 1, :] = exps[kq] / den
        rank_ref[kq:kq + 1, :] = jnp.sum(jnp.where(hots[kq], before, 0.0), axis=0, keepdims=True).astype(jnp.int32)
    cnt_scr[...] = cnt_scr[...] + jnp.sum(chosen, axis=1, keepdims=True)
    cnt_ref[...] = cnt_scr[...].astype(jnp.int32)


def _route_call(x, mod, row_of_tile, g_norm, router_w, router_b):
    t, d = x.shape
    tm = TM_ROUTE
    ne = router_w.shape[1]
    tri = jnp.triu(jnp.ones((tm, tm), F32), 1).astype(BF16)

    def full(shape):
        return pl.BlockSpec(shape, lambda i: (0,) * len(shape))

    kt_spec = pl.BlockSpec((TOP_K, tm), lambda i: (0, i))
    return pl.pallas_call(
        _route_kernel,
        grid=(t // tm,),
        in_specs=[
            pl.BlockSpec((tm, d), lambda i: (i, 0)),
            pl.BlockSpec((None, N_MOD, d), lambda i: (row_of_tile(i, tm), 0, 0)),
            full((1, d)), full((ne, d)), full((ne, 1)), full((tm, tm)),
        ],
        out_specs=[pl.BlockSpec((tm, 1, d), lambda i: (i, 0, 0)), kt_spec, kt_spec, kt_spec, full((ne, 128))],
        out_shape=[
            jax.ShapeDtypeStruct((t, 1, d), F32),
            jax.ShapeDtypeStruct((TOP_K, t), jnp.int32),
            jax.ShapeDtypeStruct((TOP_K, t), F32),
            jax.ShapeDtypeStruct((TOP_K, t), jnp.int32),
            jax.ShapeDtypeStruct((ne, 128), jnp.int32),
        ],
        scratch_shapes=[pltpu.VMEM((ne, 128), F32)],
        compiler_params=_cparams(("arbitrary",)),
        name="route",
    )(x, mod, g_norm.reshape(1, d), router_w.T, router_b.reshape(ne, 1), tri)


def _row_copy_wait(buf_hbm, n_rows, sem):
    view = buf_hbm.at[pl.ds(0, n_rows)]
    pltpu.make_async_copy(view, view, sem).wait()


def _dispatch_kernel(t_all, dest_ref, h_ref, rows_hbm, sem):
    tm = TM_DISPATCH
    t0 = pl.program_id(0) * tm

    def body(t, carry):
        for kq in range(TOP_K):
            pltpu.make_async_copy(h_ref.at[t], rows_hbm.at[dest_ref[kq * t_all + t0 + t]], sem).start(priority=kq % 2)
        return carry

    lax.fori_loop(0, tm, body, 0, unroll=ISSUE_UNROLL)
    _row_copy_wait(rows_hbm, TOP_K * tm, sem)


def _dispatch_call(h3, dest_flat):
    t, _, d = h3.shape
    grid_spec = pltpu.PrefetchScalarGridSpec(
        num_scalar_prefetch=1,
        grid=(t // TM_DISPATCH,),
        in_specs=[pl.BlockSpec((TM_DISPATCH, 1, d), lambda i, dr: (i, 0, 0))],
        out_specs=pl.BlockSpec(memory_space=pl.ANY),
        scratch_shapes=[pltpu.SemaphoreType.DMA(())],
    )
    return pl.pallas_call(
        functools.partial(_dispatch_kernel, t),
        grid_spec=grid_spec,
        out_shape=jax.ShapeDtypeStruct((TOP_K * t, 1, d), F32),
        compiler_params=_cparams(("arbitrary",)),
        name="moe_dispatch",
    )(dest_flat, h3)


def _moe_kernel(blk_ref, exp_ref, lo_ref, hi_ref, fblk_ref, fexp_ref, nit_ref,
                x_ref, w1_ref, b1_ref, w2_ref, b2_ref, o_ref, w1_scr, w2_scr, x_scr, acc_scr):
    del blk_ref, exp_ref
    it = pl.program_id(0)

    @pl.when(fexp_ref[it] == 1)
    def _():
        w1_scr[...] = w1_ref[...].astype(BF16)
        w2_scr[...] = w2_ref[...].astype(BF16)

    @pl.when(it < nit_ref[0])
    def _():
        x_scr[...] = x_ref[...].reshape(x_scr.shape)
        hm = _dot(x_scr[...].astype(BF16), w1_scr[...]) + b1_ref[...]
        glu = jnp.minimum(hm[:, :D_EXPERT], SWIGLU_LIMIT)
        lin = jnp.clip(hm[:, D_EXPERT:], -SWIGLU_LIMIT, SWIGLU_LIMIT)
        act = glu * _sigmoid(SWIGLU_ALPHA * glu) * (lin + 1.0)
        y = _dot(act.astype(BF16), w2_scr[...]) + b2_ref[...]
        row = lax.broadcasted_iota(jnp.int32, (y.shape[0], 1), 0)
        y = jnp.where((row >= lo_ref[it]) & (row < hi_ref[it]), y, 0.0)

        @pl.when(fblk_ref[it] == 1)
        def _():
            acc_scr[...] = y

        @pl.when(fblk_ref[it] == 0)
        def _():
            acc_scr[...] = acc_scr[...] + y

        o_ref[...] = acc_scr[...].reshape(o_ref.shape)


def _moe_call(rows, plan, layer, w1, b1, w2, b2):
    n_rows, _, d = rows.shape
    tm = TM_MOE
    depth, ne, _, dh2 = w1.shape
    de = w2.shape[2]
    n_items = plan[0].shape[0]

    def row_map(it, blk, ex, lo, hi, fb, fe, nit):
        return (blk[it], 0, 0)

    def w_map(it, blk, ex, lo, hi, fb, fe, nit):
        return (layer, ex[it], 0, 0)

    grid_spec = pltpu.PrefetchScalarGridSpec(
        num_scalar_prefetch=7,
        grid=(n_items,),
        in_specs=[
            pl.BlockSpec((tm, 1, d), row_map),
            pl.BlockSpec((None, None, d, dh2), w_map),
            pl.BlockSpec((None, None, 1, dh2), w_map),
            pl.BlockSpec((None, None, de, d), w_map),
            pl.BlockSpec((None, None, 1, d), w_map),
        ],
        out_specs=pl.BlockSpec((tm, 1, d), row_map),
        scratch_shapes=[pltpu.VMEM((d, dh2), BF16), pltpu.VMEM((de, d), BF16), pltpu.VMEM((tm, d), F32),
                        pltpu.VMEM((tm, d), F32)],
    )
    return pl.pallas_call(
        _moe_kernel,
        grid_spec=grid_spec,
        out_shape=jax.ShapeDtypeStruct((n_rows, 1, d), F32),
        compiler_params=_cparams(("arbitrary",)),
        name="moe",
    )(*plan, rows, w1, b1.reshape(depth, ne, 1, dh2), w2, b2.reshape(depth, ne, 1, d))


def _moe_plan(counts, n_rows):
    tm = TM_MOE
    ne = counts.shape[0]
    n_items = n_rows // tm + ne - 1
    counts = counts.astype(jnp.int32)
    ends = jnp.cumsum(counts)
    starts = ends - counts
    first_blk = starts // tm
    last_blk = jnp.maximum(ends - 1, 0) // tm
    per_exp = jnp.where(counts > 0, last_blk - first_blk + 1, 0)
    item_end = jnp.cumsum(per_exp)
    item_off = item_end - per_exp
    total = item_end[-1]
    it = jnp.arange(n_items, dtype=jnp.int32)
    itc = jnp.minimum(it, total - 1)
    ex = jnp.minimum(jnp.sum((itc[:, None] >= item_end[None, :]).astype(jnp.int32), axis=1), ne - 1)
    hot = ex[:, None] == jnp.arange(ne, dtype=jnp.int32)[None, :]

    def pick(v):
        return jnp.sum(jnp.where(hot, v[None, :], 0), axis=1)

    blk = pick(first_blk) + itc - pick(item_off)
    lo = jnp.maximum(pick(starts), blk * tm) - blk * tm
    hi = jnp.where(it < total, jnp.minimum(pick(ends), (blk + 1) * tm) - blk * tm, lo)
    one = jnp.ones((1,), jnp.int32)
    f_blk = jnp.concatenate([one, (blk[1:] != blk[:-1]).astype(jnp.int32)])
    f_exp = jnp.concatenate([one, (ex[1:] != ex[:-1]).astype(jnp.int32)])
    return (blk, ex, lo, hi, f_blk, f_exp, total.reshape(1)), starts


def _combine_kernel(t_all, dest_ref, y_hbm, gate_ref, x_ref, mod_ref, o_ref, buf, y_scr, sems):
    tm = TM_COMBINE
    i = pl.program_id(0)

    def start_gather(tile, slot):
        t0 = tile * tm

        def body(t, carry):
            for kq in range(TOP_K):
                pltpu.make_async_copy(y_hbm.at[dest_ref[kq * t_all + t0 + t]], buf.at[slot, kq * tm + t],
                                      sems.at[slot]).start(priority=kq % 2)
            return carry

        lax.fori_loop(0, tm, body, 0, unroll=ISSUE_UNROLL)

    @pl.when(i == 0)
    def _():
        start_gather(0, 0)

    @pl.when(i + 1 < pl.num_programs(0))
    def _():
        start_gather(i + 1, (i + 1) % 2)

    slot = i % 2
    pltpu.make_async_copy(y_hbm.at[pl.ds(0, TOP_K * tm)], buf.at[slot], sems.at[slot]).wait()
    y_scr[...] = buf[slot].reshape(y_scr.shape)
    gate = gate_ref[...]
    acc = gate[:, 0:1] * y_scr[0:tm, :]
    for kq in range(1, TOP_K):
        acc = acc + gate[:, kq:kq + 1] * y_scr[kq * tm:(kq + 1) * tm, :]
    o_ref[...] = x_ref[...] + mod_ref[5:6, :] * acc


def _combine_call(y_rows, dest_flat, gate_tk, x, mod, row_of_tile):
    t = gate_tk.shape[0]
    d = y_rows.shape[-1]
    tm = TM_COMBINE
    grid_spec = pltpu.PrefetchScalarGridSpec(
        num_scalar_prefetch=1,
        grid=(t // tm,),
        in_specs=[pl.BlockSpec(memory_space=pl.ANY), pl.BlockSpec((tm, TOP_K), lambda i, dr: (i, 0)),
                  pl.BlockSpec((tm, d), lambda i, dr: (i, 0)),
                  pl.BlockSpec((None, N_MOD, d), lambda i, dr: (row_of_tile(i, tm), 0, 0))],
        out_specs=pl.BlockSpec((tm, d), lambda i, dr: (i, 0)),
        scratch_shapes=[pltpu.VMEM((2, TOP_K * tm, 1, d), F32), pltpu.VMEM((TOP_K * tm, d), F32),
                        pltpu.SemaphoreType.DMA((2,))],
    )
    return pl.pallas_call(
        functools.partial(_combine_kernel, t),
        grid_spec=grid_spec,
        out_shape=jax.ShapeDtypeStruct((t, d), F32),
        compiler_params=_cparams(("arbitrary",)),
        name="moe_combine",
    )(dest_flat, y_rows, gate_tk, x, mod)


def _moe_layer(x_mid, mod, row_of_tile, h3, e_t, gate_t, rank_t, counts, layer, w1, b1, w2, b2):
    t = h3.shape[0]
    ne = w1.shape[1]
    plan, starts = _moe_plan(counts, TOP_K * t)
    start_of = jnp.sum(jnp.where(e_t[..., None] == jnp.arange(ne, dtype=jnp.int32), starts, 0), axis=-1)
    dest_flat = (start_of + rank_t).reshape(-1)
    rows = _dispatch_call(h3, dest_flat)
    y_rows = _moe_call(rows, plan, layer, w1, b1, w2, b2)
    return _combine_call(y_rows, dest_flat, gate_t.T, x_mid, mod, row_of_tile)


def _final_kernel(x_ref, g_ref, o_ref):
    x = x_ref[...]
    ms = jnp.mean(x * x, axis=-1, keepdims=True)
    o_ref[...] = x * lax.rsqrt(ms + RMS_EPS) * g_ref[...]


def _final_call(x, g_final):
    t, d = x.shape
    tm = TM_IN
    spec = pl.BlockSpec((tm, d), lambda i: (i, 0))
    return pl.pallas_call(
        _final_kernel,
        grid=(t // tm,),
        in_specs=[spec, pl.BlockSpec((1, d), lambda i: (0, 0))],
        out_specs=spec,
        out_shape=jax.ShapeDtypeStruct((t, d), F32),
        compiler_params=_cparams(("arbitrary",)),
        name="final_norm",
    )(x, g_final.reshape(1, d))


def _permute_cols(w):
    o = _SRC_OFF
    return jnp.concatenate([
        w[..., o["rwkv"]:o["lora"]], w[..., o["conv"]:o["qkv"]], w[..., o["gates"]:],
        w[..., o["qkv"]:o["rwkv"]], w[..., o["lora"]:o["gates"]]], axis=-1)


def kernel(x_prompt, x_sample, cache_k, cache_v, state_rwkv_fwd, state_rwkv_bwd, c, c_ctx, w_ada, b_ada, g_norm1, g_norm2, w_in, conv_w, attn_sink, rwkv_w0, rwkv_w2, rwkv_a0, rwkv_a2, rwkv_g2, rwkv_k_k, rwkv_k_a, rwkv_r_k, rwkv_ln_g, rwkv_ln_b, w_branch_conv, w_branch_attn, w_branch_rwkv, w_out, router_w, router_b, moe_w1, moe_b1, moe_w2, moe_b2, g_final):
    bc, lc, d = x_prompt.shape
    bl, tl, _ = x_sample.shape
    depth = w_in.shape[0]
    n_ctx = bc * lc
    t_all = n_ctx + bl * tl
    assert lc % TM_MIX == 0 and tl % TM_IN == 0 and n_ctx % TM_IN == 0 and t_all % TM_ROUTE == 0

    def row_of_tile(i, tm):
        return jnp.where(i < n_ctx // tm, 0, 1 + (i - n_ctx // tm) // (tl // tm))

    xs = (x_prompt.reshape(n_ctx, d), x_sample.reshape(bl * tl, d))
    n_cond = -(-(1 + bl) // 8) * 8
    cond = jnp.zeros((n_cond, d), F32).at[0].set(c_ctx).at[1:1 + bl].set(c)
    mods = _ada_call(cond, w_ada, b_ada).reshape(depth, n_cond, N_MOD, d)

    w_in_p = _permute_cols(w_in).astype(BF16)
    head_id = np.arange(PAIR) // RWKV_HEAD
    ones_pair = jnp.asarray((head_id[:, None] == head_id[None, :]).astype(np.float32), BF16)
    cos_t, sin_t = _rope_tables(tl)
    zeros_state = jnp.zeros((bc, 2 * N_PAIR, RWKV_HEAD, RWKV_HEAD), F32)
    tiles = (n_ctx // TM_MIX, lc // TM_MIX, tl // TM_MIX)

    new_k, new_v, new_sf, new_sb = [], [], [], []
    for l in range(depth):
        mod = mods[l]
        lw = dict(
            w0=rwkv_w0[l], w2=rwkv_w2[l], a0=rwkv_a0[l], a2=rwkv_a2[l], g2=rwkv_g2[l],
            kk_w=rwkv_k_k[l].reshape(1, -1), ka_w=rwkv_k_a[l].reshape(1, -1), rk_w=rwkv_r_k[l].reshape(1, -1),
            ln_g=rwkv_ln_g[l].reshape(1, -1), ln_b=rwkv_ln_b[l].reshape(1, -1), conv_w=conv_w[l],
            ones_pair=ones_pair,
            wa=w_branch_conv[l].astype(BF16), wb=w_branch_attn[l].astype(BF16), wc=w_branch_rwkv[l].astype(BF16),
            wo=w_out[l].astype(BF16),
        )
        z = _in_call(xs, w_in_p, l, g_norm1[l], mod, row_of_tile)

        kv_ctx = z[:n_ctx, COL_QKV + ATT_W:COL_QKV + ATT_W + 2 * KV_W].astype(F32)
        new_k.append(kv_ctx[:, :KV_W].reshape(bc, lc, N_KV_HEADS, HEAD_DIM))
        new_v.append(kv_ctx[:, KV_W:].reshape(bc, lc, N_KV_HEADS, HEAD_DIM))

        ya_c = _attn_ctx_call(z, attn_sink[l], bc, lc)
        ya_l = _attn_lat_call(z, attn_sink[l], cache_k[:, l].reshape(bl, -1, KV_W), cache_v[:, l].reshape(bl, -1, KV_W),
                              cos_t, sin_t, bl, tl, n_ctx)

        of_c, ob_c, zf_c, zb_c = _rwkv_call(z, lw, _state_to_z(zeros_state), _state_to_z(zeros_state), bc, lc, 0)
        of_l, ob_l, _, _ = _rwkv_call(z, lw, _state_to_z(state_rwkv_fwd[:, l].astype(F32)),
                                      _state_to_z(state_rwkv_bwd[:, l].astype(F32)), bl, tl, n_ctx)
        new_sf.append(_z_to_state(zf_c))
        new_sb.append(_z_to_state(zb_c))

        x_mid = _mix_call(xs, z, (of_c, of_l), (ob_c, ob_l), (ya_c, ya_l), mod, row_of_tile, lw, tiles)
        h2, e_t, gate_t, rank_t, cnt = _route_call(x_mid, mod, row_of_tile, g_norm2[l], router_w[l], router_b[l])
        xs = (_moe_layer(x_mid, mod, row_of_tile, h2, e_t, gate_t, rank_t, cnt[:, 0], l, moe_w1, moe_b1, moe_w2, moe_b2),)

    y = _final_call(xs[0], g_final)
    y_prompt = y[:n_ctx].reshape(bc, lc, d)
    y_sample = y[n_ctx:].reshape(bl, tl, d)
    dt = x_prompt.dtype
    return (y_prompt, y_sample, jnp.stack(new_k, axis=1), jnp.stack(new_v, axis=1),
            jnp.stack(new_sf, axis=1).astype(dt), jnp.stack(new_sb, axis=1).astype(dt))
```

```python
import functools

import numpy as np
import jax
import jax.numpy as jnp
from jax import lax
from jax.experimental import pallas as pl
from jax.experimental.pallas import tpu as pltpu

F32 = jnp.float32
BF16 = jnp.bfloat16

D_MODEL = 1024
N_MOD = 6
RMS_EPS = 1e-6
CONV_W = 512
N_HEADS = 8
N_KV_HEADS = 2
GQA_GROUP = N_HEADS // N_KV_HEADS
HEAD_DIM = 64
ATT_W = N_HEADS * HEAD_DIM
KV_W = N_KV_HEADS * HEAD_DIM
WINDOW = 128
Q_BLOCK = 128
ATTN_SCALE = HEAD_DIM ** -0.5
ROPE_THETA = 10000.0
GRID_W = 64
RWKV_HEAD = 64
RWKV_W = 512
DECAY_LORA = 64
ICLR_LORA = 64
GATE_LORA = 128
GN_EPS = 64e-5
N_EXPERTS = 32
TOP_K = 4
D_EXPERT = 1024
SWIGLU_LIMIT = 7.0
SWIGLU_ALPHA = 1.702
P_TOTAL = 7296

_SRC_OFF = dict(conv=0, qkv=1536, rwkv=2304, lora=3840, gates=4224)
COL_RWKV, COL_CONV, COL_GATES, COL_QKV, COL_LORA = 0, 1536, 3072, 6144, 6912
LORA_W = 2 * DECAY_LORA + 2 * ICLR_LORA + GATE_LORA

CHUNK = 64
CHUNKS_PER_STEP = 2
PAIR = 2 * RWKV_HEAD
N_PAIR = RWKV_W // PAIR
NEG_BIG = -1e30
HALO = 16
EXP_NEG_HALF = float(np.exp(-0.5))

TM_IN = 1024
TN_IN = 2432
TM_MIX = 256
TM_ROUTE = 512
TM_MOE = 512
TM_DISPATCH = 512
TM_COMBINE = 256
ISSUE_UNROLL = 8
VMEM_LIMIT = 56 * 1024 * 1024


def _cparams(sem, vmem=VMEM_LIMIT):
    return pltpu.CompilerParams(dimension_semantics=sem, vmem_limit_bytes=vmem)


def _dot(a, b, dims=(((1,), (0,)), ((), ()))):
    return lax.dot_general(a, b, dims, preferred_element_type=F32)


_NT = (((1,), (1,)), ((), ()))
_TN = (((0,), (0,)), ((), ()))


def _split(x):
    hi = x.astype(BF16)
    lo = (x - hi.astype(F32)).astype(BF16)
    return hi, lo


def _dot3(a, b, dims=(((1,), (0,)), ((), ()))):
    ah, al = _split(a)
    bh, bl = _split(b)
    return _dot(ah, bh, dims) + (_dot(ah, bl, dims) + _dot(al, bh, dims))


def _head_sums(x, ones_pair):
    rows = x.shape[0]
    xs = jnp.concatenate([x[:, PAIR * p:PAIR * (p + 1)] for p in range(N_PAIR)], axis=0).astype(BF16)
    s = _dot(xs, ones_pair)
    return jnp.concatenate([s[rows * p:rows * (p + 1)] for p in range(N_PAIR)], axis=1)


def _sigmoid(x):
    return 1.0 / (1.0 + jnp.exp(-x))


def _rms_mod(x, g, scale, shift):
    ms = jnp.mean(x * x, axis=-1, keepdims=True)
    return (x * lax.rsqrt(ms + RMS_EPS) * g) * (1.0 + scale) + shift


def _ada_kernel(cond_ref, w_ref, b_ref, o_ref):
    c = cond_ref[...]
    s = c * _sigmoid(c)
    o_ref[...] = _dot(s.astype(BF16), w_ref[...].astype(BF16)) + b_ref[...]


def _ada_call(cond, w_ada, b_ada):
    depth, d, n = w_ada.shape
    r = cond.shape[0]
    tn = 1536
    return pl.pallas_call(
        _ada_kernel,
        grid=(depth, n // tn),
        in_specs=[
            pl.BlockSpec((r, d), lambda l, j: (0, 0)),
            pl.BlockSpec((None, d, tn), lambda l, j: (l, 0, j)),
            pl.BlockSpec((None, 1, tn), lambda l, j: (l, 0, j)),
        ],
        out_specs=pl.BlockSpec((None, r, tn), lambda l, j: (l, 0, j)),
        out_shape=jax.ShapeDtypeStruct((depth, r, n), F32),
        compiler_params=_cparams(("arbitrary", "arbitrary")),
        name="ada",
    )(cond, w_ada, b_ada.reshape(depth, 1, n))


def _in_kernel(n_first, *refs):
    if n_first is None:
        x_ref, mod_ref, g_ref, w_ref, z_ref = refs
        x = x_ref[...]
    else:
        xa_ref, xb_ref, mod_ref, g_ref, w_ref, z_ref = refs
        x = jnp.where(pl.program_id(1) < n_first, xa_ref[...], xb_ref[...])
    h = _rms_mod(x, g_ref[...], mod_ref[1:2, :], mod_ref[0:1, :]).astype(BF16)
    z_ref[...] = _dot(h, w_ref[...]).astype(BF16)


def _in_call(xs, w_bf16, layer, g_norm, mod, row_of_tile, tm):
    d = xs[0].shape[1]
    t = sum(x.shape[0] for x in xs)
    n = w_bf16.shape[2]
    tn = TN_IN
    if len(xs) == 1:
        n_first = None
        x_specs = [pl.BlockSpec((tm, d), lambda j, i: (i, 0))]
    else:
        n_first = xs[0].shape[0] // tm
        x_specs = [pl.BlockSpec((tm, d), lambda j, i: (jnp.minimum(i, n_first - 1), 0)),
                   pl.BlockSpec((tm, d), lambda j, i: (jnp.maximum(i - n_first, 0), 0))]
    return pl.pallas_call(
        functools.partial(_in_kernel, n_first),
        grid=(n // tn, t // tm),
        in_specs=x_specs + [
            pl.BlockSpec((None, N_MOD, d), lambda j, i: (row_of_tile(i, tm), 0, 0)),
            pl.BlockSpec((1, d), lambda j, i: (0, 0)),
            pl.BlockSpec((None, d, tn), lambda j, i: (layer, 0, j)),
        ],
        out_specs=pl.BlockSpec((tm, tn), lambda j, i: (i, j)),
        out_shape=jax.ShapeDtypeStruct((t, n), BF16),
        compiler_params=_cparams(("arbitrary", "arbitrary")),
        name="in_proj",
    )(*xs, mod, g_norm.reshape(1, d), w_bf16)


def _softmax_pv(scores, values, sink):
    m = sink
    for s in scores:
        m = jnp.maximum(m, jnp.max(s, axis=-1, keepdims=True))
    den = jnp.exp(sink - m)
    acc = None
    for s, v in zip(scores, values):
        p = jnp.exp(s - m)
        den = den + jnp.sum(p, axis=-1, keepdims=True)
        pv = _dot(p.astype(BF16), v)
        acc = pv if acc is None else acc + pv
    return acc / den


def _attn_ctx_kernel(sink_ref, q_ref, kv_ref, o_ref):
    q = (q_ref[...] * ATTN_SCALE).astype(BF16)
    kv = kv_ref[...].astype(BF16)
    outs = []
    for h in range(N_HEADS):
        g = h // GQA_GROUP
        qh = q[:, HEAD_DIM * h:HEAD_DIM * (h + 1)]
        kh = kv[:, HEAD_DIM * g:HEAD_DIM * (g + 1)]
        vh = kv[:, KV_W + HEAD_DIM * g:KV_W + HEAD_DIM * (g + 1)]
        outs.append(_softmax_pv([_dot(qh, kh, _NT)], [vh], sink_ref[h]))
    o_ref[...] = jnp.concatenate(outs, axis=-1).astype(BF16)


def _attn_ctx_call(z, sink, n_seq, seq_len):
    return pl.pallas_call(
        _attn_ctx_kernel,
        grid=(n_seq,),
        in_specs=[
            pl.BlockSpec(memory_space=pltpu.SMEM),
            pl.BlockSpec((seq_len, ATT_W), lambda s: (s, COL_QKV // ATT_W)),
            pl.BlockSpec((seq_len, 2 * KV_W), lambda s: (s, (COL_QKV + ATT_W) // (2 * KV_W))),
        ],
        out_specs=pl.BlockSpec((seq_len, ATT_W), lambda s: (s, 0)),
        out_shape=jax.ShapeDtypeStruct((n_seq * seq_len, ATT_W), BF16),
        compiler_params=_cparams(("arbitrary",)),
        name="attn_ctx",
    )(sink, z, z)


def _rope(x, cos, sin_signed):
    n = x.shape[-1]
    lane = lax.broadcasted_iota(jnp.int32, x.shape, 1)
    up = pltpu.roll(x, n - 16, 1)
    dn = pltpu.roll(x, 16, 1)
    partner = jnp.where((lane & 31) < 16, up, dn)
    return x * cos + partner * sin_signed


def _attn_lat_kernel(sink_ref, q_ref, kvp_ref, kvc_ref, kvn_ref, ck_ref, cv_ref,
                     cosc_ref, sinc_ref, cosp_ref, sinp_ref, cosn_ref, sinn_ref, o_ref):
    step = pl.program_id(1)
    n_steps = pl.num_programs(1)
    qb2 = 2 * Q_BLOCK
    cc, sc = cosc_ref[...], sinc_ref[...]
    q = _rope(q_ref[...].astype(F32), jnp.concatenate([cc] * 4, axis=1), jnp.concatenate([sc] * 4, axis=1))
    q = (q * ATTN_SCALE).astype(BF16)
    kvp, kvc, kvn = (ref[...].astype(F32) for ref in (kvp_ref, kvc_ref, kvn_ref))
    kp = _rope(kvp[:, :KV_W], cosp_ref[...], sinp_ref[...]).astype(BF16)
    kc = _rope(kvc[:, :KV_W], cc, sc).astype(BF16)
    kn = _rope(kvn[:, :KV_W], cosn_ref[...], sinn_ref[...]).astype(BF16)
    vp, vc, vn = (t[:, KV_W:].astype(BF16) for t in (kvp, kvc, kvn))
    ck = ck_ref[...].astype(BF16)
    cv = cv_ref[...].astype(BF16)
    k_cat = [jnp.concatenate([kp, kc, ck], axis=0), jnp.concatenate([kc, kn, ck], axis=0)]
    v_cat = [jnp.concatenate([vp, vc, cv], axis=0), jnp.concatenate([vc, vn, cv], axis=0)]
    nk = k_cat[0].shape[0]
    qi = lax.broadcasted_iota(jnp.int32, (Q_BLOCK, nk), 0)
    kj = lax.broadcasted_iota(jnp.int32, (Q_BLOCK, nk), 1)
    in_next = (kj >= qb2) & (kj < 3 * Q_BLOCK)
    before_ok = (step > 0, True)
    after_ok = (True, step < n_steps - 1)
    bias = []
    for u in range(2):
        ok_p = (kj >= qi) & before_ok[u]
        ok_n = (kj - qb2 <= qi) & after_ok[u]
        bu = jnp.where(kj < Q_BLOCK, jnp.where(ok_p, 0.0, NEG_BIG), jnp.where(in_next, jnp.where(ok_n, 0.0, NEG_BIG), 0.0))
        bias.append(jnp.concatenate([bu] * GQA_GROUP, axis=0))
    chains = [(u, g) for u in range(2) for g in range(N_KV_HEADS)]
    heads = {g: range(GQA_GROUP * g, GQA_GROUP * (g + 1)) for g in range(N_KV_HEADS)}
    gsl = {g: slice(HEAD_DIM * g, HEAD_DIM * (g + 1)) for g in range(N_KV_HEADS)}
    q_g = {(u, g): jnp.concatenate([q[Q_BLOCK * u:Q_BLOCK * (u + 1), HEAD_DIM * h:HEAD_DIM * (h + 1)] for h in heads[g]], axis=0)
           for u, g in chains}
    sink = {g: jnp.concatenate([jnp.full((Q_BLOCK, 1), sink_ref[h], F32) for h in heads[g]], axis=0) for g in range(N_KV_HEADS)}
    s = {ch: _dot(q_g[ch], k_cat[ch[0]][:, gsl[ch[1]]], _NT) + bias[ch[0]] for ch in chains}
    m = {ch: jnp.maximum(jnp.max(s[ch], axis=-1, keepdims=True), sink[ch[1]]) for ch in chains}
    p = {ch: jnp.exp(s[ch] - m[ch]) for ch in chains}
    den = {ch: jnp.sum(p[ch], axis=-1, keepdims=True) + jnp.exp(sink[ch[1]] - m[ch]) for ch in chains}
    o = {ch: _dot(p[ch].astype(BF16), v_cat[ch[0]][:, gsl[ch[1]]]) / den[ch] for ch in chains}
    for u in range(2):
        outs = [o[u, g][Q_BLOCK * i:Q_BLOCK * (i + 1)] for g in range(N_KV_HEADS) for i in range(GQA_GROUP)]
        o_ref[Q_BLOCK * u:Q_BLOCK * (u + 1), :] = jnp.concatenate(outs, axis=-1).astype(BF16)


def _attn_lat_call(z, sink, cache_k, cache_v, cos_t, sin_t, n_seq, seq_len, row0):
    nb = seq_len // Q_BLOCK
    ns = nb // 2
    base = row0 // Q_BLOCK
    past = cache_k.shape[1]
    kv_col = (COL_QKV + ATT_W) // (2 * KV_W)

    def kv_spec(off):
        return pl.BlockSpec((Q_BLOCK, 2 * KV_W), lambda b, s: (base + b * nb + jnp.clip(2 * s + off, 0, nb - 1), kv_col))

    def tab_spec(off):
        return pl.BlockSpec((Q_BLOCK, KV_W), lambda b, s: (jnp.clip(2 * s + off, 0, nb - 1), 0))

    return pl.pallas_call(
        _attn_lat_kernel,
        grid=(n_seq, ns),
        in_specs=[
            pl.BlockSpec(memory_space=pltpu.SMEM),
            pl.BlockSpec((2 * Q_BLOCK, ATT_W), lambda b, s: (base // 2 + b * ns + s, COL_QKV // ATT_W)),
            kv_spec(-1),
            pl.BlockSpec((2 * Q_BLOCK, 2 * KV_W), lambda b, s: (base // 2 + b * ns + s, kv_col)),
            kv_spec(2),
            pl.BlockSpec((None, past, KV_W), lambda b, s: (b, 0, 0)),
            pl.BlockSpec((None, past, KV_W), lambda b, s: (b, 0, 0)),
            pl.BlockSpec((2 * Q_BLOCK, KV_W), lambda b, s: (s, 0)), pl.BlockSpec((2 * Q_BLOCK, KV_W), lambda b, s: (s, 0)),
            tab_spec(-1), tab_spec(-1), tab_spec(2), tab_spec(2),
        ],
        out_specs=pl.BlockSpec((2 * Q_BLOCK, ATT_W), lambda b, s: (b * ns + s, 0)),
        out_shape=jax.ShapeDtypeStruct((n_seq * seq_len, ATT_W), BF16),
        compiler_params=_cparams(("arbitrary", "arbitrary")),
        name="attn_lat",
    )(sink, z, z, z, z, cache_k, cache_v, cos_t, sin_t, cos_t, sin_t, cos_t, sin_t)


def _rope_tables(seq_len):
    half = HEAD_DIM // 2
    pos = np.arange(seq_len)
    inv_freq = 1.0 / (ROPE_THETA ** (np.arange(0, half, 2, dtype=np.float32) / half))
    inv_freq = inv_freq.astype(np.float32)

    def part(p):
        ang = (p.astype(np.float32)[:, None] * inv_freq[None, :]).astype(np.float32)
        c, s = np.cos(ang), np.sin(ang)
        return np.concatenate([c, c], axis=1), np.concatenate([-s, s], axis=1)

    c_r, s_r = part(pos // GRID_W)
    c_c, s_c = part(pos % GRID_W)
    cos = np.concatenate([c_r, c_c] * N_KV_HEADS, axis=1).astype(np.float32)
    sin = np.concatenate([s_r, s_c] * N_KV_HEADS, axis=1).astype(np.float32)
    return jnp.asarray(cos), jnp.asarray(sin)


def _rwkv_dir_inputs(zm, zl, d, w0, w2, a0, a2, kk_w, ka_w, ones_pair):
    r = zm[:, :RWKV_W]
    kraw = zm[:, RWKV_W:2 * RWKV_W]
    v = zm[:, 2 * RWKV_W:]
    wl = zl[:, DECAY_LORA * d:DECAY_LORA * (d + 1)]
    al = zl[:, 2 * DECAY_LORA + ICLR_LORA * d:2 * DECAY_LORA + ICLR_LORA * (d + 1)]
    xw = w0 + _dot(jnp.tanh(wl).astype(BF16), w2.astype(BF16))
    ld = -EXP_NEG_HALF * _sigmoid(xw)
    a = _sigmoid(a0 + _dot(al.astype(BF16), a2.astype(BF16)))
    k = kraw * (1.0 + (a - 1.0) * ka_w)
    kkr = kraw * kk_w
    n2 = _head_sums(kkr * kkr, ones_pair)
    kk = kkr / jnp.maximum(jnp.sqrt(n2), 1e-12)
    return r, v, kk, ld, a, k


def _tri_masks(rev):
    c = CHUNK
    ti = lax.broadcasted_iota(jnp.int32, (c, c), 0)
    si = lax.broadcasted_iota(jnp.int32, (c, c), 1)
    incl = (si >= ti) if rev else (si <= ti)
    strict = (si > ti) if rev else (si < ti)
    return incl, strict, (si == ti).astype(F32)


def _chunk_prepare(r, v, kk, ld, a, k, rev):
    incl, _, _ = _tri_masks(rev)
    m_incl = jnp.where(incl, 1.0, 0.0).astype(BF16)
    ldh, ldl = _split(ld)
    cin = _dot(m_incl, ldh) + _dot(m_incl, ldl)
    tot = jnp.sum(ld, axis=0, keepdims=True)
    e_neg = jnp.exp(-cin)
    e_end = jnp.exp(tot - cin)
    bb = kk * a
    return dict(
        a_m=kk * jnp.exp(cin - ld), r_m=r * jnp.exp(cin),
        b_m=(bb * e_neg).astype(BF16), k_m=(k * e_neg).astype(BF16),
        b_end=(bb * e_end).astype(BF16), k_end=(k * e_end).astype(BF16),
        v=v, e_tot=jnp.exp(tot), rev=rev)


def _chunk_problems(probs):
    c = CHUNK
    n = len(probs)
    rng = range(n)
    lane = lax.broadcasted_iota(jnp.int32, (1, PAIR), 1)
    first_head = lane < RWKV_HEAD

    def bd(x):
        xb = x.astype(BF16)
        zero = jnp.zeros_like(xb)
        return jnp.concatenate([jnp.where(first_head, xb, zero), jnp.where(first_head, zero, xb)], axis=0)

    ti = lax.broadcasted_iota(jnp.int32, (c, PAIR), 0)
    si = lax.broadcasted_iota(jnp.int32, (c, PAIR), 1) & (RWKV_HEAD - 1)
    incl = {False: si <= ti, True: si >= ti}
    strict = {False: si < ti, True: si > ti}
    eye = (si == ti).astype(F32)
    rev = [p["rev"] for p in probs]

    lhs = [jnp.concatenate([probs[i]["a_m"], probs[i]["r_m"]], axis=0).astype(BF16) for i in rng]
    xb = [_dot(lhs[i], bd(probs[i]["b_m"]), _NT) for i in rng]
    xk = [_dot(lhs[i], bd(probs[i]["k_m"]), _NT) for i in rng]
    m_ab = [jnp.where(strict[rev[i]], xb[i][:c], 0.0) for i in rng]
    m_ak = [jnp.where(strict[rev[i]], xk[i][:c], 0.0).astype(BF16) for i in rng]
    m_rb = [jnp.where(incl[rev[i]], xb[i][c:], 0.0).astype(BF16) for i in rng]
    m_rk = [jnp.where(incl[rev[i]], xk[i][c:], 0.0).astype(BF16) for i in rng]
    v_bd = [bd(probs[i]["v"]) for i in rng]
    mv = [_dot(jnp.concatenate([m_ak[i], m_rk[i]], axis=0), v_bd[i]) for i in rng]
    mak_v = [mv[i][:c] for i in rng]
    mrk_v = [mv[i][c:] for i in rng]
    t_inv = [eye - m_ab[i] for i in rng]
    lp = [_dot(m_ab[i].astype(BF16), bd(m_ab[i])) for i in rng]
    for _ in range(4):
        both = [_dot(jnp.concatenate([lp[i], t_inv[i]], axis=0).astype(BF16), bd(lp[i])) for i in rng]
        t_inv = [t_inv[i] + both[i][c:] for i in rng]
        lp = [both[i][:c] for i in rng]
    t_inv = [t_inv[i] + _dot(t_inv[i].astype(BF16), bd(lp[i])) for i in rng]
    wu = [_dot(t_inv[i].astype(BF16), jnp.concatenate([bd(probs[i]["a_m"]), bd(mak_v[i])], axis=1)) for i in rng]
    ro = [_dot(m_rb[i], jnp.concatenate([bd(wu[i][:, :PAIR]), bd(wu[i][:, PAIR:])], axis=1)) for i in rng]
    ri = lax.broadcasted_iota(jnp.int32, (PAIR, PAIR), 0)
    ci = lax.broadcasted_iota(jnp.int32, (PAIR, PAIR), 1)
    same_head = (ri < RWKV_HEAD) == (ci < RWKV_HEAD)
    bt_wu = [_dot(probs[i]["b_end"], wu[i].astype(BF16), _TN) for i in rng]
    kt_v = [_dot(probs[i]["k_end"], probs[i]["v"].astype(BF16), _TN) for i in rng]
    local = []
    for i in rng:
        r_eff = probs[i]["r_m"] - ro[i][:, :PAIR]
        o_loc = mrk_v[i] - ro[i][:, PAIR:]
        g_t = jnp.where(same_head, jnp.where(ri == ci, probs[i]["e_tot"], 0.0) - bt_wu[i][:, :PAIR], 0.0)
        h_t = jnp.where(same_head, kt_v[i] - bt_wu[i][:, PAIR:], 0.0)
        local.append((_split(jnp.concatenate([r_eff, g_t], axis=0)), o_loc, h_t))
    return local


def _chunk_apply(local, z):
    (bh, bl), o_loc, h_t = local
    zh, zl = _split(z)
    prod = _dot(bh, zh) + (_dot(bh, zl) + _dot(bl, zh))
    return prod[:CHUNK] + o_loc, prod[CHUNK:] + h_t


def _rwkv_kernel(zmf_ref, zlf_ref, zmb_ref, zlb_ref, w0_ref, w2_ref, a0_ref, a2_ref, kkw_ref, kaw_ref,
                 ones_ref, s0f_ref, s0b_ref, of_ref, ob_ref, zf_ref, zb_ref, zf_scr, zb_scr):
    @pl.when(pl.program_id(1) == 0)
    def _():
        zf_scr[...] = s0f_ref[...]
        zb_scr[...] = s0b_ref[...]

    ones_pair = ones_ref[...]
    dirs = ((zmf_ref, zlf_ref, zf_scr, of_ref), (zmb_ref, zlb_ref, zb_scr, ob_ref))
    n_sub = zmf_ref.shape[0] // CHUNK
    probs = []
    for d, (zm_ref, zl_ref, _, _) in enumerate(dirs):
        r, v, kk, ld, a, k = _rwkv_dir_inputs(zm_ref[...].astype(F32), zl_ref[...].astype(F32), d, w0_ref[d:d + 1, :], w2_ref[d],
                                              a0_ref[d:d + 1, :], a2_ref[d], kkw_ref[...], kaw_ref[...], ones_pair)
        for sub in range(n_sub):
            rows = slice(CHUNK * sub, CHUNK * (sub + 1))
            full = _chunk_prepare(r[rows], v[rows], kk[rows], ld[rows], a[rows], k[rows], d == 1)
            for p in range(N_PAIR):
                ps = slice(PAIR * p, PAIR * (p + 1))
                probs.append({key: (val[:, ps] if hasattr(val, "shape") else val) for key, val in full.items()})
    local = _chunk_problems(probs)
    for d, (_, _, z_scr, o_ref) in enumerate(dirs):
        order = range(n_sub) if d == 0 else range(n_sub - 1, -1, -1)
        for p in range(N_PAIR):
            z = z_scr[p]
            for sub in order:
                out, z = _chunk_apply(local[(d * n_sub + sub) * N_PAIR + p], z)
                o_ref[CHUNK * sub:CHUNK * (sub + 1), PAIR * p:PAIR * (p + 1)] = out
            z_scr[p] = z
    zf_ref[...] = zf_scr[...]
    zb_ref[...] = zb_scr[...]


def _rwkv_call(z, lw, s0f, s0b, n_seq, seq_len, row0):
    rows = CHUNK * CHUNKS_PER_STEP
    nc = seq_len // rows
    base = row0 // rows

    def fwd(s, c):
        return base + s * nc + c

    def bwd(s, c):
        return base + s * nc + (nc - 1 - c)

    def zm_spec(f):
        return pl.BlockSpec((rows, 3 * RWKV_W), lambda s, c: (f(s, c), COL_RWKV // (3 * RWKV_W)))

    def zl_spec(f):
        return pl.BlockSpec((rows, LORA_W), lambda s, c: (f(s, c), COL_LORA // LORA_W))

    def full(shape):
        return pl.BlockSpec(shape, lambda s, c: (0,) * len(shape))

    st_spec = pl.BlockSpec((None, N_PAIR, PAIR, PAIR), lambda s, c: (s, 0, 0, 0))
    o_shape = jax.ShapeDtypeStruct((n_seq * seq_len, RWKV_W), F32)
    st_shape = jax.ShapeDtypeStruct((n_seq, N_PAIR, PAIR, PAIR), F32)
    return pl.pallas_call(
        _rwkv_kernel,
        grid=(n_seq, nc),
        in_specs=[
            zm_spec(fwd), zl_spec(fwd), zm_spec(bwd), zl_spec(bwd),
            full((2, RWKV_W)), full((2, DECAY_LORA, RWKV_W)), full((2, RWKV_W)), full((2, ICLR_LORA, RWKV_W)),
            full((1, RWKV_W)), full((1, RWKV_W)), full((PAIR, PAIR)),
            st_spec, st_spec,
        ],
        out_specs=[
            pl.BlockSpec((rows, RWKV_W), lambda s, c: (s * nc + c, 0)),
            pl.BlockSpec((rows, RWKV_W), lambda s, c: (s * nc + (nc - 1 - c), 0)),
            st_spec, st_spec,
        ],
        out_shape=[o_shape, o_shape, st_shape, st_shape],
        scratch_shapes=[pltpu.VMEM((N_PAIR, PAIR, PAIR), F32), pltpu.VMEM((N_PAIR, PAIR, PAIR), F32)],
        compiler_params=_cparams(("arbitrary", "arbitrary")),
        name="rwkv_scan",
    )(z, z, z, z, lw["w0"], lw["w2"], lw["a0"], lw["a2"], lw["kk_w"], lw["ka_w"], lw["ones_pair"], s0f, s0b)


def _state_to_z(s):
    n = s.shape[0]
    st = jnp.swapaxes(s, -1, -2).reshape(n, N_PAIR, 2, RWKV_HEAD, RWKV_HEAD)
    zero = jnp.zeros_like(st[:, :, 0])
    top = jnp.concatenate([st[:, :, 0], zero], axis=-1)
    bot = jnp.concatenate([zero, st[:, :, 1]], axis=-1)
    return jnp.concatenate([top, bot], axis=-2)


def _z_to_state(z):
    n = z.shape[0]
    h0 = z[:, :, :RWKV_HEAD, :RWKV_HEAD]
    h1 = z[:, :, RWKV_HEAD:, RWKV_HEAD:]
    st = jnp.stack([h0, h1], axis=2).reshape(n, 2 * N_PAIR, RWKV_HEAD, RWKV_HEAD)
    return jnp.swapaxes(st, -1, -2)


def _mix_kernel(tiles, n_x, *refs):
    n_ctx_tiles, per_ctx, per_lat = tiles
    i = pl.program_id(0)
    in_ctx = i < n_ctx_tiles

    def pick(pair):
        return jnp.where(in_ctx, pair[0][...], pair[1][...])

    x_refs, refs = refs[:n_x], refs[n_x:]
    (zc_ref, zcp_ref, zcn_ref, zm_ref, zl_ref, zg_ref, ofc_ref, ofl_ref, obc_ref, obl_ref, yac_ref, yal_ref, mod_ref,
     convw_ref, a0_ref, a2_ref, g2_ref, kaw_ref, rkw_ref, lng_ref, lnb_ref, ones_ref,
     wa_ref, wb_ref, wc_ref, wo_ref, o_ref) = refs
    x_in = x_refs[0][...] if n_x == 1 else pick(x_refs)
    pos = jnp.where(i < n_ctx_tiles, i % per_ctx, (i - n_ctx_tiles) % per_lat)
    per = jnp.where(i < n_ctx_tiles, per_ctx, per_lat)
    tm = o_ref.shape[0]

    zc = zc_ref[...].astype(F32)
    u = zc[:, 2 * CONV_W:] * zc[:, :CONV_W]
    zp = zcp_ref[...].astype(F32)
    zn = zcn_ref[...].astype(F32)
    u_prev_row = jnp.where(pos > 0, zp[HALO - 1:HALO, 2 * CONV_W:] * zp[HALO - 1:HALO, :CONV_W], 0.0)
    u_next_row = jnp.where(pos < per - 1, zn[0:1, 2 * CONV_W:] * zn[0:1, :CONV_W], 0.0)
    row = lax.broadcasted_iota(jnp.int32, (tm, CONV_W), 0)
    u_prev = jnp.where(row == 0, u_prev_row, pltpu.roll(u, 1, 0))
    u_next = jnp.where(row == tm - 1, u_next_row, pltpu.roll(u, tm - 1, 0))
    cw = convw_ref[...]
    y_conv = zc[:, CONV_W:2 * CONV_W] * (cw[0:1, :] * u_prev + cw[1:2, :] * u + cw[2:3, :] * u_next)

    zm = zm_ref[...].astype(F32)
    zl = zl_ref[...].astype(F32)
    r = zm[:, :RWKV_W]
    kraw = zm[:, RWKV_W:2 * RWKV_W]
    v = zm[:, 2 * RWKV_W:]
    ones_pair = ones_ref[...]
    o = pick((ofc_ref, ofl_ref)) + pick((obc_ref, obl_ref))
    mu = _head_sums(o, ones_pair) * (1.0 / RWKV_HEAD)
    dlt = o - mu
    var = _head_sums(dlt * dlt, ones_pair) * (1.0 / RWKV_HEAD)
    y = dlt * lax.rsqrt(var + GN_EPS) * lng_ref[...] + lnb_ref[...]
    for d in range(2):
        al = zl[:, 2 * DECAY_LORA + ICLR_LORA * d:2 * DECAY_LORA + ICLR_LORA * (d + 1)]
        a = _sigmoid(a0_ref[d:d + 1, :] + _dot(al.astype(BF16), a2_ref[d].astype(BF16)))
        k = kraw * (1.0 + (a - 1.0) * kaw_ref[...])
        y = y + _head_sums(r * k * rkw_ref[...], ones_pair) * v
    g1 = zl[:, 2 * DECAY_LORA + 2 * ICLR_LORA:]
    y_rwkv = y * _dot(_sigmoid(g1).astype(BF16), g2_ref[...].astype(BF16))

    zg = zg_ref[...].astype(F32)
    merged = (_sigmoid(zg[:, :D_MODEL]) * _dot(y_conv.astype(BF16), wa_ref[...])
              + _sigmoid(zg[:, D_MODEL:2 * D_MODEL]) * _dot(pick((yac_ref, yal_ref)), wb_ref[...])
              + _sigmoid(zg[:, 2 * D_MODEL:]) * _dot(y_rwkv.astype(BF16), wc_ref[...]))
    o_ref[...] = x_in + mod_ref[2:3, :] * _dot(merged.astype(BF16), wo_ref[...])


def _mix_call(xs, z, o_f, o_b, y_attn, mod, row_of_tile, lw, tiles):
    t, d = z.shape[0], xs[0].shape[1]
    tm = TM_MIX
    nt = t // tm
    hb = tm // HALO
    n_ctx_tiles = tiles[0]

    def rows(w, col):
        return pl.BlockSpec((tm, w), lambda i: (i, col // w))

    def pair(w):
        return [pl.BlockSpec((tm, w), lambda i: (jnp.minimum(i, n_ctx_tiles - 1), 0)),
                pl.BlockSpec((tm, w), lambda i: (jnp.maximum(i - n_ctx_tiles, 0), 0))]

    def full(shape):
        return pl.BlockSpec(shape, lambda i: (0,) * len(shape))

    in_specs = (pair(d) if len(xs) == 2 else [rows(d, 0)]) + [
        rows(3 * CONV_W, COL_CONV),
        pl.BlockSpec((HALO, 3 * CONV_W), lambda i: (jnp.maximum(i * hb - 1, 0), COL_CONV // (3 * CONV_W))),
        pl.BlockSpec((HALO, 3 * CONV_W), lambda i: (jnp.minimum((i + 1) * hb, nt * hb - 1), COL_CONV // (3 * CONV_W))),
        rows(3 * RWKV_W, COL_RWKV),
        rows(LORA_W, COL_LORA),
        rows(3 * D_MODEL, COL_GATES),
    ] + pair(RWKV_W) + pair(RWKV_W) + pair(ATT_W) + [
        pl.BlockSpec((None, N_MOD, d), lambda i: (row_of_tile(i, tm), 0, 0)),
        full((3, CONV_W)), full((2, RWKV_W)), full((2, ICLR_LORA, RWKV_W)), full((GATE_LORA, RWKV_W)),
        full((1, RWKV_W)), full((1, RWKV_W)), full((1, RWKV_W)), full((1, RWKV_W)),
        full((PAIR, PAIR)),
        full((CONV_W, d)), full((ATT_W, d)), full((RWKV_W, d)), full((d, d)),
    ]
    return pl.pallas_call(
        functools.partial(_mix_kernel, tiles, len(xs)),
        grid=(nt,),
        in_specs=in_specs,
        out_specs=rows(d, 0),
        out_shape=jax.ShapeDtypeStruct((t, d), F32),
        compiler_params=_cparams(("arbitrary",)),
        name="mix",
    )(*xs, z, z, z, z, z, z, *o_f, *o_b, *y_attn, mod,
      lw["conv_w"], lw["a0"], lw["a2"], lw["g2"], lw["ka_w"], lw["rk_w"], lw["ln_g"], lw["ln_b"],
      lw["ones_pair"], lw["wa"], lw["wb"], lw["wc"], lw["wo"])


def _route_kernel(x_ref, mod_ref, g_ref, rwt_ref, rb_ref, tri_ref, h_ref, e_ref, gate_ref, rank_ref, cnt_ref, cnt_scr):
    @pl.when(pl.program_id(0) == 0)
    def _():
        cnt_scr[...] = jnp.zeros_like(cnt_scr)

    h2 = _rms_mod(x_ref[...], g_ref[...], mod_ref[4:5, :], mod_ref[3:4, :])
    h_ref[...] = h2.reshape(h_ref.shape)
    logits = _dot3(rwt_ref[...], h2, _NT) + rb_ref[...]
    ne, tm = logits.shape
    ex = lax.broadcasted_iota(jnp.int32, (ne, tm), 0)
    work = logits
    vals, hots = [], []
    for kq in range(TOP_K):
        m = jnp.max(work, axis=0, keepdims=True)
        idx = jnp.min(jnp.where(work == m, ex, ne), axis=0, keepdims=True)
        hot = ex == idx
        vals.append(m)
        hots.append(hot)
        e_ref[kq:kq + 1, :] = idx
        work = jnp.where(hot, -jnp.inf, work)
    exps = [jnp.exp(vk - vals[0]) for vk in vals]
    den = exps[0] + exps[1] + exps[2] + exps[3]
    chosen = jnp.where(hots[0] | hots[1] | hots[2] | hots[3], 1.0, 0.0)
    before = cnt_scr[:, 0:1] + _dot(chosen.astype(BF16), tri_ref[...])
    for kq in range(TOP_K):
        gate_ref[kq:kq + 1, :] = exps[kq] / den
        rank_ref[kq:kq + 1, :] = jnp.sum(jnp.where(hots[kq], before, 0.0), axis=0, keepdims=True).astype(jnp.int32)
    cnt_scr[...] = cnt_scr[...] + jnp.sum(chosen, axis=1, keepdims=True)
    cnt_ref[...] = cnt_scr[...].astype(jnp.int32)


def _route_call(x, mod, row_of_tile, g_norm, router_w, router_b):
    t, d = x.shape
    tm = TM_ROUTE
    ne = router_w.shape[1]
    tri = jnp.triu(jnp.ones((tm, tm), F32), 1).astype(BF16)

    def full(shape):
        return pl.BlockSpec(shape, lambda i: (0,) * len(shape))

    kt_spec = pl.BlockSpec((TOP_K, tm), lambda i: (0, i))
    return pl.pallas_call(
        _route_kernel,
        grid=(t // tm,),
        in_specs=[
            pl.BlockSpec((tm, d), lambda i: (i, 0)),
            pl.BlockSpec((None, N_MOD, d), lambda i: (row_of_tile(i, tm), 0, 0)),
            full((1, d)), full((ne, d)), full((ne, 1)), full((tm, tm)),
        ],
        out_specs=[pl.BlockSpec((tm, 1, d), lambda i: (i, 0, 0)), kt_spec, kt_spec, kt_spec, full((ne, 128))],
        out_shape=[
            jax.ShapeDtypeStruct((t, 1, d), F32),
            jax.ShapeDtypeStruct((TOP_K, t), jnp.int32),
            jax.ShapeDtypeStruct((TOP_K, t), F32),
            jax.ShapeDtypeStruct((TOP_K, t), jnp.int32),
            jax.ShapeDtypeStruct((ne, 128), jnp.int32),
        ],
        scratch_shapes=[pltpu.VMEM((ne, 128), F32)],
        compiler_params=_cparams(("arbitrary",)),
        name="route",
    )(x, mod, g_norm.reshape(1, d), router_w.T, router_b.reshape(ne, 1), tri)


def _row_copy_wait(buf_hbm, n_rows, sem):
    view = buf_hbm.at[pl.ds(0, n_rows)]
    pltpu.make_async_copy(view, view, sem).wait()


def _dispatch_kernel(t_all, dest_ref, h_ref, rows_hbm, sem):
    tm = TM_DISPATCH
    t0 = pl.program_id(0) * tm

    def body(t, carry):
        for kq in range(TOP_K):
            pltpu.make_async_copy(h_ref.at[t], rows_hbm.at[dest_ref[kq * t_all + t0 + t]], sem).start(priority=kq % 2)
        return carry

    lax.fori_loop(0, tm, body, 0, unroll=ISSUE_UNROLL)
    _row_copy_wait(rows_hbm, TOP_K * tm, sem)


def _dispatch_call(h3, dest_flat):
    t, _, d = h3.shape
    grid_spec = pltpu.PrefetchScalarGridSpec(
        num_scalar_prefetch=1,
        grid=(t // TM_DISPATCH,),
        in_specs=[pl.BlockSpec((TM_DISPATCH, 1, d), lambda i, dr: (i, 0, 0))],
        out_specs=pl.BlockSpec(memory_space=pl.ANY),
        scratch_shapes=[pltpu.SemaphoreType.DMA(())],
    )
    return pl.pallas_call(
        functools.partial(_dispatch_kernel, t),
        grid_spec=grid_spec,
        out_shape=jax.ShapeDtypeStruct((TOP_K * t, 1, d), F32),
        compiler_params=_cparams(("arbitrary",)),
        name="moe_dispatch",
    )(dest_flat, h3)


def _moe_kernel(blk_ref, exp_ref, lo_ref, hi_ref, fblk_ref, fexp_ref, nit_ref,
                x_ref, w1_ref, b1_ref, w2_ref, b2_ref, o_ref, w1_scr, w2_scr, x_scr, acc_scr):
    del blk_ref, exp_ref
    it = pl.program_id(0)

    @pl.when(fexp_ref[it] == 1)
    def _():
        w1_scr[...] = w1_ref[...].astype(BF16)
        w2_scr[...] = w2_ref[...].astype(BF16)

    @pl.when(it < nit_ref[0])
    def _():
        x_scr[...] = x_ref[...].reshape(x_scr.shape)
        hm = _dot(x_scr[...].astype(BF16), w1_scr[...]) + b1_ref[...]
        glu = jnp.minimum(hm[:, :D_EXPERT], SWIGLU_LIMIT)
        lin = jnp.clip(hm[:, D_EXPERT:], -SWIGLU_LIMIT, SWIGLU_LIMIT)
        act = glu * _sigmoid(SWIGLU_ALPHA * glu) * (lin + 1.0)
        y = _dot(act.astype(BF16), w2_scr[...]) + b2_ref[...]
        row = lax.broadcasted_iota(jnp.int32, (y.shape[0], 1), 0)
        y = jnp.where((row >= lo_ref[it]) & (row < hi_ref[it]), y, 0.0)

        @pl.when(fblk_ref[it] == 1)
        def _():
            acc_scr[...] = y

        @pl.when(fblk_ref[it] == 0)
        def _():
            acc_scr[...] = acc_scr[...] + y

        o_ref[...] = acc_scr[...].reshape(o_ref.shape)


def _moe_call(rows, plan, layer, w1, b1, w2, b2):
    n_rows, _, d = rows.shape
    tm = TM_MOE
    depth, ne, _, dh2 = w1.shape
    de = w2.shape[2]
    n_items = plan[0].shape[0]

    def row_map(it, blk, ex, lo, hi, fb, fe, nit):
        return (blk[it], 0, 0)

    def w_map(it, blk, ex, lo, hi, fb, fe, nit):
        return (layer, ex[it], 0, 0)

    grid_spec = pltpu.PrefetchScalarGridSpec(
        num_scalar_prefetch=7,
        grid=(n_items,),
        in_specs=[
            pl.BlockSpec((tm, 1, d), row_map),
            pl.BlockSpec((None, None, d, dh2), w_map),
            pl.BlockSpec((None, None, 1, dh2), w_map),
            pl.BlockSpec((None, None, de, d), w_map),
            pl.BlockSpec((None, None, 1, d), w_map),
        ],
        out_specs=pl.BlockSpec((tm, 1, d), row_map),
        scratch_shapes=[pltpu.VMEM((d, dh2), BF16), pltpu.VMEM((de, d), BF16), pltpu.VMEM((tm, d), F32),
                        pltpu.VMEM((tm, d), F32)],
    )
    return pl.pallas_call(
        _moe_kernel,
        grid_spec=grid_spec,
        out_shape=jax.ShapeDtypeStruct((n_rows, 1, d), F32),
        compiler_params=_cparams(("arbitrary",)),
        name="moe",
    )(*plan, rows, w1, b1.reshape(depth, ne, 1, dh2), w2, b2.reshape(depth, ne, 1, d))


def _moe_plan(counts, n_rows):
    tm = TM_MOE
    ne = counts.shape[0]
    n_items = n_rows // tm + ne - 1
    counts = counts.astype(jnp.int32)
    ends = jnp.cumsum(counts)
    starts = ends - counts
    first_blk = starts // tm
    last_blk = jnp.maximum(ends - 1, 0) // tm
    per_exp = jnp.where(counts > 0, last_blk - first_blk + 1, 0)
    item_end = jnp.cumsum(per_exp)
    item_off = item_end - per_exp
    total = item_end[-1]
    it = jnp.arange(n_items, dtype=jnp.int32)
    itc = jnp.minimum(it, total - 1)
    ex = jnp.minimum(jnp.sum((itc[:, None] >= item_end[None, :]).astype(jnp.int32), axis=1), ne - 1)
    hot = ex[:, None] == jnp.arange(ne, dtype=jnp.int32)[None, :]

    def pick(v):
        return jnp.sum(jnp.where(hot, v[None, :], 0), axis=1)

    blk = pick(first_blk) + itc - pick(item_off)
    lo = jnp.maximum(pick(starts), blk * tm) - blk * tm
    hi = jnp.where(it < total, jnp.minimum(pick(ends), (blk + 1) * tm) - blk * tm, lo)
    one = jnp.ones((1,), jnp.int32)
    f_blk = jnp.concatenate([one, (blk[1:] != blk[:-1]).astype(jnp.int32)])
    f_exp = jnp.concatenate([one, (ex[1:] != ex[:-1]).astype(jnp.int32)])
    return (blk, ex, lo, hi, f_blk, f_exp, total.reshape(1)), starts


def _combine_kernel(t_all, dest_ref, y_hbm, gate_ref, x_ref, mod_ref, o_ref, buf, y_scr, sems):
    tm = TM_COMBINE
    i = pl.program_id(0)

    def start_gather(tile, slot):
        t0 = tile * tm

        def body(t, carry):
            for kq in range(TOP_K):
                pltpu.make_async_copy(y_hbm.at[dest_ref[kq * t_all + t0 + t]], buf.at[slot, kq * tm + t],
                                      sems.at[slot]).start(priority=kq % 2)
            return carry

        lax.fori_loop(0, tm, body, 0, unroll=ISSUE_UNROLL)

    @pl.when(i == 0)
    def _():
        start_gather(0, 0)

    @pl.when(i + 1 < pl.num_programs(0))
    def _():
        start_gather(i + 1, (i + 1) % 2)

    slot = i % 2
    pltpu.make_async_copy(y_hbm.at[pl.ds(0, TOP_K * tm)], buf.at[slot], sems.at[slot]).wait()
    y_scr[...] = buf[slot].reshape(y_scr.shape)
    gate = gate_ref[...]
    acc = gate[:, 0:1] * y_scr[0:tm, :]
    for kq in range(1, TOP_K):
        acc = acc + gate[:, kq:kq + 1] * y_scr[kq * tm:(kq + 1) * tm, :]
    o_ref[...] = x_ref[...] + mod_ref[5:6, :] * acc


def _combine_call(y_rows, dest_flat, gate_tk, x, mod, row_of_tile):
    t = gate_tk.shape[0]
    d = y_rows.shape[-1]
    tm = TM_COMBINE
    grid_spec = pltpu.PrefetchScalarGridSpec(
        num_scalar_prefetch=1,
        grid=(t // tm,),
        in_specs=[pl.BlockSpec(memory_space=pl.ANY), pl.BlockSpec((tm, TOP_K), lambda i, dr: (i, 0)),
                  pl.BlockSpec((tm, d), lambda i, dr: (i, 0)),
                  pl.BlockSpec((None, N_MOD, d), lambda i, dr: (row_of_tile(i, tm), 0, 0))],
        out_specs=pl.BlockSpec((tm, d), lambda i, dr: (i, 0)),
        scratch_shapes=[pltpu.VMEM((2, TOP_K * tm, 1, d), F32), pltpu.VMEM((TOP_K * tm, d), F32),
                        pltpu.SemaphoreType.DMA((2,))],
    )
    return pl.pallas_call(
        functools.partial(_combine_kernel, t),
        grid_spec=grid_spec,
        out_shape=jax.ShapeDtypeStruct((t, d), F32),
        compiler_params=_cparams(("arbitrary",)),
        name="moe_combine",
    )(dest_flat, y_rows, gate_tk, x, mod)


def _moe_layer(x_mid, mod, row_of_tile, h3, e_t, gate_t, rank_t, counts, layer, w1, b1, w2, b2):
    t = h3.shape[0]
    ne = w1.shape[1]
    plan, starts = _moe_plan(counts, TOP_K * t)
    start_of = jnp.sum(jnp.where(e_t[..., None] == jnp.arange(ne, dtype=jnp.int32), starts, 0), axis=-1)
    dest_flat = (start_of + rank_t).reshape(-1)
    rows = _dispatch_call(h3, dest_flat)
    y_rows = _moe_call(rows, plan, layer, w1, b1, w2, b2)
    return _combine_call(y_rows, dest_flat, gate_t.T, x_mid, mod, row_of_tile)


def _final_kernel(x_ref, g_ref, o_ref):
    x = x_ref[...]
    ms = jnp.mean(x * x, axis=-1, keepdims=True)
    o_ref[...] = x * lax.rsqrt(ms + RMS_EPS) * g_ref[...]


def _final_call(x, g_final, tm):
    t, d = x.shape
    spec = pl.BlockSpec((tm, d), lambda i: (i, 0))
    return pl.pallas_call(
        _final_kernel,
        grid=(t // tm,),
        in_specs=[spec, pl.BlockSpec((1, d), lambda i: (0, 0))],
        out_specs=spec,
        out_shape=jax.ShapeDtypeStruct((t, d), F32),
        compiler_params=_cparams(("arbitrary",)),
        name="final_norm",
    )(x, g_final.reshape(1, d))


def _permute_cols(w):
    o = _SRC_OFF
    return jnp.concatenate([
        w[..., o["rwkv"]:o["lora"]], w[..., o["conv"]:o["qkv"]], w[..., o["gates"]:],
        w[..., o["qkv"]:o["rwkv"]], w[..., o["lora"]:o["gates"]]], axis=-1)


def kernel(x_prompt, x_sample, cache_k, cache_v, state_rwkv_fwd, state_rwkv_bwd, c, c_ctx, w_ada, b_ada, g_norm1, g_norm2, w_in, conv_w, attn_sink, rwkv_w0, rwkv_w2, rwkv_a0, rwkv_a2, rwkv_g2, rwkv_k_k, rwkv_k_a, rwkv_r_k, rwkv_ln_g, rwkv_ln_b, w_branch_conv, w_branch_attn, w_branch_rwkv, w_out, router_w, router_b, moe_w1, moe_b1, moe_w2, moe_b2, g_final):
    bc, lc, d = x_prompt.shape
    bl, tl, _ = x_sample.shape
    depth = w_in.shape[0]
    n_ctx = bc * lc
    t_all = n_ctx + bl * tl
    tm_in = next(tm for tm in (TM_IN, TM_IN // 2) if n_ctx % tm == 0 and tl % tm == 0)
    assert lc % TM_MIX == 0 and tl % (2 * Q_BLOCK) == 0 and t_all % TM_ROUTE == 0

    def row_of_tile(i, tm):
        return jnp.where(i < n_ctx // tm, 0, 1 + (i - n_ctx // tm) // (tl // tm))

    xs = (x_prompt.reshape(n_ctx, d), x_sample.reshape(bl * tl, d))
    n_cond = -(-(1 + bl) // 8) * 8
    cond = jnp.zeros((n_cond, d), F32).at[0].set(c_ctx).at[1:1 + bl].set(c)
    mods = _ada_call(cond, w_ada, b_ada).reshape(depth, n_cond, N_MOD, d)

    w_in_p = _permute_cols(w_in).astype(BF16)
    head_id = np.arange(PAIR) // RWKV_HEAD
    ones_pair = jnp.asarray((head_id[:, None] == head_id[None, :]).astype(np.float32), BF16)
    cos_t, sin_t = _rope_tables(tl)
    zeros_state = jnp.zeros((bc, 2 * N_PAIR, RWKV_HEAD, RWKV_HEAD), F32)
    tiles = (n_ctx // TM_MIX, lc // TM_MIX, tl // TM_MIX)

    new_k, new_v, new_sf, new_sb = [], [], [], []
    for l in range(depth):
        mod = mods[l]
        lw = dict(
            w0=rwkv_w0[l], w2=rwkv_w2[l], a0=rwkv_a0[l], a2=rwkv_a2[l], g2=rwkv_g2[l],
            kk_w=rwkv_k_k[l].reshape(1, -1), ka_w=rwkv_k_a[l].reshape(1, -1), rk_w=rwkv_r_k[l].reshape(1, -1),
            ln_g=rwkv_ln_g[l].reshape(1, -1), ln_b=rwkv_ln_b[l].reshape(1, -1), conv_w=conv_w[l],
            ones_pair=ones_pair,
            wa=w_branch_conv[l].astype(BF16), wb=w_branch_attn[l].astype(BF16), wc=w_branch_rwkv[l].astype(BF16),
            wo=w_out[l].astype(BF16),
        )
        z = _in_call(xs, w_in_p, l, g_norm1[l], mod, row_of_tile, tm_in)

        kv_ctx = z[:n_ctx, COL_QKV + ATT_W:COL_QKV + ATT_W + 2 * KV_W].astype(F32)
        new_k.append(kv_ctx[:, :KV_W].reshape(bc, lc, N_KV_HEADS, HEAD_DIM))
        new_v.append(kv_ctx[:, KV_W:].reshape(bc, lc, N_KV_HEADS, HEAD_DIM))

        ya_c = _attn_ctx_call(z, attn_sink[l], bc, lc)
        ya_l = _attn_lat_call(z, attn_sink[l], cache_k[:, l].reshape(bl, -1, KV_W), cache_v[:, l].reshape(bl, -1, KV_W),
                              cos_t, sin_t, bl, tl, n_ctx)

        of_c, ob_c, zf_c, zb_c = _rwkv_call(z, lw, _state_to_z(zeros_state), _state_to_z(zeros_state), bc, lc, 0)
        of_l, ob_l, _, _ = _rwkv_call(z, lw, _state_to_z(state_rwkv_fwd[:, l].astype(F32)),
                                      _state_to_z(state_rwkv_bwd[:, l].astype(F32)), bl, tl, n_ctx)
        new_sf.append(_z_to_state(zf_c))
        new_sb.append(_z_to_state(zb_c))

        x_mid = _mix_call(xs, z, (of_c, of_l), (ob_c, ob_l), (ya_c, ya_l), mod, row_of_tile, lw, tiles)
        h2, e_t, gate_t, rank_t, cnt = _route_call(x_mid, mod, row_of_tile, g_norm2[l], router_w[l], router_b[l])
        xs = (_moe_layer(x_mid, mod, row_of_tile, h2, e_t, gate_t, rank_t, cnt[:, 0], l, moe_w1, moe_b1, moe_w2, moe_b2),)

    y = _final_call(xs[0], g_final, tm_in)
    y_prompt = y[:n_ctx].reshape(bc, lc, d)
    y_sample = y[n_ctx:].reshape(bl, tl, d)
    dt = x_prompt.dtype
    return (y_prompt, y_sample, jnp.stack(new_k, axis=1), jnp.stack(new_v, axis=1),
            jnp.stack(new_sf, axis=1).astype(dt), jnp.stack(new_sb, axis=1).astype(dt))
```

```python
import functools

import numpy as np
import jax
import jax.numpy as jnp
from jax import lax
from jax.experimental import pallas as pl
from jax.experimental.pallas import tpu as pltpu

F32 = jnp.float32
BF16 = jnp.bfloat16

D_MODEL = 1024
N_MOD = 6
RMS_EPS = 1e-6
CONV_W = 512
N_HEADS = 8
N_KV_HEADS = 2
GQA_GROUP = N_HEADS // N_KV_HEADS
HEAD_DIM = 64
ATT_W = N_HEADS * HEAD_DIM
KV_W = N_KV_HEADS * HEAD_DIM
WINDOW = 128
Q_BLOCK = 128
ATTN_SCALE = HEAD_DIM ** -0.5
ROPE_THETA = 10000.0
GRID_W = 64
RWKV_HEAD = 64
RWKV_W = 512
DECAY_LORA = 64
ICLR_LORA = 64
GATE_LORA = 128
GN_EPS = 64e-5
N_EXPERTS = 32
TOP_K = 4
D_EXPERT = 1024
SWIGLU_LIMIT = 7.0
SWIGLU_ALPHA = 1.702
P_TOTAL = 7296

_SRC_OFF = dict(conv=0, qkv=1536, rwkv=2304, lora=3840, gates=4224)
COL_RWKV, COL_CONV, COL_GATES, COL_QKV, COL_LORA = 0, 1536, 3072, 6144, 6912
LORA_W = 2 * DECAY_LORA + 2 * ICLR_LORA + GATE_LORA

CHUNK = 64
CHUNKS_PER_STEP = 2
PAIR = 2 * RWKV_HEAD
N_PAIR = RWKV_W // PAIR
NEG_BIG = -1e30
HALO = 16
EXP_NEG_HALF = float(np.exp(-0.5))

TM_IN = 1024
TN_IN = 2432
TM_MIX = 512
TM_ROUTE = 512
TM_MOE = 512
TM_DISPATCH = 512
TM_COMBINE = 256
ISSUE_UNROLL = 8
VMEM_LIMIT = 56 * 1024 * 1024


def _cparams(sem, vmem=VMEM_LIMIT):
    return pltpu.CompilerParams(dimension_semantics=sem, vmem_limit_bytes=vmem)


def _dot(a, b, dims=(((1,), (0,)), ((), ()))):
    return lax.dot_general(a, b, dims, preferred_element_type=F32)


_NT = (((1,), (1,)), ((), ()))
_TN = (((0,), (0,)), ((), ()))


def _split(x):
    hi = x.astype(BF16)
    lo = (x - hi.astype(F32)).astype(BF16)
    return hi, lo


def _dot3(a, b, dims=(((1,), (0,)), ((), ()))):
    ah, al = _split(a)
    bh, bl = _split(b)
    return _dot(ah, bh, dims) + (_dot(ah, bl, dims) + _dot(al, bh, dims))


def _head_sums(x, ones_pair):
    rows = x.shape[0]
    xs = jnp.concatenate([x[:, PAIR * p:PAIR * (p + 1)] for p in range(N_PAIR)], axis=0).astype(BF16)
    s = _dot(xs, ones_pair)
    return jnp.concatenate([s[rows * p:rows * (p + 1)] for p in range(N_PAIR)], axis=1)


def _sigmoid(x):
    return 1.0 / (1.0 + jnp.exp(-x))


def _rms_mod(x, g, scale, shift):
    ms = jnp.mean(x * x, axis=-1, keepdims=True)
    return (x * lax.rsqrt(ms + RMS_EPS) * g) * (1.0 + scale) + shift


def _ada_kernel(cond_ref, w_ref, b_ref, o_ref):
    c = cond_ref[...]
    s = c * _sigmoid(c)
    o_ref[...] = _dot(s.astype(BF16), w_ref[...].astype(BF16)) + b_ref[...]


def _ada_call(cond, w_ada, b_ada):
    depth, d, n = w_ada.shape
    r = cond.shape[0]
    tn = 1536
    return pl.pallas_call(
        _ada_kernel,
        grid=(depth, n // tn),
        in_specs=[
            pl.BlockSpec((r, d), lambda l, j: (0, 0)),
            pl.BlockSpec((None, d, tn), lambda l, j: (l, 0, j)),
            pl.BlockSpec((None, 1, tn), lambda l, j: (l, 0, j)),
        ],
        out_specs=pl.BlockSpec((None, r, tn), lambda l, j: (l, 0, j)),
        out_shape=jax.ShapeDtypeStruct((depth, r, n), F32),
        compiler_params=_cparams(("arbitrary", "arbitrary")),
        name="ada",
    )(cond, w_ada, b_ada.reshape(depth, 1, n))


def _in_kernel(n_first, *refs):
    if n_first is None:
        x_ref, mod_ref, g_ref, w_ref, z_ref = refs
        x = x_ref[...]
    else:
        xa_ref, xb_ref, mod_ref, g_ref, w_ref, z_ref = refs
        x = jnp.where(pl.program_id(1) < n_first, xa_ref[...], xb_ref[...])
    h = _rms_mod(x, g_ref[...], mod_ref[1:2, :], mod_ref[0:1, :]).astype(BF16)
    z_ref[...] = _dot(h, w_ref[...]).astype(BF16)


def _in_call(xs, w_bf16, layer, g_norm, mod, row_of_tile, tm):
    d = xs[0].shape[1]
    t = sum(x.shape[0] for x in xs)
    n = w_bf16.shape[2]
    tn = TN_IN
    if len(xs) == 1:
        n_first = None
        x_specs = [pl.BlockSpec((tm, d), lambda j, i: (i, 0))]
    else:
        n_first = xs[0].shape[0] // tm
        x_specs = [pl.BlockSpec((tm, d), lambda j, i: (jnp.minimum(i, n_first - 1), 0)),
                   pl.BlockSpec((tm, d), lambda j, i: (jnp.maximum(i - n_first, 0), 0))]
    return pl.pallas_call(
        functools.partial(_in_kernel, n_first),
        grid=(n // tn, t // tm),
        in_specs=x_specs + [
            pl.BlockSpec((None, N_MOD, d), lambda j, i: (row_of_tile(i, tm), 0, 0)),
            pl.BlockSpec((1, d), lambda j, i: (0, 0)),
            pl.BlockSpec((None, d, tn), lambda j, i: (layer, 0, j)),
        ],
        out_specs=pl.BlockSpec((tm, tn), lambda j, i: (i, j)),
        out_shape=jax.ShapeDtypeStruct((t, n), BF16),
        compiler_params=_cparams(("arbitrary", "arbitrary")),
        name="in_proj",
    )(*xs, mod, g_norm.reshape(1, d), w_bf16)


def _softmax_pv(scores, values, sink):
    m = sink
    for s in scores:
        m = jnp.maximum(m, jnp.max(s, axis=-1, keepdims=True))
    den = jnp.exp(sink - m)
    acc = None
    for s, v in zip(scores, values):
        p = jnp.exp(s - m)
        den = den + jnp.sum(p, axis=-1, keepdims=True)
        pv = _dot(p.astype(BF16), v)
        acc = pv if acc is None else acc + pv
    return acc / den


def _attn_ctx_kernel(sink_ref, q_ref, kv_ref, o_ref):
    q = (q_ref[...] * ATTN_SCALE).astype(BF16)
    kv = kv_ref[...].astype(BF16)
    outs = []
    for h in range(N_HEADS):
        g = h // GQA_GROUP
        qh = q[:, HEAD_DIM * h:HEAD_DIM * (h + 1)]
        kh = kv[:, HEAD_DIM * g:HEAD_DIM * (g + 1)]
        vh = kv[:, KV_W + HEAD_DIM * g:KV_W + HEAD_DIM * (g + 1)]
        outs.append(_softmax_pv([_dot(qh, kh, _NT)], [vh], sink_ref[h]))
    o_ref[...] = jnp.concatenate(outs, axis=-1).astype(BF16)


def _attn_ctx_call(z, sink, n_seq, seq_len):
    return pl.pallas_call(
        _attn_ctx_kernel,
        grid=(n_seq,),
        in_specs=[
            pl.BlockSpec(memory_space=pltpu.SMEM),
            pl.BlockSpec((seq_len, ATT_W), lambda s: (s, COL_QKV // ATT_W)),
            pl.BlockSpec((seq_len, 2 * KV_W), lambda s: (s, (COL_QKV + ATT_W) // (2 * KV_W))),
        ],
        out_specs=pl.BlockSpec((seq_len, ATT_W), lambda s: (s, 0)),
        out_shape=jax.ShapeDtypeStruct((n_seq * seq_len, ATT_W), BF16),
        compiler_params=_cparams(("arbitrary",)),
        name="attn_ctx",
    )(sink, z, z)


def _rope(x, cos, sin_signed):
    n = x.shape[-1]
    lane = lax.broadcasted_iota(jnp.int32, x.shape, 1)
    up = pltpu.roll(x, n - 16, 1)
    dn = pltpu.roll(x, 16, 1)
    partner = jnp.where((lane & 31) < 16, up, dn)
    return x * cos + partner * sin_signed


def _attn_lat_kernel(sink_ref, q_ref, kvp_ref, kvc_ref, kvn_ref, ck_ref, cv_ref,
                     cosc_ref, sinc_ref, cosp_ref, sinp_ref, cosn_ref, sinn_ref, o_ref):
    step = pl.program_id(1)
    n_steps = pl.num_programs(1)
    qb2 = 2 * Q_BLOCK
    cc, sc = cosc_ref[...], sinc_ref[...]
    q = _rope(q_ref[...].astype(F32), jnp.concatenate([cc] * 4, axis=1), jnp.concatenate([sc] * 4, axis=1))
    q = (q * ATTN_SCALE).astype(BF16)
    kvp, kvc, kvn = (ref[...].astype(F32) for ref in (kvp_ref, kvc_ref, kvn_ref))
    kp = _rope(kvp[:, :KV_W], cosp_ref[...], sinp_ref[...]).astype(BF16)
    kc = _rope(kvc[:, :KV_W], cc, sc).astype(BF16)
    kn = _rope(kvn[:, :KV_W], cosn_ref[...], sinn_ref[...]).astype(BF16)
    vp, vc, vn = (t[:, KV_W:].astype(BF16) for t in (kvp, kvc, kvn))
    ck = ck_ref[...].astype(BF16)
    cv = cv_ref[...].astype(BF16)
    k_cat = [jnp.concatenate([kp, kc, ck], axis=0), jnp.concatenate([kc, kn, ck], axis=0)]
    v_cat = [jnp.concatenate([vp, vc, cv], axis=0), jnp.concatenate([vc, vn, cv], axis=0)]
    nk = k_cat[0].shape[0]
    qi = lax.broadcasted_iota(jnp.int32, (Q_BLOCK, nk), 0)
    kj = lax.broadcasted_iota(jnp.int32, (Q_BLOCK, nk), 1)
    in_next = (kj >= qb2) & (kj < 3 * Q_BLOCK)
    before_ok = (step > 0, True)
    after_ok = (True, step < n_steps - 1)
    bias = []
    for u in range(2):
        ok_p = (kj >= qi) & before_ok[u]
        ok_n = (kj - qb2 <= qi) & after_ok[u]
        bu = jnp.where(kj < Q_BLOCK, jnp.where(ok_p, 0.0, NEG_BIG), jnp.where(in_next, jnp.where(ok_n, 0.0, NEG_BIG), 0.0))
        bias.append(jnp.concatenate([bu] * GQA_GROUP, axis=0))
    chains = [(u, g) for u in range(2) for g in range(N_KV_HEADS)]
    heads = {g: range(GQA_GROUP * g, GQA_GROUP * (g + 1)) for g in range(N_KV_HEADS)}
    gsl = {g: slice(HEAD_DIM * g, HEAD_DIM * (g + 1)) for g in range(N_KV_HEADS)}
    q_g = {(u, g): jnp.concatenate([q[Q_BLOCK * u:Q_BLOCK * (u + 1), HEAD_DIM * h:HEAD_DIM * (h + 1)] for h in heads[g]], axis=0)
           for u, g in chains}
    sink = {g: jnp.concatenate([jnp.full((Q_BLOCK, 1), sink_ref[h], F32) for h in heads[g]], axis=0) for g in range(N_KV_HEADS)}
    s = {ch: _dot(q_g[ch], k_cat[ch[0]][:, gsl[ch[1]]], _NT) + bias[ch[0]] for ch in chains}
    m = {ch: jnp.maximum(jnp.max(s[ch], axis=-1, keepdims=True), sink[ch[1]]) for ch in chains}
    p = {ch: jnp.exp(s[ch] - m[ch]) for ch in chains}
    den = {ch: jnp.sum(p[ch], axis=-1, keepdims=True) + jnp.exp(sink[ch[1]] - m[ch]) for ch in chains}
    o = {ch: _dot(p[ch].astype(BF16), v_cat[ch[0]][:, gsl[ch[1]]]) / den[ch] for ch in chains}
    for u in range(2):
        outs = [o[u, g][Q_BLOCK * i:Q_BLOCK * (i + 1)] for g in range(N_KV_HEADS) for i in range(GQA_GROUP)]
        o_ref[Q_BLOCK * u:Q_BLOCK * (u + 1), :] = jnp.concatenate(outs, axis=-1).astype(BF16)


def _attn_lat_call(z, sink, cache_k, cache_v, cos_t, sin_t, n_seq, seq_len, row0):
    nb = seq_len // Q_BLOCK
    ns = nb // 2
    base = row0 // Q_BLOCK
    past = cache_k.shape[1]
    kv_col = (COL_QKV + ATT_W) // (2 * KV_W)

    def kv_spec(off):
        return pl.BlockSpec((Q_BLOCK, 2 * KV_W), lambda b, s: (base + b * nb + jnp.clip(2 * s + off, 0, nb - 1), kv_col))

    def tab_spec(off):
        return pl.BlockSpec((Q_BLOCK, KV_W), lambda b, s: (jnp.clip(2 * s + off, 0, nb - 1), 0))

    return pl.pallas_call(
        _attn_lat_kernel,
        grid=(n_seq, ns),
        in_specs=[
            pl.BlockSpec(memory_space=pltpu.SMEM),
            pl.BlockSpec((2 * Q_BLOCK, ATT_W), lambda b, s: (base // 2 + b * ns + s, COL_QKV // ATT_W)),
            kv_spec(-1),
            pl.BlockSpec((2 * Q_BLOCK, 2 * KV_W), lambda b, s: (base // 2 + b * ns + s, kv_col)),
            kv_spec(2),
            pl.BlockSpec((None, past, KV_W), lambda b, s: (b, 0, 0)),
            pl.BlockSpec((None, past, KV_W), lambda b, s: (b, 0, 0)),
            pl.BlockSpec((2 * Q_BLOCK, KV_W), lambda b, s: (s, 0)), pl.BlockSpec((2 * Q_BLOCK, KV_W), lambda b, s: (s, 0)),
            tab_spec(-1), tab_spec(-1), tab_spec(2), tab_spec(2),
        ],
        out_specs=pl.BlockSpec((2 * Q_BLOCK, ATT_W), lambda b, s: (b * ns + s, 0)),
        out_shape=jax.ShapeDtypeStruct((n_seq * seq_len, ATT_W), BF16),
        compiler_params=_cparams(("arbitrary", "arbitrary")),
        name="attn_lat",
    )(sink, z, z, z, z, cache_k, cache_v, cos_t, sin_t, cos_t, sin_t, cos_t, sin_t)


def _rope_tables(seq_len):
    half = HEAD_DIM // 2
    pos = np.arange(seq_len)
    inv_freq = 1.0 / (ROPE_THETA ** (np.arange(0, half, 2, dtype=np.float32) / half))
    inv_freq = inv_freq.astype(np.float32)

    def part(p):
        ang = (p.astype(np.float32)[:, None] * inv_freq[None, :]).astype(np.float32)
        c, s = np.cos(ang), np.sin(ang)
        return np.concatenate([c, c], axis=1), np.concatenate([-s, s], axis=1)

    c_r, s_r = part(pos // GRID_W)
    c_c, s_c = part(pos % GRID_W)
    cos = np.concatenate([c_r, c_c] * N_KV_HEADS, axis=1).astype(np.float32)
    sin = np.concatenate([s_r, s_c] * N_KV_HEADS, axis=1).astype(np.float32)
    return jnp.asarray(cos), jnp.asarray(sin)


def _rwkv_dir_inputs(zm, zl, d, w0, w2, a0, a2, kk_w, ka_w, ones_pair):
    r = zm[:, :RWKV_W]
    kraw = zm[:, RWKV_W:2 * RWKV_W]
    v = zm[:, 2 * RWKV_W:]
    wl = zl[:, DECAY_LORA * d:DECAY_LORA * (d + 1)]
    al = zl[:, 2 * DECAY_LORA + ICLR_LORA * d:2 * DECAY_LORA + ICLR_LORA * (d + 1)]
    xw = w0 + _dot(jnp.tanh(wl).astype(BF16), w2.astype(BF16))
    ld = -EXP_NEG_HALF * _sigmoid(xw)
    a = _sigmoid(a0 + _dot(al.astype(BF16), a2.astype(BF16)))
    k = kraw * (1.0 + (a - 1.0) * ka_w)
    kkr = kraw * kk_w
    n2 = _head_sums(kkr * kkr, ones_pair)
    kk = kkr / jnp.maximum(jnp.sqrt(n2), 1e-12)
    return r, v, kk, ld, a, k


def _tri_masks(rev):
    c = CHUNK
    ti = lax.broadcasted_iota(jnp.int32, (c, c), 0)
    si = lax.broadcasted_iota(jnp.int32, (c, c), 1)
    incl = (si >= ti) if rev else (si <= ti)
    strict = (si > ti) if rev else (si < ti)
    return incl, strict, (si == ti).astype(F32)


def _chunk_prepare(r, v, kk, ld, a, k, rev):
    incl, _, _ = _tri_masks(rev)
    m_incl = jnp.where(incl, 1.0, 0.0).astype(BF16)
    ldh, ldl = _split(ld)
    cin = _dot(m_incl, ldh) + _dot(m_incl, ldl)
    tot = jnp.sum(ld, axis=0, keepdims=True)
    e_neg = jnp.exp(-cin)
    e_end = jnp.exp(tot - cin)
    bb = kk * a
    return dict(
        a_m=kk * jnp.exp(cin - ld), r_m=r * jnp.exp(cin),
        b_m=(bb * e_neg).astype(BF16), k_m=(k * e_neg).astype(BF16),
        b_end=(bb * e_end).astype(BF16), k_end=(k * e_end).astype(BF16),
        v=v, e_tot=jnp.exp(tot), rev=rev)


def _chunk_problems(probs):
    c = CHUNK
    n = len(probs)
    rng = range(n)
    lane = lax.broadcasted_iota(jnp.int32, (1, PAIR), 1)
    first_head = lane < RWKV_HEAD

    def bd(x):
        xb = x.astype(BF16)
        zero = jnp.zeros_like(xb)
        return jnp.concatenate([jnp.where(first_head, xb, zero), jnp.where(first_head, zero, xb)], axis=0)

    ti = lax.broadcasted_iota(jnp.int32, (c, PAIR), 0)
    si = lax.broadcasted_iota(jnp.int32, (c, PAIR), 1) & (RWKV_HEAD - 1)
    incl = {False: si <= ti, True: si >= ti}
    strict = {False: si < ti, True: si > ti}
    eye = (si == ti).astype(F32)
    rev = [p["rev"] for p in probs]

    lhs = [jnp.concatenate([probs[i]["a_m"], probs[i]["r_m"]], axis=0).astype(BF16) for i in rng]
    xb = [_dot(lhs[i], bd(probs[i]["b_m"]), _NT) for i in rng]
    xk = [_dot(lhs[i], bd(probs[i]["k_m"]), _NT) for i in rng]
    m_ab = [jnp.where(strict[rev[i]], xb[i][:c], 0.0) for i in rng]
    m_ak = [jnp.where(strict[rev[i]], xk[i][:c], 0.0).astype(BF16) for i in rng]
    m_rb = [jnp.where(incl[rev[i]], xb[i][c:], 0.0).astype(BF16) for i in rng]
    m_rk = [jnp.where(incl[rev[i]], xk[i][c:], 0.0).astype(BF16) for i in rng]
    v_bd = [bd(probs[i]["v"]) for i in rng]
    mv = [_dot(jnp.concatenate([m_ak[i], m_rk[i]], axis=0), v_bd[i]) for i in rng]
    mak_v = [mv[i][:c] for i in rng]
    mrk_v = [mv[i][c:] for i in rng]
    t_inv = [eye - m_ab[i] for i in rng]
    lp = [_dot(m_ab[i].astype(BF16), bd(m_ab[i])) for i in rng]
    for _ in range(4):
        both = [_dot(jnp.concatenate([lp[i], t_inv[i]], axis=0).astype(BF16), bd(lp[i])) for i in rng]
        t_inv = [t_inv[i] + both[i][c:] for i in rng]
        lp = [both[i][:c] for i in rng]
    t_inv = [t_inv[i] + _dot(t_inv[i].astype(BF16), bd(lp[i])) for i in rng]
    wu = [_dot(t_inv[i].astype(BF16), jnp.concatenate([bd(probs[i]["a_m"]), bd(mak_v[i])], axis=1)) for i in rng]
    ro = [_dot(m_rb[i], jnp.concatenate([bd(wu[i][:, :PAIR]), bd(wu[i][:, PAIR:])], axis=1)) for i in rng]
    ri = lax.broadcasted_iota(jnp.int32, (PAIR, PAIR), 0)
    ci = lax.broadcasted_iota(jnp.int32, (PAIR, PAIR), 1)
    same_head = (ri < RWKV_HEAD) == (ci < RWKV_HEAD)
    bt_wu = [_dot(probs[i]["b_end"], wu[i].astype(BF16), _TN) for i in rng]
    kt_v = [_dot(probs[i]["k_end"], probs[i]["v"].astype(BF16), _TN) for i in rng]
    local = []
    for i in rng:
        r_eff = probs[i]["r_m"] - ro[i][:, :PAIR]
        o_loc = mrk_v[i] - ro[i][:, PAIR:]
        g_t = jnp.where(same_head, jnp.where(ri == ci, probs[i]["e_tot"], 0.0) - bt_wu[i][:, :PAIR], 0.0)
        h_t = jnp.where(same_head, kt_v[i] - bt_wu[i][:, PAIR:], 0.0)
        local.append((_split(jnp.concatenate([r_eff, g_t], axis=0)), o_loc, h_t))
    return local


def _chunk_apply(local, z):
    (bh, bl), o_loc, h_t = local
    zh, zl = _split(z)
    prod = _dot(bh, zh) + (_dot(bh, zl) + _dot(bl, zh))
    return prod[:CHUNK] + o_loc, prod[CHUNK:] + h_t


def _rwkv_kernel(zmf_ref, zlf_ref, zmb_ref, zlb_ref, w0_ref, w2_ref, a0_ref, a2_ref, kkw_ref, kaw_ref,
                 ones_ref, s0f_ref, s0b_ref, of_ref, ob_ref, zf_ref, zb_ref, zf_scr, zb_scr):
    @pl.when(pl.program_id(1) == 0)
    def _():
        zf_scr[...] = s0f_ref[...]
        zb_scr[...] = s0b_ref[...]

    ones_pair = ones_ref[...]
    dirs = ((zmf_ref, zlf_ref, zf_scr, of_ref), (zmb_ref, zlb_ref, zb_scr, ob_ref))
    n_sub = zmf_ref.shape[0] // CHUNK
    probs = []
    for d, (zm_ref, zl_ref, _, _) in enumerate(dirs):
        r, v, kk, ld, a, k = _rwkv_dir_inputs(zm_ref[...].astype(F32), zl_ref[...].astype(F32), d, w0_ref[d:d + 1, :], w2_ref[d],
                                              a0_ref[d:d + 1, :], a2_ref[d], kkw_ref[...], kaw_ref[...], ones_pair)
        for sub in range(n_sub):
            rows = slice(CHUNK * sub, CHUNK * (sub + 1))
            full = _chunk_prepare(r[rows], v[rows], kk[rows], ld[rows], a[rows], k[rows], d == 1)
            for p in range(N_PAIR):
                ps = slice(PAIR * p, PAIR * (p + 1))
                probs.append({key: (val[:, ps] if hasattr(val, "shape") else val) for key, val in full.items()})
    local = _chunk_problems(probs)
    for d, (_, _, z_scr, o_ref) in enumerate(dirs):
        order = range(n_sub) if d == 0 else range(n_sub - 1, -1, -1)
        for p in range(N_PAIR):
            z = z_scr[p]
            for sub in order:
                out, z = _chunk_apply(local[(d * n_sub + sub) * N_PAIR + p], z)
                o_ref[CHUNK * sub:CHUNK * (sub + 1), PAIR * p:PAIR * (p + 1)] = out
            z_scr[p] = z
    zf_ref[...] = zf_scr[...]
    zb_ref[...] = zb_scr[...]


def _rwkv_call(z, lw, s0f, s0b, n_seq, seq_len, row0):
    rows = CHUNK * CHUNKS_PER_STEP
    nc = seq_len // rows
    base = row0 // rows

    def fwd(s, c):
        return base + s * nc + c

    def bwd(s, c):
        return base + s * nc + (nc - 1 - c)

    def zm_spec(f):
        return pl.BlockSpec((rows, 3 * RWKV_W), lambda s, c: (f(s, c), COL_RWKV // (3 * RWKV_W)))

    def zl_spec(f):
        return pl.BlockSpec((rows, LORA_W), lambda s, c: (f(s, c), COL_LORA // LORA_W))

    def full(shape):
        return pl.BlockSpec(shape, lambda s, c: (0,) * len(shape))

    st_spec = pl.BlockSpec((None, N_PAIR, PAIR, PAIR), lambda s, c: (s, 0, 0, 0))
    o_shape = jax.ShapeDtypeStruct((n_seq * seq_len, RWKV_W), F32)
    st_shape = jax.ShapeDtypeStruct((n_seq, N_PAIR, PAIR, PAIR), F32)
    return pl.pallas_call(
        _rwkv_kernel,
        grid=(n_seq, nc),
        in_specs=[
            zm_spec(fwd), zl_spec(fwd), zm_spec(bwd), zl_spec(bwd),
            full((2, RWKV_W)), full((2, DECAY_LORA, RWKV_W)), full((2, RWKV_W)), full((2, ICLR_LORA, RWKV_W)),
            full((1, RWKV_W)), full((1, RWKV_W)), full((PAIR, PAIR)),
            st_spec, st_spec,
        ],
        out_specs=[
            pl.BlockSpec((rows, RWKV_W), lambda s, c: (s * nc + c, 0)),
            pl.BlockSpec((rows, RWKV_W), lambda s, c: (s * nc + (nc - 1 - c), 0)),
            st_spec, st_spec,
        ],
        out_shape=[o_shape, o_shape, st_shape, st_shape],
        scratch_shapes=[pltpu.VMEM((N_PAIR, PAIR, PAIR), F32), pltpu.VMEM((N_PAIR, PAIR, PAIR), F32)],
        compiler_params=_cparams(("arbitrary", "arbitrary")),
        name="rwkv_scan",
    )(z, z, z, z, lw["w0"], lw["w2"], lw["a0"], lw["a2"], lw["kk_w"], lw["ka_w"], lw["ones_pair"], s0f, s0b)


def _state_to_z(s):
    n = s.shape[0]
    st = jnp.swapaxes(s, -1, -2).reshape(n, N_PAIR, 2, RWKV_HEAD, RWKV_HEAD)
    zero = jnp.zeros_like(st[:, :, 0])
    top = jnp.concatenate([st[:, :, 0], zero], axis=-1)
    bot = jnp.concatenate([zero, st[:, :, 1]], axis=-1)
    return jnp.concatenate([top, bot], axis=-2)


def _z_to_state(z):
    n = z.shape[0]
    h0 = z[:, :, :RWKV_HEAD, :RWKV_HEAD]
    h1 = z[:, :, RWKV_HEAD:, RWKV_HEAD:]
    st = jnp.stack([h0, h1], axis=2).reshape(n, 2 * N_PAIR, RWKV_HEAD, RWKV_HEAD)
    return jnp.swapaxes(st, -1, -2)


def _mix_kernel(tiles, n_x, *refs):
    n_ctx_tiles, len_ctx, len_lat = tiles
    i = pl.program_id(0)
    in_ctx = i < n_ctx_tiles

    def pick(pair):
        return jnp.where(in_ctx, pair[0][...], pair[1][...])

    x_refs, refs = refs[:n_x], refs[n_x:]
    (zc_ref, zcp_ref, zcn_ref, zm_ref, zl_ref, zg_ref, ofc_ref, ofl_ref, obc_ref, obl_ref, yac_ref, yal_ref, mod_ref,
     convw_ref, a0_ref, a2_ref, g2_ref, kaw_ref, rkw_ref, lng_ref, lnb_ref, ones_ref,
     wa_ref, wb_ref, wc_ref, wo_ref, o_ref) = refs
    x_in = x_refs[0][...] if n_x == 1 else pick(x_refs)
    tm = o_ref.shape[0]

    zc = zc_ref[...].astype(F32)
    u = zc[:, 2 * CONV_W:] * zc[:, :CONV_W]
    zp = zcp_ref[...].astype(F32)
    zn = zcn_ref[...].astype(F32)
    row = lax.broadcasted_iota(jnp.int32, (tm, 1), 0)
    pos = jnp.where(in_ctx, lax.rem(i * tm + row, len_ctx), lax.rem((i - n_ctx_tiles) * tm + row, len_lat))
    seq_len = jnp.where(in_ctx, len_ctx, len_lat)
    u_prev = jnp.where(row == 0, zp[HALO - 1:HALO, 2 * CONV_W:] * zp[HALO - 1:HALO, :CONV_W], pltpu.roll(u, 1, 0))
    u_next = jnp.where(row == tm - 1, zn[0:1, 2 * CONV_W:] * zn[0:1, :CONV_W], pltpu.roll(u, tm - 1, 0))
    u_prev = jnp.where(pos == 0, 0.0, u_prev)
    u_next = jnp.where(pos == seq_len - 1, 0.0, u_next)
    cw = convw_ref[...]
    y_conv = zc[:, CONV_W:2 * CONV_W] * (cw[0:1, :] * u_prev + cw[1:2, :] * u + cw[2:3, :] * u_next)

    zm = zm_ref[...].astype(F32)
    zl = zl_ref[...].astype(F32)
    r = zm[:, :RWKV_W]
    kraw = zm[:, RWKV_W:2 * RWKV_W]
    v = zm[:, 2 * RWKV_W:]
    ones_pair = ones_ref[...]
    o = pick((ofc_ref, ofl_ref)) + pick((obc_ref, obl_ref))
    mu = _head_sums(o, ones_pair) * (1.0 / RWKV_HEAD)
    dlt = o - mu
    var = _head_sums(dlt * dlt, ones_pair) * (1.0 / RWKV_HEAD)
    y = dlt * lax.rsqrt(var + GN_EPS) * lng_ref[...] + lnb_ref[...]
    for d in range(2):
        al = zl[:, 2 * DECAY_LORA + ICLR_LORA * d:2 * DECAY_LORA + ICLR_LORA * (d + 1)]
        a = _sigmoid(a0_ref[d:d + 1, :] + _dot(al.astype(BF16), a2_ref[d].astype(BF16)))
        k = kraw * (1.0 + (a - 1.0) * kaw_ref[...])
        y = y + _head_sums(r * k * rkw_ref[...], ones_pair) * v
    g1 = zl[:, 2 * DECAY_LORA + 2 * ICLR_LORA:]
    y_rwkv = y * _dot(_sigmoid(g1).astype(BF16), g2_ref[...].astype(BF16))

    zg = zg_ref[...].astype(F32)
    merged = (_sigmoid(zg[:, :D_MODEL]) * _dot(y_conv.astype(BF16), wa_ref[...])
              + _sigmoid(zg[:, D_MODEL:2 * D_MODEL]) * _dot(pick((yac_ref, yal_ref)), wb_ref[...])
              + _sigmoid(zg[:, 2 * D_MODEL:]) * _dot(y_rwkv.astype(BF16), wc_ref[...]))
    o_ref[...] = x_in + mod_ref[2:3, :] * _dot(merged.astype(BF16), wo_ref[...])


def _mix_call(xs, z, o_f, o_b, y_attn, mod, row_of_tile, lw, tiles):
    t, d = z.shape[0], xs[0].shape[1]
    tm = TM_MIX
    nt = t // tm
    hb = tm // HALO
    n_ctx_tiles = tiles[0]

    def rows(w, col):
        return pl.BlockSpec((tm, w), lambda i: (i, col // w))

    def pair(w):
        return [pl.BlockSpec((tm, w), lambda i: (jnp.minimum(i, n_ctx_tiles - 1), 0)),
                pl.BlockSpec((tm, w), lambda i: (jnp.maximum(i - n_ctx_tiles, 0), 0))]

    def full(shape):
        return pl.BlockSpec(shape, lambda i: (0,) * len(shape))

    in_specs = (pair(d) if len(xs) == 2 else [rows(d, 0)]) + [
        rows(3 * CONV_W, COL_CONV),
        pl.BlockSpec((HALO, 3 * CONV_W), lambda i: (jnp.maximum(i * hb - 1, 0), COL_CONV // (3 * CONV_W))),
        pl.BlockSpec((HALO, 3 * CONV_W), lambda i: (jnp.minimum((i + 1) * hb, nt * hb - 1), COL_CONV // (3 * CONV_W))),
        rows(3 * RWKV_W, COL_RWKV),
        rows(LORA_W, COL_LORA),
        rows(3 * D_MODEL, COL_GATES),
    ] + pair(RWKV_W) + pair(RWKV_W) + pair(ATT_W) + [
        pl.BlockSpec((None, N_MOD, d), lambda i: (row_of_tile(i, tm), 0, 0)),
        full((3, CONV_W)), full((2, RWKV_W)), full((2, ICLR_LORA, RWKV_W)), full((GATE_LORA, RWKV_W)),
        full((1, RWKV_W)), full((1, RWKV_W)), full((1, RWKV_W)), full((1, RWKV_W)),
        full((PAIR, PAIR)),
        full((CONV_W, d)), full((ATT_W, d)), full((RWKV_W, d)), full((d, d)),
    ]
    return pl.pallas_call(
        functools.partial(_mix_kernel, tiles, len(xs)),
        grid=(nt,),
        in_specs=in_specs,
        out_specs=rows(d, 0),
        out_shape=jax.ShapeDtypeStruct((t, d), F32),
        compiler_params=_cparams(("arbitrary",)),
        name="mix",
    )(*xs, z, z, z, z, z, z, *o_f, *o_b, *y_attn, mod,
      lw["conv_w"], lw["a0"], lw["a2"], lw["g2"], lw["ka_w"], lw["rk_w"], lw["ln_g"], lw["ln_b"],
      lw["ones_pair"], lw["wa"], lw["wb"], lw["wc"], lw["wo"])


def _route_kernel(x_ref, mod_ref, g_ref, rwt_ref, rb_ref, tri_ref, h_ref, e_ref, gate_ref, rank_ref, cnt_ref, cnt_scr):
    @pl.when(pl.program_id(0) == 0)
    def _():
        cnt_scr[...] = jnp.zeros_like(cnt_scr)

    h2 = _rms_mod(x_ref[...], g_ref[...], mod_ref[4:5, :], mod_ref[3:4, :])
    h_ref[...] = h2.reshape(h_ref.shape)
    logits = _dot3(rwt_ref[...], h2, _NT) + rb_ref[...]
    ne, tm = logits.shape
    ex = lax.broadcasted_iota(jnp.int32, (ne, tm), 0)
    work = logits
    vals, hots = [], []
    for kq in range(TOP_K):
        m = jnp.max(work, axis=0, keepdims=True)
        idx = jnp.min(jnp.where(work == m, ex, ne), axis=0, keepdims=True)
        hot = ex == idx
        vals.append(m)
        hots.append(hot)
        e_ref[kq:kq + 1, :] = idx
        work = jnp.where(hot, -jnp.inf, work)
    exps = [jnp.exp(vk - vals[0]) for vk in vals]
    den = exps[0] + exps[1] + exps[2] + exps[3]
    chosen = jnp.where(hots[0] | hots[1] | hots[2] | hots[3], 1.0, 0.0)
    before = cnt_scr[:, 0:1] + _dot(chosen.astype(BF16), tri_ref[...])
    for kq in range(TOP_K):
        gate_ref[kq:kq + 1, :] = exps[kq] / den
        rank_ref[kq:kq + 1, :] = jnp.sum(jnp.where(hots[kq], before, 0.0), axis=0, keepdims=True).astype(jnp.int32)
    cnt_scr[...] = cnt_scr[...] + jnp.sum(chosen, axis=1, keepdims=True)
    cnt_ref[...] = cnt_scr[...].astype(jnp.int32)


def _route_call(x, mod, row_of_tile, g_norm, router_w, router_b):
    t, d = x.shape
    tm = TM_ROUTE
    ne = router_w.shape[1]
    tri = jnp.triu(jnp.ones((tm, tm), F32), 1).astype(BF16)

    def full(shape):
        return pl.BlockSpec(shape, lambda i: (0,) * len(shape))

    kt_spec = pl.BlockSpec((TOP_K, tm), lambda i: (0, i))
    return pl.pallas_call(
        _route_kernel,
        grid=(t // tm,),
        in_specs=[
            pl.BlockSpec((tm, d), lambda i: (i, 0)),
            pl.BlockSpec((None, N_MOD, d), lambda i: (row_of_tile(i, tm), 0, 0)),
            full((1, d)), full((ne, d)), full((ne, 1)), full((tm, tm)),
        ],
        out_specs=[pl.BlockSpec((tm, 1, d), lambda i: (i, 0, 0)), kt_spec, kt_spec, kt_spec, full((ne, 128))],
        out_shape=[
            jax.ShapeDtypeStruct((t, 1, d), F32),
            jax.ShapeDtypeStruct((TOP_K, t), jnp.int32),
            jax.ShapeDtypeStruct((TOP_K, t), F32),
            jax.ShapeDtypeStruct((TOP_K, t), jnp.int32),
            jax.ShapeDtypeStruct((ne, 128), jnp.int32),
        ],
        scratch_shapes=[pltpu.VMEM((ne, 128), F32)],
        compiler_params=_cparams(("arbitrary",)),
        name="route",
    )(x, mod, g_norm.reshape(1, d), router_w.T, router_b.reshape(ne, 1), tri)


def _row_copy_wait(buf_hbm, n_rows, sem):
    view = buf_hbm.at[pl.ds(0, n_rows)]
    pltpu.make_async_copy(view, view, sem).wait()


def _dispatch_kernel(t_all, dest_ref, h_ref, rows_hbm, sem):
    tm = TM_DISPATCH
    t0 = pl.program_id(0) * tm

    def body(t, carry):
        for kq in range(TOP_K):
            pltpu.make_async_copy(h_ref.at[t], rows_hbm.at[dest_ref[kq * t_all + t0 + t]], sem).start(priority=kq % 2)
        return carry

    lax.fori_loop(0, tm, body, 0, unroll=ISSUE_UNROLL)
    _row_copy_wait(rows_hbm, TOP_K * tm, sem)


def _dispatch_call(h3, dest_flat):
    t, _, d = h3.shape
    grid_spec = pltpu.PrefetchScalarGridSpec(
        num_scalar_prefetch=1,
        grid=(t // TM_DISPATCH,),
        in_specs=[pl.BlockSpec((TM_DISPATCH, 1, d), lambda i, dr: (i, 0, 0))],
        out_specs=pl.BlockSpec(memory_space=pl.ANY),
        scratch_shapes=[pltpu.SemaphoreType.DMA(())],
    )
    return pl.pallas_call(
        functools.partial(_dispatch_kernel, t),
        grid_spec=grid_spec,
        out_shape=jax.ShapeDtypeStruct((TOP_K * t, 1, d), F32),
        compiler_params=_cparams(("arbitrary",)),
        name="moe_dispatch",
    )(dest_flat, h3)


def _moe_kernel(blk_ref, exp_ref, lo_ref, hi_ref, fblk_ref, fexp_ref, nit_ref,
                x_ref, w1_ref, b1_ref, w2_ref, b2_ref, o_ref, w1_scr, w2_scr, x_scr, acc_scr):
    del blk_ref, exp_ref
    it = pl.program_id(0)

    @pl.when(fexp_ref[it] == 1)
    def _():
        w1_scr[...] = w1_ref[...].astype(BF16)
        w2_scr[...] = w2_ref[...].astype(BF16)

    @pl.when(it < nit_ref[0])
    def _():
        x_scr[...] = x_ref[...].reshape(x_scr.shape)
        hm = _dot(x_scr[...].astype(BF16), w1_scr[...]) + b1_ref[...]
        glu = jnp.minimum(hm[:, :D_EXPERT], SWIGLU_LIMIT)
        lin = jnp.clip(hm[:, D_EXPERT:], -SWIGLU_LIMIT, SWIGLU_LIMIT)
        act = glu * _sigmoid(SWIGLU_ALPHA * glu) * (lin + 1.0)
        y = _dot(act.astype(BF16), w2_scr[...]) + b2_ref[...]
        row = lax.broadcasted_iota(jnp.int32, (y.shape[0], 1), 0)
        y = jnp.where((row >= lo_ref[it]) & (row < hi_ref[it]), y, 0.0)

        @pl.when(fblk_ref[it] == 1)
        def _():
            o_ref[...] = y.reshape(o_ref.shape)

        @pl.when(fblk_ref[it] == 0)
        def _():
            acc_scr[...] = o_ref[...].reshape(acc_scr.shape)
            o_ref[...] = (acc_scr[...] + y).reshape(o_ref.shape)


def _moe_call(rows, plan, layer, w1, b1, w2, b2):
    n_rows, _, d = rows.shape
    tm = TM_MOE
    depth, ne, _, dh2 = w1.shape
    de = w2.shape[2]
    n_items = plan[0].shape[0]

    def row_map(it, blk, ex, lo, hi, fb, fe, nit):
        return (blk[it], 0, 0)

    def w_map(it, blk, ex, lo, hi, fb, fe, nit):
        return (layer, ex[it], 0, 0)

    grid_spec = pltpu.PrefetchScalarGridSpec(
        num_scalar_prefetch=7,
        grid=(n_items,),
        in_specs=[
            pl.BlockSpec((tm, 1, d), row_map),
            pl.BlockSpec((None, None, d, dh2), w_map),
            pl.BlockSpec((None, None, 1, dh2), w_map),
            pl.BlockSpec((None, None, de, d), w_map),
            pl.BlockSpec((None, None, 1, d), w_map),
        ],
        out_specs=pl.BlockSpec((tm, 1, d), row_map),
        scratch_shapes=[pltpu.VMEM((d, dh2), BF16), pltpu.VMEM((de, d), BF16), pltpu.VMEM((tm, d), F32),
                        pltpu.VMEM((tm, d), F32)],
    )
    return pl.pallas_call(
        _moe_kernel,
        grid_spec=grid_spec,
        out_shape=jax.ShapeDtypeStruct((n_rows, 1, d), F32),
        compiler_params=_cparams(("arbitrary",)),
        name="moe",
    )(*plan, rows, w1, b1.reshape(depth, ne, 1, dh2), w2, b2.reshape(depth, ne, 1, d))


def _moe_plan(counts, n_rows):
    tm = TM_MOE
    ne = counts.shape[0]
    n_items = n_rows // tm + ne - 1
    counts = counts.astype(jnp.int32)
    ends = jnp.cumsum(counts)
    starts = ends - counts
    first_blk = starts // tm
    last_blk = jnp.maximum(ends - 1, 0) // tm
    per_exp = jnp.where(counts > 0, last_blk - first_blk + 1, 0)
    item_end = jnp.cumsum(per_exp)
    item_off = item_end - per_exp
    total = item_end[-1]
    it = jnp.arange(n_items, dtype=jnp.int32)
    itc = jnp.minimum(it, total - 1)
    ex = jnp.minimum(jnp.sum((itc[:, None] >= item_end[None, :]).astype(jnp.int32), axis=1), ne - 1)
    hot = ex[:, None] == jnp.arange(ne, dtype=jnp.int32)[None, :]

    def pick(v):
        return jnp.sum(jnp.where(hot, v[None, :], 0), axis=1)

    blk = pick(first_blk) + itc - pick(item_off)
    lo = jnp.maximum(pick(starts), blk * tm) - blk * tm
    hi = jnp.where(it < total, jnp.minimum(pick(ends), (blk + 1) * tm) - blk * tm, lo)
    one = jnp.ones((1,), jnp.int32)
    f_blk = jnp.concatenate([one, (blk[1:] != blk[:-1]).astype(jnp.int32)])
    f_exp = jnp.concatenate([one, (ex[1:] != ex[:-1]).astype(jnp.int32)])
    return (blk, ex, lo, hi, f_blk, f_exp, total.reshape(1)), starts


def _combine_kernel(t_all, has_norm, dest_ref, y_hbm, gate_ref, x_ref, mod_ref, *rest):
    g_ref = rest[0] if has_norm else None
    o_ref, buf, y_scr, sems = rest[-4:]
    tm = TM_COMBINE
    i = pl.program_id(0)

    def start_gather(tile, slot):
        t0 = tile * tm

        def body(t, carry):
            for kq in range(TOP_K):
                pltpu.make_async_copy(y_hbm.at[dest_ref[kq * t_all + t0 + t]], buf.at[slot, kq * tm + t],
                                      sems.at[slot]).start(priority=kq % 2)
            return carry

        lax.fori_loop(0, tm, body, 0, unroll=ISSUE_UNROLL)

    @pl.when(i == 0)
    def _():
        start_gather(0, 0)

    @pl.when(i + 1 < pl.num_programs(0))
    def _():
        start_gather(i + 1, (i + 1) % 2)

    slot = i % 2
    pltpu.make_async_copy(y_hbm.at[pl.ds(0, TOP_K * tm)], buf.at[slot], sems.at[slot]).wait()
    y_scr[...] = buf[slot].reshape(y_scr.shape)
    gate = gate_ref[...]
    acc = gate[:, 0:1] * y_scr[0:tm, :]
    for kq in range(1, TOP_K):
        acc = acc + gate[:, kq:kq + 1] * y_scr[kq * tm:(kq + 1) * tm, :]
    x = x_ref[...] + mod_ref[5:6, :] * acc
    if has_norm:
        x = x * lax.rsqrt(jnp.mean(x * x, axis=-1, keepdims=True) + RMS_EPS) * g_ref[...]
    o_ref[...] = x


def _combine_call(y_rows, dest_flat, gate_tk, x, mod, row_of_tile, g_final=None):
    t = gate_tk.shape[0]
    d = y_rows.shape[-1]
    tm = TM_COMBINE
    norm_specs, norm_args = [], []
    if g_final is not None:
        norm_specs, norm_args = [pl.BlockSpec((1, d), lambda i, dr: (0, 0))], [g_final.reshape(1, d)]
    grid_spec = pltpu.PrefetchScalarGridSpec(
        num_scalar_prefetch=1,
        grid=(t // tm,),
        in_specs=[pl.BlockSpec(memory_space=pl.ANY), pl.BlockSpec((tm, TOP_K), lambda i, dr: (i, 0)),
                  pl.BlockSpec((tm, d), lambda i, dr: (i, 0)),
                  pl.BlockSpec((None, N_MOD, d), lambda i, dr: (row_of_tile(i, tm), 0, 0))] + norm_specs,
        out_specs=pl.BlockSpec((tm, d), lambda i, dr: (i, 0)),
        scratch_shapes=[pltpu.VMEM((2, TOP_K * tm, 1, d), F32), pltpu.VMEM((TOP_K * tm, d), F32),
                        pltpu.SemaphoreType.DMA((2,))],
    )
    return pl.pallas_call(
        functools.partial(_combine_kernel, t, g_final is not None),
        grid_spec=grid_spec,
        out_shape=jax.ShapeDtypeStruct((t, d), F32),
        compiler_params=_cparams(("arbitrary",)),
        name="moe_combine",
    )(dest_flat, y_rows, gate_tk, x, mod, *norm_args)


def _moe_layer(x_mid, mod, row_of_tile, h3, e_t, gate_t, rank_t, counts, layer, w1, b1, w2, b2, g_final=None):
    t = h3.shape[0]
    ne = w1.shape[1]
    plan, starts = _moe_plan(counts, TOP_K * t)
    start_of = jnp.sum(jnp.where(e_t[..., None] == jnp.arange(ne, dtype=jnp.int32), starts, 0), axis=-1)
    dest_flat = (start_of + rank_t).reshape(-1)
    rows = _dispatch_call(h3, dest_flat)
    y_rows = _moe_call(rows, plan, layer, w1, b1, w2, b2)
    return _combine_call(y_rows, dest_flat, gate_t.T, x_mid, mod, row_of_tile, g_final)


def _permute_cols(w):
    o = _SRC_OFF
    return jnp.concatenate([
        w[..., o["rwkv"]:o["lora"]], w[..., o["conv"]:o["qkv"]], w[..., o["gates"]:],
        w[..., o["qkv"]:o["rwkv"]], w[..., o["lora"]:o["gates"]]], axis=-1)


def kernel(x_prompt, x_sample, cache_k, cache_v, state_rwkv_fwd, state_rwkv_bwd, c, c_ctx, w_ada, b_ada, g_norm1, g_norm2, w_in, conv_w, attn_sink, rwkv_w0, rwkv_w2, rwkv_a0, rwkv_a2, rwkv_g2, rwkv_k_k, rwkv_k_a, rwkv_r_k, rwkv_ln_g, rwkv_ln_b, w_branch_conv, w_branch_attn, w_branch_rwkv, w_out, router_w, router_b, moe_w1, moe_b1, moe_w2, moe_b2, g_final):
    bc, lc, d = x_prompt.shape
    bl, tl, _ = x_sample.shape
    depth = w_in.shape[0]
    n_ctx = bc * lc
    t_all = n_ctx + bl * tl
    tm_in = next(tm for tm in (TM_IN, TM_IN // 2) if n_ctx % tm == 0 and tl % tm == 0)
    assert n_ctx % TM_MIX == 0 and tl % TM_MIX == 0 and lc % (CHUNK * CHUNKS_PER_STEP) == 0 and t_all % TM_ROUTE == 0

    def row_of_tile(i, tm):
        return jnp.where(i < n_ctx // tm, 0, 1 + (i - n_ctx // tm) // (tl // tm))

    xs = (x_prompt.reshape(n_ctx, d), x_sample.reshape(bl * tl, d))
    n_cond = -(-(1 + bl) // 8) * 8
    cond = jnp.zeros((n_cond, d), F32).at[0].set(c_ctx).at[1:1 + bl].set(c)
    mods = _ada_call(cond, w_ada, b_ada).reshape(depth, n_cond, N_MOD, d)

    w_in_p = _permute_cols(w_in).astype(BF16)
    head_id = np.arange(PAIR) // RWKV_HEAD
    ones_pair = jnp.asarray((head_id[:, None] == head_id[None, :]).astype(np.float32), BF16)
    cos_t, sin_t = _rope_tables(tl)
    zeros_state = jnp.zeros((bc, 2 * N_PAIR, RWKV_HEAD, RWKV_HEAD), F32)
    tiles = (n_ctx // TM_MIX, lc, tl)

    new_k, new_v, new_sf, new_sb = [], [], [], []
    for l in range(depth):
        mod = mods[l]
        lw = dict(
            w0=rwkv_w0[l], w2=rwkv_w2[l], a0=rwkv_a0[l], a2=rwkv_a2[l], g2=rwkv_g2[l],
            kk_w=rwkv_k_k[l].reshape(1, -1), ka_w=rwkv_k_a[l].reshape(1, -1), rk_w=rwkv_r_k[l].reshape(1, -1),
            ln_g=rwkv_ln_g[l].reshape(1, -1), ln_b=rwkv_ln_b[l].reshape(1, -1), conv_w=conv_w[l],
            ones_pair=ones_pair,
            wa=w_branch_conv[l].astype(BF16), wb=w_branch_attn[l].astype(BF16), wc=w_branch_rwkv[l].astype(BF16),
            wo=w_out[l].astype(BF16),
        )
        z = _in_call(xs, w_in_p, l, g_norm1[l], mod, row_of_tile, tm_in)

        kv_ctx = z[:n_ctx, COL_QKV + ATT_W:COL_QKV + ATT_W + 2 * KV_W].astype(F32)
        new_k.append(kv_ctx[:, :KV_W].reshape(bc, lc, N_KV_HEADS, HEAD_DIM))
        new_v.append(kv_ctx[:, KV_W:].reshape(bc, lc, N_KV_HEADS, HEAD_DIM))

        ya_c = _attn_ctx_call(z, attn_sink[l], bc, lc)
        ya_l = _attn_lat_call(z, attn_sink[l], cache_k[:, l].reshape(bl, -1, KV_W), cache_v[:, l].reshape(bl, -1, KV_W),
                              cos_t, sin_t, bl, tl, n_ctx)

        of_c, ob_c, zf_c, zb_c = _rwkv_call(z, lw, _state_to_z(zeros_state), _state_to_z(zeros_state), bc, lc, 0)
        of_l, ob_l, _, _ = _rwkv_call(z, lw, _state_to_z(state_rwkv_fwd[:, l].astype(F32)),
                                      _state_to_z(state_rwkv_bwd[:, l].astype(F32)), bl, tl, n_ctx)
        new_sf.append(_z_to_state(zf_c))
        new_sb.append(_z_to_state(zb_c))

        x_mid = _mix_call(xs, z, (of_c, of_l), (ob_c, ob_l), (ya_c, ya_l), mod, row_of_tile, lw, tiles)
        h2, e_t, gate_t, rank_t, cnt = _route_call(x_mid, mod, row_of_tile, g_norm2[l], router_w[l], router_b[l])
        xs = (_moe_layer(x_mid, mod, row_of_tile, h2, e_t, gate_t, rank_t, cnt[:, 0], l, moe_w1, moe_b1, moe_w2, moe_b2,
                         g_final if l == depth - 1 else None),)

    y = xs[0]
    y_prompt = y[:n_ctx].reshape(bc, lc, d)
    y_sample = y[n_ctx:].reshape(bl, tl, d)
    dt = x_prompt.dtype
    return (y_prompt, y_sample, jnp.stack(new_k, axis=1), jnp.stack(new_v, axis=1),
            jnp.stack(new_sf, axis=1).astype(dt), jnp.stack(new_sb, axis=1).astype(dt))
```

```python
import functools

import numpy as np
import jax
import jax.numpy as jnp
from jax import lax
from jax.experimental import pallas as pl
from jax.experimental.pallas import tpu as pltpu

F32 = jnp.float32
BF16 = jnp.bfloat16

D_MODEL = 1024
N_MOD = 6
RMS_EPS = 1e-6
CONV_W = 512
N_HEADS = 8
N_KV_HEADS = 2
GQA_GROUP = N_HEADS // N_KV_HEADS
HEAD_DIM = 64
ATT_W = N_HEADS * HEAD_DIM
KV_W = N_KV_HEADS * HEAD_DIM
WINDOW = 128
Q_BLOCK = 128
ATTN_SCALE = HEAD_DIM ** -0.5
ROPE_THETA = 10000.0
GRID_W = 64
RWKV_HEAD = 64
RWKV_W = 512
DECAY_LORA = 64
ICLR_LORA = 64
GATE_LORA = 128
GN_EPS = 64e-5
N_EXPERTS = 32
TOP_K = 4
D_EXPERT = 1024
SWIGLU_LIMIT = 7.0
SWIGLU_ALPHA = 1.702
P_TOTAL = 7296

_SRC_OFF = dict(conv=0, qkv=1536, rwkv=2304, lora=3840, gates=4224)
COL_RWKV, COL_CONV, COL_GATES, COL_QKV, COL_LORA = 0, 1536, 3072, 6144, 6912
LORA_W = 2 * DECAY_LORA + 2 * ICLR_LORA + GATE_LORA

CHUNK = 64
CHUNKS_PER_STEP = 2
PAIR = 2 * RWKV_HEAD
N_PAIR = RWKV_W // PAIR
NEG_BIG = -1e30
HALO = 16
EXP_NEG_HALF = float(np.exp(-0.5))

TM_IN = 1024
TN_IN = 2432
TM_MIX = 512
TM_ROUTE = 512
TM_MOE = 512
TM_DISPATCH = 512
TM_COMBINE = 512
ISSUE_UNROLL = 8
VMEM_LIMIT = 56 * 1024 * 1024


def _cparams(sem, vmem=VMEM_LIMIT):
    return pltpu.CompilerParams(dimension_semantics=sem, vmem_limit_bytes=vmem)


def _dot(a, b, dims=(((1,), (0,)), ((), ()))):
    return lax.dot_general(a, b, dims, preferred_element_type=F32)


_NT = (((1,), (1,)), ((), ()))
_TN = (((0,), (0,)), ((), ()))


def _split(x):
    hi = x.astype(BF16)
    lo = (x - hi.astype(F32)).astype(BF16)
    return hi, lo


def _dot3(a, b, dims=(((1,), (0,)), ((), ()))):
    ah, al = _split(a)
    bh, bl = _split(b)
    return _dot(ah, bh, dims) + (_dot(ah, bl, dims) + _dot(al, bh, dims))


def _head_sums(x, ones_pair):
    rows = x.shape[0]
    xs = jnp.concatenate([x[:, PAIR * p:PAIR * (p + 1)] for p in range(N_PAIR)], axis=0).astype(BF16)
    s = _dot(xs, ones_pair)
    return jnp.concatenate([s[rows * p:rows * (p + 1)] for p in range(N_PAIR)], axis=1)


def _sigmoid(x):
    return 0.5 * jnp.tanh(0.5 * x) + 0.5


def _rms_mod(x, g, scale, shift):
    ms = jnp.mean(x * x, axis=-1, keepdims=True)
    return (x * lax.rsqrt(ms + RMS_EPS) * g) * (1.0 + scale) + shift


def _ada_kernel(cond_ref, w_ref, b_ref, o_ref):
    c = cond_ref[...]
    s = c * _sigmoid(c)
    o_ref[...] = _dot(s.astype(BF16), w_ref[...].astype(BF16)) + b_ref[...]


def _ada_call(cond, w_ada, b_ada):
    depth, d, n = w_ada.shape
    r = cond.shape[0]
    tn = 1536
    return pl.pallas_call(
        _ada_kernel,
        grid=(depth, n // tn),
        in_specs=[
            pl.BlockSpec((r, d), lambda l, j: (0, 0)),
            pl.BlockSpec((None, d, tn), lambda l, j: (l, 0, j)),
            pl.BlockSpec((None, 1, tn), lambda l, j: (l, 0, j)),
        ],
        out_specs=pl.BlockSpec((None, r, tn), lambda l, j: (l, 0, j)),
        out_shape=jax.ShapeDtypeStruct((depth, r, n), F32),
        compiler_params=_cparams(("arbitrary", "arbitrary")),
        name="ada",
    )(cond, w_ada, b_ada.reshape(depth, 1, n))


def _in_kernel(n_first, *refs):
    if n_first is None:
        x_ref, mod_ref, g_ref, w_ref, z_ref = refs
        x = x_ref[...]
    else:
        xa_ref, xb_ref, mod_ref, g_ref, w_ref, z_ref = refs
        x = jnp.where(pl.program_id(1) < n_first, xa_ref[...], xb_ref[...])
    h = _rms_mod(x, g_ref[...], mod_ref[1:2, :], mod_ref[0:1, :]).astype(BF16)
    z_ref[...] = _dot(h, w_ref[...]).astype(BF16)


def _in_call(xs, w_bf16, layer, g_norm, mod, row_of_tile, tm):
    d = xs[0].shape[1]
    t = sum(x.shape[0] for x in xs)
    n = w_bf16.shape[2]
    tn = TN_IN
    if len(xs) == 1:
        n_first = None
        x_specs = [pl.BlockSpec((tm, d), lambda j, i: (i, 0))]
    else:
        n_first = xs[0].shape[0] // tm
        x_specs = [pl.BlockSpec((tm, d), lambda j, i: (jnp.minimum(i, n_first - 1), 0)),
                   pl.BlockSpec((tm, d), lambda j, i: (jnp.maximum(i - n_first, 0), 0))]
    return pl.pallas_call(
        functools.partial(_in_kernel, n_first),
        grid=(n // tn, t // tm),
        in_specs=x_specs + [
            pl.BlockSpec((None, N_MOD, d), lambda j, i: (row_of_tile(i, tm), 0, 0)),
            pl.BlockSpec((1, d), lambda j, i: (0, 0)),
            pl.BlockSpec((None, d, tn), lambda j, i: (layer, 0, j)),
        ],
        out_specs=pl.BlockSpec((tm, tn), lambda j, i: (i, j)),
        out_shape=jax.ShapeDtypeStruct((t, n), BF16),
        compiler_params=_cparams(("arbitrary", "arbitrary")),
        name="in_proj",
    )(*xs, mod, g_norm.reshape(1, d), w_bf16)


def _softmax_pv(scores, values, sink):
    m = sink
    for s in scores:
        m = jnp.maximum(m, jnp.max(s, axis=-1, keepdims=True))
    den = jnp.exp(sink - m)
    acc = None
    for s, v in zip(scores, values):
        p = jnp.exp(s - m)
        den = den + jnp.sum(p, axis=-1, keepdims=True)
        pv = _dot(p.astype(BF16), v)
        acc = pv if acc is None else acc + pv
    return acc / den


def _attn_ctx_kernel(sink_ref, q_ref, kv_ref, o_ref):
    q = (q_ref[...] * ATTN_SCALE).astype(BF16)
    kv = kv_ref[...].astype(BF16)
    outs = []
    for h in range(N_HEADS):
        g = h // GQA_GROUP
        qh = q[:, HEAD_DIM * h:HEAD_DIM * (h + 1)]
        kh = kv[:, HEAD_DIM * g:HEAD_DIM * (g + 1)]
        vh = kv[:, KV_W + HEAD_DIM * g:KV_W + HEAD_DIM * (g + 1)]
        outs.append(_softmax_pv([_dot(qh, kh, _NT)], [vh], sink_ref[h]))
    o_ref[...] = jnp.concatenate(outs, axis=-1).astype(BF16)


def _attn_ctx_call(z, sink, n_seq, seq_len):
    return pl.pallas_call(
        _attn_ctx_kernel,
        grid=(n_seq,),
        in_specs=[
            pl.BlockSpec(memory_space=pltpu.SMEM),
            pl.BlockSpec((seq_len, ATT_W), lambda s: (s, COL_QKV // ATT_W)),
            pl.BlockSpec((seq_len, 2 * KV_W), lambda s: (s, (COL_QKV + ATT_W) // (2 * KV_W))),
        ],
        out_specs=pl.BlockSpec((seq_len, ATT_W), lambda s: (s, 0)),
        out_shape=jax.ShapeDtypeStruct((n_seq * seq_len, ATT_W), BF16),
        compiler_params=_cparams(("arbitrary",)),
        name="attn_ctx",
    )(sink, z, z)


def _rope(x, cos, sin_signed):
    n = x.shape[-1]
    lane = lax.broadcasted_iota(jnp.int32, x.shape, 1)
    up = pltpu.roll(x, n - 16, 1)
    dn = pltpu.roll(x, 16, 1)
    partner = jnp.where((lane & 31) < 16, up, dn)
    return x * cos + partner * sin_signed


def _attn_lat_kernel(sink_ref, q_ref, kvp_ref, kvc_ref, kvn_ref, ck_ref, cv_ref,
                     cosc_ref, sinc_ref, cosp_ref, sinp_ref, cosn_ref, sinn_ref, o_ref):
    step = pl.program_id(1)
    n_steps = pl.num_programs(1)
    qb2 = 2 * Q_BLOCK
    cc, sc = cosc_ref[...], sinc_ref[...]
    q = _rope(q_ref[...].astype(F32), jnp.concatenate([cc] * 4, axis=1), jnp.concatenate([sc] * 4, axis=1))
    q = (q * ATTN_SCALE).astype(BF16)
    kvp, kvc, kvn = (ref[...].astype(F32) for ref in (kvp_ref, kvc_ref, kvn_ref))
    kp = _rope(kvp[:, :KV_W], cosp_ref[...], sinp_ref[...]).astype(BF16)
    kc = _rope(kvc[:, :KV_W], cc, sc).astype(BF16)
    kn = _rope(kvn[:, :KV_W], cosn_ref[...], sinn_ref[...]).astype(BF16)
    vp, vc, vn = (t[:, KV_W:].astype(BF16) for t in (kvp, kvc, kvn))
    ck = ck_ref[...].astype(BF16)
    cv = cv_ref[...].astype(BF16)
    k_cat = [jnp.concatenate([kp, kc, ck], axis=0), jnp.concatenate([kc, kn, ck], axis=0)]
    v_cat = [jnp.concatenate([vp, vc, cv], axis=0), jnp.concatenate([vc, vn, cv], axis=0)]
    nk = k_cat[0].shape[0]
    qi = lax.broadcasted_iota(jnp.int32, (Q_BLOCK, nk), 0)
    kj = lax.broadcasted_iota(jnp.int32, (Q_BLOCK, nk), 1)
    in_next = (kj >= qb2) & (kj < 3 * Q_BLOCK)
    before_ok = (step > 0, True)
    after_ok = (True, step < n_steps - 1)
    bias = []
    for u in range(2):
        ok_p = (kj >= qi) & before_ok[u]
        ok_n = (kj - qb2 <= qi) & after_ok[u]
        bu = jnp.where(kj < Q_BLOCK, jnp.where(ok_p, 0.0, NEG_BIG), jnp.where(in_next, jnp.where(ok_n, 0.0, NEG_BIG), 0.0))
        bias.append(jnp.concatenate([bu] * GQA_GROUP, axis=0))
    chains = [(u, g) for u in range(2) for g in range(N_KV_HEADS)]
    heads = {g: range(GQA_GROUP * g, GQA_GROUP * (g + 1)) for g in range(N_KV_HEADS)}
    gsl = {g: slice(HEAD_DIM * g, HEAD_DIM * (g + 1)) for g in range(N_KV_HEADS)}
    q_g = {(u, g): jnp.concatenate([q[Q_BLOCK * u:Q_BLOCK * (u + 1), HEAD_DIM * h:HEAD_DIM * (h + 1)] for h in heads[g]], axis=0)
           for u, g in chains}
    sink = {g: jnp.concatenate([jnp.full((Q_BLOCK, 1), sink_ref[h], F32) for h in heads[g]], axis=0) for g in range(N_KV_HEADS)}
    s = {ch: _dot(q_g[ch], k_cat[ch[0]][:, gsl[ch[1]]], _NT) + bias[ch[0]] for ch in chains}
    m = {ch: jnp.maximum(jnp.max(s[ch], axis=-1, keepdims=True), sink[ch[1]]) for ch in chains}
    p = {ch: jnp.exp(s[ch] - m[ch]) for ch in chains}
    den = {ch: jnp.sum(p[ch], axis=-1, keepdims=True) + jnp.exp(sink[ch[1]] - m[ch]) for ch in chains}
    o = {ch: _dot(p[ch].astype(BF16), v_cat[ch[0]][:, gsl[ch[1]]]) / den[ch] for ch in chains}
    for u in range(2):
        outs = [o[u, g][Q_BLOCK * i:Q_BLOCK * (i + 1)] for g in range(N_KV_HEADS) for i in range(GQA_GROUP)]
        o_ref[Q_BLOCK * u:Q_BLOCK * (u + 1), :] = jnp.concatenate(outs, axis=-1).astype(BF16)


def _attn_lat_call(z, sink, cache_k, cache_v, cos_t, sin_t, n_seq, seq_len, row0):
    nb = seq_len // Q_BLOCK
    ns = nb // 2
    base = row0 // Q_BLOCK
    past = cache_k.shape[1]
    kv_col = (COL_QKV + ATT_W) // (2 * KV_W)

    def kv_spec(off):
        return pl.BlockSpec((Q_BLOCK, 2 * KV_W), lambda b, s: (base + b * nb + jnp.clip(2 * s + off, 0, nb - 1), kv_col))

    def tab_spec(off):
        return pl.BlockSpec((Q_BLOCK, KV_W), lambda b, s: (jnp.clip(2 * s + off, 0, nb - 1), 0))

    return pl.pallas_call(
        _attn_lat_kernel,
        grid=(n_seq, ns),
        in_specs=[
            pl.BlockSpec(memory_space=pltpu.SMEM),
            pl.BlockSpec((2 * Q_BLOCK, ATT_W), lambda b, s: (base // 2 + b * ns + s, COL_QKV // ATT_W)),
            kv_spec(-1),
            pl.BlockSpec((2 * Q_BLOCK, 2 * KV_W), lambda b, s: (base // 2 + b * ns + s, kv_col)),
            kv_spec(2),
            pl.BlockSpec((None, past, KV_W), lambda b, s: (b, 0, 0)),
            pl.BlockSpec((None, past, KV_W), lambda b, s: (b, 0, 0)),
            pl.BlockSpec((2 * Q_BLOCK, KV_W), lambda b, s: (s, 0)), pl.BlockSpec((2 * Q_BLOCK, KV_W), lambda b, s: (s, 0)),
            tab_spec(-1), tab_spec(-1), tab_spec(2), tab_spec(2),
        ],
        out_specs=pl.BlockSpec((2 * Q_BLOCK, ATT_W), lambda b, s: (b * ns + s, 0)),
        out_shape=jax.ShapeDtypeStruct((n_seq * seq_len, ATT_W), BF16),
        compiler_params=_cparams(("arbitrary", "arbitrary")),
        name="attn_lat",
    )(sink, z, z, z, z, cache_k, cache_v, cos_t, sin_t, cos_t, sin_t, cos_t, sin_t)


def _rope_tables(seq_len):
    half = HEAD_DIM // 2
    pos = np.arange(seq_len)
    inv_freq = 1.0 / (ROPE_THETA ** (np.arange(0, half, 2, dtype=np.float32) / half))
    inv_freq = inv_freq.astype(np.float32)

    def part(p):
        ang = (p.astype(np.float32)[:, None] * inv_freq[None, :]).astype(np.float32)
        c, s = np.cos(ang), np.sin(ang)
        return np.concatenate([c, c], axis=1), np.concatenate([-s, s], axis=1)

    c_r, s_r = part(pos // GRID_W)
    c_c, s_c = part(pos % GRID_W)
    cos = np.concatenate([c_r, c_c] * N_KV_HEADS, axis=1).astype(np.float32)
    sin = np.concatenate([s_r, s_c] * N_KV_HEADS, axis=1).astype(np.float32)
    return jnp.asarray(cos), jnp.asarray(sin)


def _rwkv_dir_inputs(zm, zl, d, w0, w2, a0, a2, kk_w, ka_w, ones_pair):
    r = zm[:, :RWKV_W]
    kraw = zm[:, RWKV_W:2 * RWKV_W]
    v = zm[:, 2 * RWKV_W:]
    wl = zl[:, DECAY_LORA * d:DECAY_LORA * (d + 1)]
    al = zl[:, 2 * DECAY_LORA + ICLR_LORA * d:2 * DECAY_LORA + ICLR_LORA * (d + 1)]
    xw = w0 + _dot(jnp.tanh(wl).astype(BF16), w2.astype(BF16))
    ld = -EXP_NEG_HALF * _sigmoid(xw)
    a = _sigmoid(a0 + _dot(al.astype(BF16), a2.astype(BF16)))
    k = kraw * (1.0 + (a - 1.0) * ka_w)
    kkr = kraw * kk_w
    n2 = _head_sums(kkr * kkr, ones_pair)
    kk = kkr / jnp.maximum(jnp.sqrt(n2), 1e-12)
    return r, v, kk, ld, a, k


def _tri_masks(rev):
    c = CHUNK
    ti = lax.broadcasted_iota(jnp.int32, (c, c), 0)
    si = lax.broadcasted_iota(jnp.int32, (c, c), 1)
    incl = (si >= ti) if rev else (si <= ti)
    strict = (si > ti) if rev else (si < ti)
    return incl, strict, (si == ti).astype(F32)


def _chunk_prepare(r, v, kk, ld, a, k, rev):
    incl, _, _ = _tri_masks(rev)
    m_incl = jnp.where(incl, 1.0, 0.0).astype(BF16)
    ldh, ldl = _split(ld)
    cin = _dot(m_incl, ldh) + _dot(m_incl, ldl)
    tot = jnp.sum(ld, axis=0, keepdims=True)
    e_neg = jnp.exp(-cin)
    e_end = jnp.exp(tot - cin)
    bb = kk * a
    return dict(
        a_m=kk * jnp.exp(cin - ld), r_m=r * jnp.exp(cin),
        b_m=(bb * e_neg).astype(BF16), k_m=(k * e_neg).astype(BF16),
        b_end=(bb * e_end).astype(BF16), k_end=(k * e_end).astype(BF16),
        v=v, e_tot=jnp.exp(tot), rev=rev)


def _chunk_problems(probs):
    c = CHUNK
    n = len(probs)
    rng = range(n)
    lane = lax.broadcasted_iota(jnp.int32, (1, PAIR), 1)
    first_head = lane < RWKV_HEAD

    def bd(x):
        xb = x.astype(BF16)
        zero = jnp.zeros_like(xb)
        return jnp.concatenate([jnp.where(first_head, xb, zero), jnp.where(first_head, zero, xb)], axis=0)

    ti = lax.broadcasted_iota(jnp.int32, (c, PAIR), 0)
    si = lax.broadcasted_iota(jnp.int32, (c, PAIR), 1) & (RWKV_HEAD - 1)
    incl = {False: si <= ti, True: si >= ti}
    strict = {False: si < ti, True: si > ti}
    eye = (si == ti).astype(F32)
    rev = [p["rev"] for p in probs]

    lhs = [jnp.concatenate([probs[i]["a_m"], probs[i]["r_m"]], axis=0).astype(BF16) for i in rng]
    xb = [_dot(lhs[i], bd(probs[i]["b_m"]), _NT) for i in rng]
    xk = [_dot(lhs[i], bd(probs[i]["k_m"]), _NT) for i in rng]
    m_ab = [jnp.where(strict[rev[i]], xb[i][:c], 0.0) for i in rng]
    m_ak = [jnp.where(strict[rev[i]], xk[i][:c], 0.0).astype(BF16) for i in rng]
    m_rb = [jnp.where(incl[rev[i]], xb[i][c:], 0.0).astype(BF16) for i in rng]
    m_rk = [jnp.where(incl[rev[i]], xk[i][c:], 0.0).astype(BF16) for i in rng]
    v_bd = [bd(probs[i]["v"]) for i in rng]
    mv = [_dot(jnp.concatenate([m_ak[i], m_rk[i]], axis=0), v_bd[i]) for i in rng]
    mak_v = [mv[i][:c] for i in rng]
    mrk_v = [mv[i][c:] for i in rng]
    t_inv = [eye - m_ab[i] for i in rng]
    lp = [_dot(m_ab[i].astype(BF16), bd(m_ab[i])) for i in rng]
    for _ in range(4):
        both = [_dot(jnp.concatenate([lp[i], t_inv[i]], axis=0).astype(BF16), bd(lp[i])) for i in rng]
        t_inv = [t_inv[i] + both[i][c:] for i in rng]
        lp = [both[i][:c] for i in rng]
    t_inv = [t_inv[i] + _dot(t_inv[i].astype(BF16), bd(lp[i])) for i in rng]
    wu = [_dot(t_inv[i].astype(BF16), jnp.concatenate([bd(probs[i]["a_m"]), bd(mak_v[i])], axis=1)) for i in rng]
    ro = [_dot(m_rb[i], jnp.concatenate([bd(wu[i][:, :PAIR]), bd(wu[i][:, PAIR:])], axis=1)) for i in rng]
    ri = lax.broadcasted_iota(jnp.int32, (PAIR, PAIR), 0)
    ci = lax.broadcasted_iota(jnp.int32, (PAIR, PAIR), 1)
    same_head = (ri < RWKV_HEAD) == (ci < RWKV_HEAD)
    bt_wu = [_dot(probs[i]["b_end"], wu[i].astype(BF16), _TN) for i in rng]
    kt_v = [_dot(probs[i]["k_end"], probs[i]["v"].astype(BF16), _TN) for i in rng]
    local = []
    for i in rng:
        r_eff = probs[i]["r_m"] - ro[i][:, :PAIR]
        o_loc = mrk_v[i] - ro[i][:, PAIR:]
        g_t = jnp.where(same_head, jnp.where(ri == ci, probs[i]["e_tot"], 0.0) - bt_wu[i][:, :PAIR], 0.0)
        h_t = jnp.where(same_head, kt_v[i] - bt_wu[i][:, PAIR:], 0.0)
        local.append((_split(jnp.concatenate([r_eff, g_t], axis=0)), o_loc, h_t))
    return local


def _chunk_apply(local, z):
    (bh, bl), o_loc, h_t = local
    zh, zl = _split(z)
    prod = _dot(bh, zh) + (_dot(bh, zl) + _dot(bl, zh))
    return prod[:CHUNK] + o_loc, prod[CHUNK:] + h_t


def _rwkv_kernel(zmf_ref, zlf_ref, zmb_ref, zlb_ref, w0_ref, w2_ref, a0_ref, a2_ref, kkw_ref, kaw_ref,
                 ones_ref, s0f_ref, s0b_ref, of_ref, ob_ref, zf_ref, zb_ref, zf_scr, zb_scr):
    @pl.when(pl.program_id(1) == 0)
    def _():
        zf_scr[...] = s0f_ref[...]
        zb_scr[...] = s0b_ref[...]

    ones_pair = ones_ref[...]
    dirs = ((zmf_ref, zlf_ref, zf_scr, of_ref), (zmb_ref, zlb_ref, zb_scr, ob_ref))
    n_sub = zmf_ref.shape[0] // CHUNK
    probs = []
    for d, (zm_ref, zl_ref, _, _) in enumerate(dirs):
        r, v, kk, ld, a, k = _rwkv_dir_inputs(zm_ref[...].astype(F32), zl_ref[...].astype(F32), d, w0_ref[d:d + 1, :], w2_ref[d],
                                              a0_ref[d:d + 1, :], a2_ref[d], kkw_ref[...], kaw_ref[...], ones_pair)
        for sub in range(n_sub):
            rows = slice(CHUNK * sub, CHUNK * (sub + 1))
            full = _chunk_prepare(r[rows], v[rows], kk[rows], ld[rows], a[rows], k[rows], d == 1)
            for p in range(N_PAIR):
                ps = slice(PAIR * p, PAIR * (p + 1))
                probs.append({key: (val[:, ps] if hasattr(val, "shape") else val) for key, val in full.items()})
    local = _chunk_problems(probs)
    for d, (_, _, z_scr, o_ref) in enumerate(dirs):
        order = range(n_sub) if d == 0 else range(n_sub - 1, -1, -1)
        for p in range(N_PAIR):
            z = z_scr[p]
            for sub in order:
                out, z = _chunk_apply(local[(d * n_sub + sub) * N_PAIR + p], z)
                o_ref[CHUNK * sub:CHUNK * (sub + 1), PAIR * p:PAIR * (p + 1)] = out
            z_scr[p] = z
    zf_ref[...] = zf_scr[...]
    zb_ref[...] = zb_scr[...]


def _rwkv_call(z, lw, s0f, s0b, n_seq, seq_len, row0):
    rows = CHUNK * CHUNKS_PER_STEP
    nc = seq_len // rows
    base = row0 // rows

    def fwd(s, c):
        return base + s * nc + c

    def bwd(s, c):
        return base + s * nc + (nc - 1 - c)

    def zm_spec(f):
        return pl.BlockSpec((rows, 3 * RWKV_W), lambda s, c: (f(s, c), COL_RWKV // (3 * RWKV_W)))

    def zl_spec(f):
        return pl.BlockSpec((rows, LORA_W), lambda s, c: (f(s, c), COL_LORA // LORA_W))

    def full(shape):
        return pl.BlockSpec(shape, lambda s, c: (0,) * len(shape))

    st_spec = pl.BlockSpec((None, N_PAIR, PAIR, PAIR), lambda s, c: (s, 0, 0, 0))
    o_shape = jax.ShapeDtypeStruct((n_seq * seq_len, RWKV_W), F32)
    st_shape = jax.ShapeDtypeStruct((n_seq, N_PAIR, PAIR, PAIR), F32)
    return pl.pallas_call(
        _rwkv_kernel,
        grid=(n_seq, nc),
        in_specs=[
            zm_spec(fwd), zl_spec(fwd), zm_spec(bwd), zl_spec(bwd),
            full((2, RWKV_W)), full((2, DECAY_LORA, RWKV_W)), full((2, RWKV_W)), full((2, ICLR_LORA, RWKV_W)),
            full((1, RWKV_W)), full((1, RWKV_W)), full((PAIR, PAIR)),
            st_spec, st_spec,
        ],
        out_specs=[
            pl.BlockSpec((rows, RWKV_W), lambda s, c: (s * nc + c, 0)),
            pl.BlockSpec((rows, RWKV_W), lambda s, c: (s * nc + (nc - 1 - c), 0)),
            st_spec, st_spec,
        ],
        out_shape=[o_shape, o_shape, st_shape, st_shape],
        scratch_shapes=[pltpu.VMEM((N_PAIR, PAIR, PAIR), F32), pltpu.VMEM((N_PAIR, PAIR, PAIR), F32)],
        compiler_params=_cparams(("arbitrary", "arbitrary")),
        name="rwkv_scan",
    )(z, z, z, z, lw["w0"], lw["w2"], lw["a0"], lw["a2"], lw["kk_w"], lw["ka_w"], lw["ones_pair"], s0f, s0b)


def _state_to_z(s):
    n = s.shape[0]
    st = jnp.swapaxes(s, -1, -2).reshape(n, N_PAIR, 2, RWKV_HEAD, RWKV_HEAD)
    zero = jnp.zeros_like(st[:, :, 0])
    top = jnp.concatenate([st[:, :, 0], zero], axis=-1)
    bot = jnp.concatenate([zero, st[:, :, 1]], axis=-1)
    return jnp.concatenate([top, bot], axis=-2)


def _z_to_state(z):
    n = z.shape[0]
    h0 = z[:, :, :RWKV_HEAD, :RWKV_HEAD]
    h1 = z[:, :, RWKV_HEAD:, RWKV_HEAD:]
    st = jnp.stack([h0, h1], axis=2).reshape(n, 2 * N_PAIR, RWKV_HEAD, RWKV_HEAD)
    return jnp.swapaxes(st, -1, -2)


def _mix_kernel(tiles, n_x, *refs):
    n_ctx_tiles, len_ctx, len_lat = tiles
    i = pl.program_id(0)
    in_ctx = i < n_ctx_tiles

    def pick(pair):
        return jnp.where(in_ctx, pair[0][...], pair[1][...])

    x_refs, refs = refs[:n_x], refs[n_x:]
    (zc_ref, zcp_ref, zcn_ref, zm_ref, zl_ref, zg_ref, ofc_ref, ofl_ref, obc_ref, obl_ref, yac_ref, yal_ref, mod_ref,
     convw_ref, a0_ref, a2_ref, g2_ref, kaw_ref, rkw_ref, lng_ref, lnb_ref, ones_ref,
     wa_ref, wb_ref, wc_ref, wo_ref, o_ref) = refs
    x_in = x_refs[0][...] if n_x == 1 else pick(x_refs)
    tm = o_ref.shape[0]

    zc = zc_ref[...].astype(F32)
    u = zc[:, 2 * CONV_W:] * zc[:, :CONV_W]
    zp = zcp_ref[...].astype(F32)
    zn = zcn_ref[...].astype(F32)
    row = lax.broadcasted_iota(jnp.int32, (tm, 1), 0)
    pos = jnp.where(in_ctx, lax.rem(i * tm + row, len_ctx), lax.rem((i - n_ctx_tiles) * tm + row, len_lat))
    seq_len = jnp.where(in_ctx, len_ctx, len_lat)
    u_prev = jnp.where(row == 0, zp[HALO - 1:HALO, 2 * CONV_W:] * zp[HALO - 1:HALO, :CONV_W], pltpu.roll(u, 1, 0))
    u_next = jnp.where(row == tm - 1, zn[0:1, 2 * CONV_W:] * zn[0:1, :CONV_W], pltpu.roll(u, tm - 1, 0))
    u_prev = jnp.where(pos == 0, 0.0, u_prev)
    u_next = jnp.where(pos == seq_len - 1, 0.0, u_next)
    cw = convw_ref[...]
    y_conv = zc[:, CONV_W:2 * CONV_W] * (cw[0:1, :] * u_prev + cw[1:2, :] * u + cw[2:3, :] * u_next)

    zm = zm_ref[...].astype(F32)
    zl = zl_ref[...].astype(F32)
    r = zm[:, :RWKV_W]
    kraw = zm[:, RWKV_W:2 * RWKV_W]
    v = zm[:, 2 * RWKV_W:]
    ones_pair = ones_ref[...]
    o = pick((ofc_ref, ofl_ref)) + pick((obc_ref, obl_ref))
    mu = _head_sums(o, ones_pair) * (1.0 / RWKV_HEAD)
    dlt = o - mu
    var = _head_sums(dlt * dlt, ones_pair) * (1.0 / RWKV_HEAD)
    y = dlt * lax.rsqrt(var + GN_EPS) * lng_ref[...] + lnb_ref[...]
    for d in range(2):
        al = zl[:, 2 * DECAY_LORA + ICLR_LORA * d:2 * DECAY_LORA + ICLR_LORA * (d + 1)]
        a = _sigmoid(a0_ref[d:d + 1, :] + _dot(al.astype(BF16), a2_ref[d].astype(BF16)))
        k = kraw * (1.0 + (a - 1.0) * kaw_ref[...])
        y = y + _head_sums(r * k * rkw_ref[...], ones_pair) * v
    g1 = zl[:, 2 * DECAY_LORA + 2 * ICLR_LORA:]
    y_rwkv = y * _dot(_sigmoid(g1).astype(BF16), g2_ref[...].astype(BF16))

    zg = zg_ref[...].astype(F32)
    merged = (_sigmoid(zg[:, :D_MODEL]) * _dot(y_conv.astype(BF16), wa_ref[...])
              + _sigmoid(zg[:, D_MODEL:2 * D_MODEL]) * _dot(pick((yac_ref, yal_ref)), wb_ref[...])
              + _sigmoid(zg[:, 2 * D_MODEL:]) * _dot(y_rwkv.astype(BF16), wc_ref[...]))
    o_ref[...] = x_in + mod_ref[2:3, :] * _dot(merged.astype(BF16), wo_ref[...])


def _mix_call(xs, z, o_f, o_b, y_attn, mod, row_of_tile, lw, tiles):
    t, d = z.shape[0], xs[0].shape[1]
    tm = TM_MIX
    nt = t // tm
    hb = tm // HALO
    n_ctx_tiles = tiles[0]

    def rows(w, col):
        return pl.BlockSpec((tm, w), lambda i: (i, col // w))

    def pair(w):
        return [pl.BlockSpec((tm, w), lambda i: (jnp.minimum(i, n_ctx_tiles - 1), 0)),
                pl.BlockSpec((tm, w), lambda i: (jnp.maximum(i - n_ctx_tiles, 0), 0))]

    def full(shape):
        return pl.BlockSpec(shape, lambda i: (0,) * len(shape))

    in_specs = (pair(d) if len(xs) == 2 else [rows(d, 0)]) + [
        rows(3 * CONV_W, COL_CONV),
        pl.BlockSpec((HALO, 3 * CONV_W), lambda i: (jnp.maximum(i * hb - 1, 0), COL_CONV // (3 * CONV_W))),
        pl.BlockSpec((HALO, 3 * CONV_W), lambda i: (jnp.minimum((i + 1) * hb, nt * hb - 1), COL_CONV // (3 * CONV_W))),
        rows(3 * RWKV_W, COL_RWKV),
        rows(LORA_W, COL_LORA),
        rows(3 * D_MODEL, COL_GATES),
    ] + pair(RWKV_W) + pair(RWKV_W) + pair(ATT_W) + [
        pl.BlockSpec((None, N_MOD, d), lambda i: (row_of_tile(i, tm), 0, 0)),
        full((3, CONV_W)), full((2, RWKV_W)), full((2, ICLR_LORA, RWKV_W)), full((GATE_LORA, RWKV_W)),
        full((1, RWKV_W)), full((1, RWKV_W)), full((1, RWKV_W)), full((1, RWKV_W)),
        full((PAIR, PAIR)),
        full((CONV_W, d)), full((ATT_W, d)), full((RWKV_W, d)), full((d, d)),
    ]
    return pl.pallas_call(
        functools.partial(_mix_kernel, tiles, len(xs)),
        grid=(nt,),
        in_specs=in_specs,
        out_specs=rows(d, 0),
        out_shape=jax.ShapeDtypeStruct((t, d), F32),
        compiler_params=_cparams(("arbitrary",)),
        name="mix",
    )(*xs, z, z, z, z, z, z, *o_f, *o_b, *y_attn, mod,
      lw["conv_w"], lw["a0"], lw["a2"], lw["g2"], lw["ka_w"], lw["rk_w"], lw["ln_g"], lw["ln_b"],
      lw["ones_pair"], lw["wa"], lw["wb"], lw["wc"], lw["wo"])


def _route_kernel(x_ref, mod_ref, g_ref, rwt_ref, rb_ref, tri_ref, h_ref, e_ref, gate_ref, rank_ref, cnt_ref, cnt_scr):
    @pl.when(pl.program_id(0) == 0)
    def _():
        cnt_scr[...] = jnp.zeros_like(cnt_scr)

    h2 = _rms_mod(x_ref[...], g_ref[...], mod_ref[4:5, :], mod_ref[3:4, :])
    h_ref[...] = h2.reshape(h_ref.shape)
    logits = _dot3(rwt_ref[...], h2, _NT) + rb_ref[...]
    ne, tm = logits.shape
    ex = lax.broadcasted_iota(jnp.int32, (ne, tm), 0)
    work = logits
    vals, hots = [], []
    for kq in range(TOP_K):
        m = jnp.max(work, axis=0, keepdims=True)
        idx = jnp.min(jnp.where(work == m, ex, ne), axis=0, keepdims=True)
        hot = ex == idx
        vals.append(m)
        hots.append(hot)
        e_ref[kq:kq + 1, :] = idx
        work = jnp.where(hot, -jnp.inf, work)
    exps = [jnp.exp(vk - vals[0]) for vk in vals]
    den = exps[0] + exps[1] + exps[2] + exps[3]
    chosen = jnp.where(hots[0] | hots[1] | hots[2] | hots[3], 1.0, 0.0)
    before = cnt_scr[:, 0:1] + _dot(chosen.astype(BF16), tri_ref[...])
    for kq in range(TOP_K):
        gate_ref[kq:kq + 1, :] = exps[kq] / den
        rank_ref[kq:kq + 1, :] = jnp.sum(jnp.where(hots[kq], before, 0.0), axis=0, keepdims=True).astype(jnp.int32)
    cnt_scr[...] = cnt_scr[...] + jnp.sum(chosen, axis=1, keepdims=True)
    cnt_ref[...] = cnt_scr[...].astype(jnp.int32)


def _route_call(x, mod, row_of_tile, g_norm, router_w, router_b):
    t, d = x.shape
    tm = TM_ROUTE
    ne = router_w.shape[1]
    tri = jnp.triu(jnp.ones((tm, tm), F32), 1).astype(BF16)

    def full(shape):
        return pl.BlockSpec(shape, lambda i: (0,) * len(shape))

    kt_spec = pl.BlockSpec((TOP_K, tm), lambda i: (0, i))
    return pl.pallas_call(
        _route_kernel,
        grid=(t // tm,),
        in_specs=[
            pl.BlockSpec((tm, d), lambda i: (i, 0)),
            pl.BlockSpec((None, N_MOD, d), lambda i: (row_of_tile(i, tm), 0, 0)),
            full((1, d)), full((ne, d)), full((ne, 1)), full((tm, tm)),
        ],
        out_specs=[pl.BlockSpec((tm, 1, d), lambda i: (i, 0, 0)), kt_spec, kt_spec, kt_spec, full((ne, 128))],
        out_shape=[
            jax.ShapeDtypeStruct((t, 1, d), F32),
            jax.ShapeDtypeStruct((TOP_K, t), jnp.int32),
            jax.ShapeDtypeStruct((TOP_K, t), F32),
            jax.ShapeDtypeStruct((TOP_K, t), jnp.int32),
            jax.ShapeDtypeStruct((ne, 128), jnp.int32),
        ],
        scratch_shapes=[pltpu.VMEM((ne, 128), F32)],
        compiler_params=_cparams(("arbitrary",)),
        name="route",
    )(x, mod, g_norm.reshape(1, d), router_w.T, router_b.reshape(ne, 1), tri)


def _row_copy_wait(buf_hbm, n_rows, sem):
    view = buf_hbm.at[pl.ds(0, n_rows)]
    pltpu.make_async_copy(view, view, sem).wait()


def _dispatch_kernel(t_all, dest_ref, h_ref, rows_hbm, sem):
    tm = TM_DISPATCH
    t0 = pl.program_id(0) * tm

    def body(t, carry):
        for kq in range(TOP_K):
            pltpu.make_async_copy(h_ref.at[t], rows_hbm.at[dest_ref[kq * t_all + t0 + t]], sem).start(priority=kq % 2)
        return carry

    lax.fori_loop(0, tm, body, 0, unroll=ISSUE_UNROLL)
    _row_copy_wait(rows_hbm, TOP_K * tm, sem)


def _dispatch_call(h3, dest_flat):
    t, _, d = h3.shape
    grid_spec = pltpu.PrefetchScalarGridSpec(
        num_scalar_prefetch=1,
        grid=(t // TM_DISPATCH,),
        in_specs=[pl.BlockSpec((TM_DISPATCH, 1, d), lambda i, dr: (i, 0, 0))],
        out_specs=pl.BlockSpec(memory_space=pl.ANY),
        scratch_shapes=[pltpu.SemaphoreType.DMA(())],
    )
    return pl.pallas_call(
        functools.partial(_dispatch_kernel, t),
        grid_spec=grid_spec,
        out_shape=jax.ShapeDtypeStruct((TOP_K * t, 1, d), F32),
        compiler_params=_cparams(("arbitrary",)),
        name="moe_dispatch",
    )(dest_flat, h3)


def _moe_kernel(blk_ref, exp_ref, lo_ref, hi_ref, fblk_ref, fexp_ref, nit_ref,
                x_ref, w1_ref, b1_ref, w2_ref, b2_ref, o_ref, w1_scr, w2_scr, x_scr, acc_scr):
    del blk_ref, exp_ref
    it = pl.program_id(0)

    @pl.when(fexp_ref[it] == 1)
    def _():
        w1_scr[...] = w1_ref[...].astype(BF16)
        w2_scr[...] = w2_ref[...].astype(BF16)

    @pl.when(it < nit_ref[0])
    def _():
        x_scr[...] = x_ref[...].reshape(x_scr.shape)
        hm = _dot(x_scr[...].astype(BF16), w1_scr[...]) + b1_ref[...]
        glu = jnp.minimum(hm[:, :D_EXPERT], SWIGLU_LIMIT)
        lin = jnp.clip(hm[:, D_EXPERT:], -SWIGLU_LIMIT, SWIGLU_LIMIT)
        act = glu * _sigmoid(SWIGLU_ALPHA * glu) * (lin + 1.0)
        y = _dot(act.astype(BF16), w2_scr[...]) + b2_ref[...]
        row = lax.broadcasted_iota(jnp.int32, (y.shape[0], 1), 0)
        y = jnp.where((row >= lo_ref[it]) & (row < hi_ref[it]), y, 0.0)

        @pl.when(fblk_ref[it] == 1)
        def _():
            o_ref[...] = y.reshape(o_ref.shape)

        @pl.when(fblk_ref[it] == 0)
        def _():
            acc_scr[...] = o_ref[...].reshape(acc_scr.shape)
            o_ref[...] = (acc_scr[...] + y).reshape(o_ref.shape)


def _moe_call(rows, plan, layer, w1, b1, w2, b2):
    n_rows, _, d = rows.shape
    tm = TM_MOE
    depth, ne, _, dh2 = w1.shape
    de = w2.shape[2]
    n_items = plan[0].shape[0]

    def row_map(it, blk, ex, lo, hi, fb, fe, nit):
        return (blk[it], 0, 0)

    def w_map(it, blk, ex, lo, hi, fb, fe, nit):
        return (layer, ex[it], 0, 0)

    grid_spec = pltpu.PrefetchScalarGridSpec(
        num_scalar_prefetch=7,
        grid=(n_items,),
        in_specs=[
            pl.BlockSpec((tm, 1, d), row_map),
            pl.BlockSpec((None, None, d, dh2), w_map),
            pl.BlockSpec((None, None, 1, dh2), w_map),
            pl.BlockSpec((None, None, de, d), w_map),
            pl.BlockSpec((None, None, 1, d), w_map),
        ],
        out_specs=pl.BlockSpec((tm, 1, d), row_map),
        scratch_shapes=[pltpu.VMEM((d, dh2), BF16), pltpu.VMEM((de, d), BF16), pltpu.VMEM((tm, d), F32),
                        pltpu.VMEM((tm, d), F32)],
    )
    return pl.pallas_call(
        _moe_kernel,
        grid_spec=grid_spec,
        out_shape=jax.ShapeDtypeStruct((n_rows, 1, d), F32),
        compiler_params=_cparams(("arbitrary",)),
        name="moe",
    )(*plan, rows, w1, b1.reshape(depth, ne, 1, dh2), w2, b2.reshape(depth, ne, 1, d))


def _moe_plan(counts, n_rows):
    tm = TM_MOE
    ne = counts.shape[0]
    n_items = n_rows // tm + ne - 1
    counts = counts.astype(jnp.int32)
    ends = jnp.cumsum(counts)
    starts = ends - counts
    first_blk = starts // tm
    last_blk = jnp.maximum(ends - 1, 0) // tm
    per_exp = jnp.where(counts > 0, last_blk - first_blk + 1, 0)
    item_end = jnp.cumsum(per_exp)
    item_off = item_end - per_exp
    total = item_end[-1]
    it = jnp.arange(n_items, dtype=jnp.int32)
    itc = jnp.minimum(it, total - 1)
    ex = jnp.minimum(jnp.sum((itc[:, None] >= item_end[None, :]).astype(jnp.int32), axis=1), ne - 1)
    hot = ex[:, None] == jnp.arange(ne, dtype=jnp.int32)[None, :]

    def pick(v):
        return jnp.sum(jnp.where(hot, v[None, :], 0), axis=1)

    blk = pick(first_blk) + itc - pick(item_off)
    lo = jnp.maximum(pick(starts), blk * tm) - blk * tm
    hi = jnp.where(it < total, jnp.minimum(pick(ends), (blk + 1) * tm) - blk * tm, lo)
    one = jnp.ones((1,), jnp.int32)
    f_blk = jnp.concatenate([one, (blk[1:] != blk[:-1]).astype(jnp.int32)])
    f_exp = jnp.concatenate([one, (ex[1:] != ex[:-1]).astype(jnp.int32)])
    return (blk, ex, lo, hi, f_blk, f_exp, total.reshape(1)), starts


def _combine_kernel(t_all, has_norm, dest_ref, y_hbm, gate_ref, x_ref, mod_ref, *rest):
    g_ref = rest[0] if has_norm else None
    o_ref, buf, y_scr, sems = rest[-4:]
    tm = TM_COMBINE
    i = pl.program_id(0)

    def start_gather(tile, slot):
        t0 = tile * tm

        def body(t, carry):
            for kq in range(TOP_K):
                pltpu.make_async_copy(y_hbm.at[dest_ref[kq * t_all + t0 + t]], buf.at[slot, kq * tm + t],
                                      sems.at[slot]).start(priority=kq % 2)
            return carry

        lax.fori_loop(0, tm, body, 0, unroll=ISSUE_UNROLL)

    @pl.when(i == 0)
    def _():
        start_gather(0, 0)

    @pl.when(i + 1 < pl.num_programs(0))
    def _():
        start_gather(i + 1, (i + 1) % 2)

    slot = i % 2
    pltpu.make_async_copy(y_hbm.at[pl.ds(0, TOP_K * tm)], buf.at[slot], sems.at[slot]).wait()
    y_scr[...] = buf[slot].reshape(y_scr.shape)
    gate = gate_ref[...]
    acc = gate[:, 0:1] * y_scr[0:tm, :]
    for kq in range(1, TOP_K):
        acc = acc + gate[:, kq:kq + 1] * y_scr[kq * tm:(kq + 1) * tm, :]
    x = x_ref[...] + mod_ref[5:6, :] * acc
    if has_norm:
        x = x * lax.rsqrt(jnp.mean(x * x, axis=-1, keepdims=True) + RMS_EPS) * g_ref[...]
    o_ref[...] = x


def _combine_call(y_rows, dest_flat, gate_tk, x, mod, row_of_tile, g_final=None):
    t = gate_tk.shape[0]
    d = y_rows.shape[-1]
    tm = TM_COMBINE
    norm_specs, norm_args = [], []
    if g_final is not None:
        norm_specs, norm_args = [pl.BlockSpec((1, d), lambda i, dr: (0, 0))], [g_final.reshape(1, d)]
    grid_spec = pltpu.PrefetchScalarGridSpec(
        num_scalar_prefetch=1,
        grid=(t // tm,),
        in_specs=[pl.BlockSpec(memory_space=pl.ANY), pl.BlockSpec((tm, TOP_K), lambda i, dr: (i, 0)),
                  pl.BlockSpec((tm, d), lambda i, dr: (i, 0)),
                  pl.BlockSpec((None, N_MOD, d), lambda i, dr: (row_of_tile(i, tm), 0, 0))] + norm_specs,
        out_specs=pl.BlockSpec((tm, d), lambda i, dr: (i, 0)),
        scratch_shapes=[pltpu.VMEM((2, TOP_K * tm, 1, d), F32), pltpu.VMEM((TOP_K * tm, d), F32),
                        pltpu.SemaphoreType.DMA((2,))],
    )
    return pl.pallas_call(
        functools.partial(_combine_kernel, t, g_final is not None),
        grid_spec=grid_spec,
        out_shape=jax.ShapeDtypeStruct((t, d), F32),
        compiler_params=_cparams(("arbitrary",)),
        name="moe_combine",
    )(dest_flat, y_rows, gate_tk, x, mod, *norm_args)


def _moe_layer(x_mid, mod, row_of_tile, h3, e_t, gate_t, rank_t, counts, layer, w1, b1, w2, b2, g_final=None):
    t = h3.shape[0]
    ne = w1.shape[1]
    plan, starts = _moe_plan(counts, TOP_K * t)
    start_of = jnp.sum(jnp.where(e_t[..., None] == jnp.arange(ne, dtype=jnp.int32), starts, 0), axis=-1)
    dest_flat = (start_of + rank_t).reshape(-1)
    rows = _dispatch_call(h3, dest_flat)
    y_rows = _moe_call(rows, plan, layer, w1, b1, w2, b2)
    return _combine_call(y_rows, dest_flat, gate_t.T, x_mid, mod, row_of_tile, g_final)


def _permute_cols(w):
    o = _SRC_OFF
    return jnp.concatenate([
        w[..., o["rwkv"]:o["lora"]], w[..., o["conv"]:o["qkv"]], w[..., o["gates"]:],
        w[..., o["qkv"]:o["rwkv"]], w[..., o["lora"]:o["gates"]]], axis=-1)


def kernel(x_prompt, x_sample, cache_k, cache_v, state_rwkv_fwd, state_rwkv_bwd, c, c_ctx, w_ada, b_ada, g_norm1, g_norm2, w_in, conv_w, attn_sink, rwkv_w0, rwkv_w2, rwkv_a0, rwkv_a2, rwkv_g2, rwkv_k_k, rwkv_k_a, rwkv_r_k, rwkv_ln_g, rwkv_ln_b, w_branch_conv, w_branch_attn, w_branch_rwkv, w_out, router_w, router_b, moe_w1, moe_b1, moe_w2, moe_b2, g_final):
    bc, lc, d = x_prompt.shape
    bl, tl, _ = x_sample.shape
    depth = w_in.shape[0]
    n_ctx = bc * lc
    t_all = n_ctx + bl * tl
    tm_in = next(tm for tm in (TM_IN, TM_IN // 2) if n_ctx % tm == 0 and tl % tm == 0)
    assert n_ctx % TM_MIX == 0 and tl % TM_MIX == 0 and lc % (CHUNK * CHUNKS_PER_STEP) == 0 and t_all % TM_ROUTE == 0

    def row_of_tile(i, tm):
        return jnp.where(i < n_ctx // tm, 0, 1 + (i - n_ctx // tm) // (tl // tm))

    xs = (x_prompt.reshape(n_ctx, d), x_sample.reshape(bl * tl, d))
    n_cond = -(-(1 + bl) // 8) * 8
    cond = jnp.zeros((n_cond, d), F32).at[0].set(c_ctx).at[1:1 + bl].set(c)
    mods = _ada_call(cond, w_ada, b_ada).reshape(depth, n_cond, N_MOD, d)

    w_in_p = _permute_cols(w_in).astype(BF16)
    head_id = np.arange(PAIR) // RWKV_HEAD
    ones_pair = jnp.asarray((head_id[:, None] == head_id[None, :]).astype(np.float32), BF16)
    cos_t, sin_t = _rope_tables(tl)
    zeros_state = jnp.zeros((bc, 2 * N_PAIR, RWKV_HEAD, RWKV_HEAD), F32)
    tiles = (n_ctx // TM_MIX, lc, tl)

    new_k, new_v, new_sf, new_sb = [], [], [], []
    for l in range(depth):
        mod = mods[l]
        lw = dict(
            w0=rwkv_w0[l], w2=rwkv_w2[l], a0=rwkv_a0[l], a2=rwkv_a2[l], g2=rwkv_g2[l],
            kk_w=rwkv_k_k[l].reshape(1, -1), ka_w=rwkv_k_a[l].reshape(1, -1), rk_w=rwkv_r_k[l].reshape(1, -1),
            ln_g=rwkv_ln_g[l].reshape(1, -1), ln_b=rwkv_ln_b[l].reshape(1, -1), conv_w=conv_w[l],
            ones_pair=ones_pair,
            wa=w_branch_conv[l].astype(BF16), wb=w_branch_attn[l].astype(BF16), wc=w_branch_rwkv[l].astype(BF16),
            wo=w_out[l].astype(BF16),
        )
        z = _in_call(xs, w_in_p, l, g_norm1[l], mod, row_of_tile, tm_in)

        kv_ctx = z[:n_ctx, COL_QKV + ATT_W:COL_QKV + ATT_W + 2 * KV_W].astype(F32)
        new_k.append(kv_ctx[:, :KV_W].reshape(bc, lc, N_KV_HEADS, HEAD_DIM))
        new_v.append(kv_ctx[:, KV_W:].reshape(bc, lc, N_KV_HEADS, HEAD_DIM))

        ya_c = _attn_ctx_call(z, attn_sink[l], bc, lc)
        ya_l = _attn_lat_call(z, attn_sink[l], cache_k[:, l].reshape(bl, -1, KV_W), cache_v[:, l].reshape(bl, -1, KV_W),
                              cos_t, sin_t, bl, tl, n_ctx)

        of_c, ob_c, zf_c, zb_c = _rwkv_call(z, lw, _state_to_z(zeros_state), _state_to_z(zeros_state), bc, lc, 0)
        of_l, ob_l, _, _ = _rwkv_call(z, lw, _state_to_z(state_rwkv_fwd[:, l].astype(F32)),
                                      _state_to_z(state_rwkv_bwd[:, l].astype(F32)), bl, tl, n_ctx)
        new_sf.append(_z_to_state(zf_c))
        new_sb.append(_z_to_state(zb_c))

        x_mid = _mix_call(xs, z, (of_c, of_l), (ob_c, ob_l), (ya_c, ya_l), mod, row_of_tile, lw, tiles)
        h2, e_t, gate_t, rank_t, cnt = _route_call(x_mid, mod, row_of_tile, g_norm2[l], router_w[l], router_b[l])
        xs = (_moe_layer(x_mid, mod, row_of_tile, h2, e_t, gate_t, rank_t, cnt[:, 0], l, moe_w1, moe_b1, moe_w2, moe_b2,
                         g_final if l == depth - 1 else None),)

    y = xs[0]
    y_prompt = y[:n_ctx].reshape(bc, lc, d)
    y_sample = y[n_ctx:].reshape(bl, tl, d)
    dt = x_prompt.dtype
    return (y_prompt, y_sample, jnp.stack(new_k, axis=1), jnp.stack(new_v, axis=1),
            jnp.stack(new_sf, axis=1).astype(dt), jnp.stack(new_sb, axis=1).astype(dt))
```

```python
import functools

import numpy as np
import jax
import jax.numpy as jnp
from jax import lax
from jax.experimental import pallas as pl
from jax.experimental.pallas import tpu as pltpu

F32 = jnp.float32
BF16 = jnp.bfloat16

D_MODEL = 1024
N_MOD = 6
RMS_EPS = 1e-6
CONV_W = 512
N_HEADS = 8
N_KV_HEADS = 2
GQA_GROUP = N_HEADS // N_KV_HEADS
HEAD_DIM = 64
ATT_W = N_HEADS * HEAD_DIM
KV_W = N_KV_HEADS * HEAD_DIM
WINDOW = 128
Q_BLOCK = 128
ATTN_SCALE = HEAD_DIM ** -0.5
ROPE_THETA = 10000.0
GRID_W = 64
RWKV_HEAD = 64
RWKV_W = 512
DECAY_LORA = 64
ICLR_LORA = 64
GATE_LORA = 128
GN_EPS = 64e-5
N_EXPERTS = 32
TOP_K = 4
D_EXPERT = 1024
SWIGLU_LIMIT = 7.0
SWIGLU_ALPHA = 1.702
P_TOTAL = 7296

_SRC_OFF = dict(conv=0, qkv=1536, rwkv=2304, lora=3840, gates=4224)
COL_RWKV, COL_CONV, COL_GATES, COL_QKV, COL_LORA = 0, 1536, 3072, 6144, 6912
LORA_W = 2 * DECAY_LORA + 2 * ICLR_LORA + GATE_LORA

CHUNK = 64
CHUNKS_PER_STEP = 2
PAIR = 2 * RWKV_HEAD
N_PAIR = RWKV_W // PAIR
NEG_BIG = -1e30
HALO = 16
EXP_NEG_HALF = float(np.exp(-0.5))

TM_IN = 1024
TN_IN = 2432
TM_MIX = 512
TM_MOE = 512
TM_DISPATCH = 1024
TM_COMBINE = 512
ISSUE_UNROLL = 8
VMEM_LIMIT = 56 * 1024 * 1024


def _cparams(sem, vmem=VMEM_LIMIT):
    return pltpu.CompilerParams(dimension_semantics=sem, vmem_limit_bytes=vmem)


def _dot(a, b, dims=(((1,), (0,)), ((), ()))):
    return lax.dot_general(a, b, dims, preferred_element_type=F32)


_NT = (((1,), (1,)), ((), ()))
_TN = (((0,), (0,)), ((), ()))


def _split(x):
    hi = x.astype(BF16)
    lo = (x - hi.astype(F32)).astype(BF16)
    return hi, lo


def _dot3(a, b, dims=(((1,), (0,)), ((), ()))):
    ah, al = _split(a)
    bh, bl = _split(b)
    return _dot(ah, bh, dims) + (_dot(ah, bl, dims) + _dot(al, bh, dims))


def _head_sums(x, ones_pair):
    rows = x.shape[0]
    xs = jnp.concatenate([x[:, PAIR * p:PAIR * (p + 1)] for p in range(N_PAIR)], axis=0).astype(BF16)
    s = _dot(xs, ones_pair)
    return jnp.concatenate([s[rows * p:rows * (p + 1)] for p in range(N_PAIR)], axis=1)


def _sigmoid(x):
    return 0.5 * jnp.tanh(0.5 * x) + 0.5


def _rms_mod(x, g, scale, shift):
    ms = jnp.mean(x * x, axis=-1, keepdims=True)
    return (x * lax.rsqrt(ms + RMS_EPS) * g) * (1.0 + scale) + shift


def _ada_kernel(cond_ref, w_ref, b_ref, o_ref):
    c = cond_ref[...]
    s = c * _sigmoid(c)
    o_ref[...] = _dot(s.astype(BF16), w_ref[...].astype(BF16)) + b_ref[...]


def _ada_call(cond, w_ada, b_ada):
    depth, d, n = w_ada.shape
    r = cond.shape[0]
    tn = 1536
    return pl.pallas_call(
        _ada_kernel,
        grid=(depth, n // tn),
        in_specs=[
            pl.BlockSpec((r, d), lambda l, j: (0, 0)),
            pl.BlockSpec((None, d, tn), lambda l, j: (l, 0, j)),
            pl.BlockSpec((None, 1, tn), lambda l, j: (l, 0, j)),
        ],
        out_specs=pl.BlockSpec((None, r, tn), lambda l, j: (l, 0, j)),
        out_shape=jax.ShapeDtypeStruct((depth, r, n), F32),
        compiler_params=_cparams(("arbitrary", "arbitrary")),
        name="ada",
    )(cond, w_ada, b_ada.reshape(depth, 1, n))


def _in_kernel(n_first, *refs):
    if n_first is None:
        x_ref, mod_ref, g_ref, w_ref, z_ref = refs
        x = x_ref[...]
    else:
        xa_ref, xb_ref, mod_ref, g_ref, w_ref, z_ref = refs
        x = jnp.where(pl.program_id(1) < n_first, xa_ref[...], xb_ref[...])
    h = _rms_mod(x, g_ref[...], mod_ref[1:2, :], mod_ref[0:1, :]).astype(BF16)
    z_ref[...] = _dot(h, w_ref[...]).astype(BF16)


def _in_call(xs, w_bf16, layer, g_norm, mod, row_of_tile, tm):
    d = xs[0].shape[1]
    t = sum(x.shape[0] for x in xs)
    n = w_bf16.shape[2]
    tn = TN_IN
    if len(xs) == 1:
        n_first = None
        x_specs = [pl.BlockSpec((tm, d), lambda j, i: (i, 0))]
    else:
        n_first = xs[0].shape[0] // tm
        x_specs = [pl.BlockSpec((tm, d), lambda j, i: (jnp.minimum(i, n_first - 1), 0)),
                   pl.BlockSpec((tm, d), lambda j, i: (jnp.maximum(i - n_first, 0), 0))]
    return pl.pallas_call(
        functools.partial(_in_kernel, n_first),
        grid=(n // tn, t // tm),
        in_specs=x_specs + [
            pl.BlockSpec((None, N_MOD, d), lambda j, i: (row_of_tile(i, tm), 0, 0)),
            pl.BlockSpec((1, d), lambda j, i: (0, 0)),
            pl.BlockSpec((None, d, tn), lambda j, i: (layer, 0, j)),
        ],
        out_specs=pl.BlockSpec((tm, tn), lambda j, i: (i, j)),
        out_shape=jax.ShapeDtypeStruct((t, n), BF16),
        compiler_params=_cparams(("arbitrary", "arbitrary")),
        name="in_proj",
    )(*xs, mod, g_norm.reshape(1, d), w_bf16)


def _softmax_pv(scores, values, sink):
    m = sink
    for s in scores:
        m = jnp.maximum(m, jnp.max(s, axis=-1, keepdims=True))
    den = jnp.exp(sink - m)
    acc = None
    for s, v in zip(scores, values):
        p = jnp.exp(s - m)
        den = den + jnp.sum(p, axis=-1, keepdims=True)
        pv = _dot(p.astype(BF16), v)
        acc = pv if acc is None else acc + pv
    return acc / den


def _attn_ctx_kernel(sink_ref, q_ref, kv_ref, o_ref):
    q = (q_ref[...] * ATTN_SCALE).astype(BF16)
    kv = kv_ref[...].astype(BF16)
    outs = []
    for h in range(N_HEADS):
        g = h // GQA_GROUP
        qh = q[:, HEAD_DIM * h:HEAD_DIM * (h + 1)]
        kh = kv[:, HEAD_DIM * g:HEAD_DIM * (g + 1)]
        vh = kv[:, KV_W + HEAD_DIM * g:KV_W + HEAD_DIM * (g + 1)]
        outs.append(_softmax_pv([_dot(qh, kh, _NT)], [vh], sink_ref[h]))
    o_ref[...] = jnp.concatenate(outs, axis=-1).astype(BF16)


def _attn_ctx_call(z, sink, n_seq, seq_len):
    return pl.pallas_call(
        _attn_ctx_kernel,
        grid=(n_seq,),
        in_specs=[
            pl.BlockSpec(memory_space=pltpu.SMEM),
            pl.BlockSpec((seq_len, ATT_W), lambda s: (s, COL_QKV // ATT_W)),
            pl.BlockSpec((seq_len, 2 * KV_W), lambda s: (s, (COL_QKV + ATT_W) // (2 * KV_W))),
        ],
        out_specs=pl.BlockSpec((seq_len, ATT_W), lambda s: (s, 0)),
        out_shape=jax.ShapeDtypeStruct((n_seq * seq_len, ATT_W), BF16),
        compiler_params=_cparams(("arbitrary",)),
        name="attn_ctx",
    )(sink, z, z)


def _rope(x, cos, sin_signed):
    n = x.shape[-1]
    lane = lax.broadcasted_iota(jnp.int32, x.shape, 1)
    up = pltpu.roll(x, n - 16, 1)
    dn = pltpu.roll(x, 16, 1)
    partner = jnp.where((lane & 31) < 16, up, dn)
    return x * cos + partner * sin_signed


def _attn_lat_kernel(sink_ref, q_ref, kvp_ref, kvc_ref, kvn_ref, ck_ref, cv_ref,
                     cosc_ref, sinc_ref, cosp_ref, sinp_ref, cosn_ref, sinn_ref, o_ref):
    step = pl.program_id(1)
    n_steps = pl.num_programs(1)
    qb2 = 2 * Q_BLOCK
    cc, sc = cosc_ref[...], sinc_ref[...]
    q = _rope(q_ref[...].astype(F32), jnp.concatenate([cc] * 4, axis=1), jnp.concatenate([sc] * 4, axis=1))
    q = (q * ATTN_SCALE).astype(BF16)
    kvp, kvc, kvn = (ref[...].astype(F32) for ref in (kvp_ref, kvc_ref, kvn_ref))
    kp = _rope(kvp[:, :KV_W], cosp_ref[...], sinp_ref[...]).astype(BF16)
    kc = _rope(kvc[:, :KV_W], cc, sc).astype(BF16)
    kn = _rope(kvn[:, :KV_W], cosn_ref[...], sinn_ref[...]).astype(BF16)
    vp, vc, vn = (t[:, KV_W:].astype(BF16) for t in (kvp, kvc, kvn))
    ck = ck_ref[...].astype(BF16)
    cv = cv_ref[...].astype(BF16)
    k_cat = [jnp.concatenate([kp, kc, ck], axis=0), jnp.concatenate([kc, kn, ck], axis=0)]
    v_cat = [jnp.concatenate([vp, vc, cv], axis=0), jnp.concatenate([vc, vn, cv], axis=0)]
    nk = k_cat[0].shape[0]
    qi = lax.broadcasted_iota(jnp.int32, (Q_BLOCK, nk), 0)
    kj = lax.broadcasted_iota(jnp.int32, (Q_BLOCK, nk), 1)
    in_next = (kj >= qb2) & (kj < 3 * Q_BLOCK)
    before_ok = (step > 0, True)
    after_ok = (True, step < n_steps - 1)
    bias = []
    for u in range(2):
        ok_p = (kj >= qi) & before_ok[u]
        ok_n = (kj - qb2 <= qi) & after_ok[u]
        bu = jnp.where(kj < Q_BLOCK, jnp.where(ok_p, 0.0, NEG_BIG), jnp.where(in_next, jnp.where(ok_n, 0.0, NEG_BIG), 0.0))
        bias.append(jnp.concatenate([bu] * GQA_GROUP, axis=0))
    chains = [(u, g) for u in range(2) for g in range(N_KV_HEADS)]
    heads = {g: range(GQA_GROUP * g, GQA_GROUP * (g + 1)) for g in range(N_KV_HEADS)}
    gsl = {g: slice(HEAD_DIM * g, HEAD_DIM * (g + 1)) for g in range(N_KV_HEADS)}
    q_g = {(u, g): jnp.concatenate([q[Q_BLOCK * u:Q_BLOCK * (u + 1), HEAD_DIM * h:HEAD_DIM * (h + 1)] for h in heads[g]], axis=0)
           for u, g in chains}
    sink = {g: jnp.concatenate([jnp.full((Q_BLOCK, 1), sink_ref[h], F32) for h in heads[g]], axis=0) for g in range(N_KV_HEADS)}
    s = {ch: _dot(q_g[ch], k_cat[ch[0]][:, gsl[ch[1]]], _NT) + bias[ch[0]] for ch in chains}
    m = {ch: jnp.maximum(jnp.max(s[ch], axis=-1, keepdims=True), sink[ch[1]]) for ch in chains}
    p = {ch: jnp.exp(s[ch] - m[ch]) for ch in chains}
    den = {ch: jnp.sum(p[ch], axis=-1, keepdims=True) + jnp.exp(sink[ch[1]] - m[ch]) for ch in chains}
    o = {ch: _dot(p[ch].astype(BF16), v_cat[ch[0]][:, gsl[ch[1]]]) / den[ch] for ch in chains}
    for u in range(2):
        outs = [o[u, g][Q_BLOCK * i:Q_BLOCK * (i + 1)] for g in range(N_KV_HEADS) for i in range(GQA_GROUP)]
        o_ref[Q_BLOCK * u:Q_BLOCK * (u + 1), :] = jnp.concatenate(outs, axis=-1).astype(BF16)


def _attn_lat_call(z, sink, cache_k, cache_v, cos_t, sin_t, n_seq, seq_len, row0):
    nb = seq_len // Q_BLOCK
    ns = nb // 2
    base = row0 // Q_BLOCK
    past = cache_k.shape[1]
    kv_col = (COL_QKV + ATT_W) // (2 * KV_W)

    def kv_spec(off):
        return pl.BlockSpec((Q_BLOCK, 2 * KV_W), lambda b, s: (base + b * nb + jnp.clip(2 * s + off, 0, nb - 1), kv_col))

    def tab_spec(off):
        return pl.BlockSpec((Q_BLOCK, KV_W), lambda b, s: (jnp.clip(2 * s + off, 0, nb - 1), 0))

    return pl.pallas_call(
        _attn_lat_kernel,
        grid=(n_seq, ns),
        in_specs=[
            pl.BlockSpec(memory_space=pltpu.SMEM),
            pl.BlockSpec((2 * Q_BLOCK, ATT_W), lambda b, s: (base // 2 + b * ns + s, COL_QKV // ATT_W)),
            kv_spec(-1),
            pl.BlockSpec((2 * Q_BLOCK, 2 * KV_W), lambda b, s: (base // 2 + b * ns + s, kv_col)),
            kv_spec(2),
            pl.BlockSpec((None, past, KV_W), lambda b, s: (b, 0, 0)),
            pl.BlockSpec((None, past, KV_W), lambda b, s: (b, 0, 0)),
            pl.BlockSpec((2 * Q_BLOCK, KV_W), lambda b, s: (s, 0)), pl.BlockSpec((2 * Q_BLOCK, KV_W), lambda b, s: (s, 0)),
            tab_spec(-1), tab_spec(-1), tab_spec(2), tab_spec(2),
        ],
        out_specs=pl.BlockSpec((2 * Q_BLOCK, ATT_W), lambda b, s: (b * ns + s, 0)),
        out_shape=jax.ShapeDtypeStruct((n_seq * seq_len, ATT_W), BF16),
        compiler_params=_cparams(("arbitrary", "arbitrary")),
        name="attn_lat",
    )(sink, z, z, z, z, cache_k, cache_v, cos_t, sin_t, cos_t, sin_t, cos_t, sin_t)


def _rope_tables(seq_len):
    half = HEAD_DIM // 2
    pos = np.arange(seq_len)
    inv_freq = 1.0 / (ROPE_THETA ** (np.arange(0, half, 2, dtype=np.float32) / half))
    inv_freq = inv_freq.astype(np.float32)

    def part(p):
        ang = (p.astype(np.float32)[:, None] * inv_freq[None, :]).astype(np.float32)
        c, s = np.cos(ang), np.sin(ang)
        return np.concatenate([c, c], axis=1), np.concatenate([-s, s], axis=1)

    c_r, s_r = part(pos // GRID_W)
    c_c, s_c = part(pos % GRID_W)
    cos = np.concatenate([c_r, c_c] * N_KV_HEADS, axis=1).astype(np.float32)
    sin = np.concatenate([s_r, s_c] * N_KV_HEADS, axis=1).astype(np.float32)
    return jnp.asarray(cos), jnp.asarray(sin)


def _rwkv_dir_inputs(zm, zl, d, w0, w2, a0, a2, kk_w, ka_w, ones_pair):
    r = zm[:, :RWKV_W]
    kraw = zm[:, RWKV_W:2 * RWKV_W]
    v = zm[:, 2 * RWKV_W:]
    wl = zl[:, DECAY_LORA * d:DECAY_LORA * (d + 1)]
    al = zl[:, 2 * DECAY_LORA + ICLR_LORA * d:2 * DECAY_LORA + ICLR_LORA * (d + 1)]
    xw = w0 + _dot(jnp.tanh(wl).astype(BF16), w2.astype(BF16))
    ld = -EXP_NEG_HALF * _sigmoid(xw)
    a = _sigmoid(a0 + _dot(al.astype(BF16), a2.astype(BF16)))
    k = kraw * (1.0 + (a - 1.0) * ka_w)
    kkr = kraw * kk_w
    n2 = _head_sums(kkr * kkr, ones_pair)
    kk = kkr / jnp.maximum(jnp.sqrt(n2), 1e-12)
    return r, v, kk, ld, a, k


def _tri_masks(rev):
    c = CHUNK
    ti = lax.broadcasted_iota(jnp.int32, (c, c), 0)
    si = lax.broadcasted_iota(jnp.int32, (c, c), 1)
    incl = (si >= ti) if rev else (si <= ti)
    strict = (si > ti) if rev else (si < ti)
    return incl, strict, (si == ti).astype(F32)


def _chunk_prepare(r, v, kk, ld, a, k, rev):
    incl, _, _ = _tri_masks(rev)
    m_incl = jnp.where(incl, 1.0, 0.0).astype(BF16)
    ldh, ldl = _split(ld)
    cin = _dot(m_incl, ldh) + _dot(m_incl, ldl)
    tot = jnp.sum(ld, axis=0, keepdims=True)
    e_neg = jnp.exp(-cin)
    e_end = jnp.exp(tot - cin)
    bb = kk * a
    return dict(
        a_m=kk * jnp.exp(cin - ld), r_m=r * jnp.exp(cin),
        b_m=(bb * e_neg).astype(BF16), k_m=(k * e_neg).astype(BF16),
        b_end=(bb * e_end).astype(BF16), k_end=(k * e_end).astype(BF16),
        v=v, e_tot=jnp.exp(tot), rev=rev)


def _chunk_problems(probs):
    c = CHUNK
    n = len(probs)
    rng = range(n)
    lane = lax.broadcasted_iota(jnp.int32, (1, PAIR), 1)
    first_head = lane < RWKV_HEAD

    def bd(x):
        xb = x.astype(BF16)
        zero = jnp.zeros_like(xb)
        return jnp.concatenate([jnp.where(first_head, xb, zero), jnp.where(first_head, zero, xb)], axis=0)

    ti = lax.broadcasted_iota(jnp.int32, (c, PAIR), 0)
    si = lax.broadcasted_iota(jnp.int32, (c, PAIR), 1) & (RWKV_HEAD - 1)
    incl = {False: si <= ti, True: si >= ti}
    strict = {False: si < ti, True: si > ti}
    eye = (si == ti).astype(F32)
    rev = [p["rev"] for p in probs]

    lhs = [jnp.concatenate([probs[i]["a_m"], probs[i]["r_m"]], axis=0).astype(BF16) for i in rng]
    xb = [_dot(lhs[i], bd(probs[i]["b_m"]), _NT) for i in rng]
    xk = [_dot(lhs[i], bd(probs[i]["k_m"]), _NT) for i in rng]
    m_ab = [jnp.where(strict[rev[i]], xb[i][:c], 0.0) for i in rng]
    m_ak = [jnp.where(strict[rev[i]], xk[i][:c], 0.0).astype(BF16) for i in rng]
    m_rb = [jnp.where(incl[rev[i]], xb[i][c:], 0.0).astype(BF16) for i in rng]
    m_rk = [jnp.where(incl[rev[i]], xk[i][c:], 0.0).astype(BF16) for i in rng]
    v_bd = [bd(probs[i]["v"]) for i in rng]
    mv = [_dot(jnp.concatenate([m_ak[i], m_rk[i]], axis=0), v_bd[i]) for i in rng]
    mak_v = [mv[i][:c] for i in rng]
    mrk_v = [mv[i][c:] for i in rng]
    t_inv = [eye - m_ab[i] for i in rng]
    lp = [_dot(m_ab[i].astype(BF16), bd(m_ab[i])) for i in rng]
    for _ in range(4):
        both = [_dot(jnp.concatenate([lp[i], t_inv[i]], axis=0).astype(BF16), bd(lp[i])) for i in rng]
        t_inv = [t_inv[i] + both[i][c:] for i in rng]
        lp = [both[i][:c] for i in rng]
    t_inv = [t_inv[i] + _dot(t_inv[i].astype(BF16), bd(lp[i])) for i in rng]
    wu = [_dot(t_inv[i].astype(BF16), jnp.concatenate([bd(probs[i]["a_m"]), bd(mak_v[i])], axis=1)) for i in rng]
    ro = [_dot(m_rb[i], jnp.concatenate([bd(wu[i][:, :PAIR]), bd(wu[i][:, PAIR:])], axis=1)) for i in rng]
    ri = lax.broadcasted_iota(jnp.int32, (PAIR, PAIR), 0)
    ci = lax.broadcasted_iota(jnp.int32, (PAIR, PAIR), 1)
    same_head = (ri < RWKV_HEAD) == (ci < RWKV_HEAD)
    bt_wu = [_dot(probs[i]["b_end"], wu[i].astype(BF16), _TN) for i in rng]
    kt_v = [_dot(probs[i]["k_end"], probs[i]["v"].astype(BF16), _TN) for i in rng]
    local = []
    for i in rng:
        r_eff = probs[i]["r_m"] - ro[i][:, :PAIR]
        o_loc = mrk_v[i] - ro[i][:, PAIR:]
        g_t = jnp.where(same_head, jnp.where(ri == ci, probs[i]["e_tot"], 0.0) - bt_wu[i][:, :PAIR], 0.0)
        h_t = jnp.where(same_head, kt_v[i] - bt_wu[i][:, PAIR:], 0.0)
        local.append((_split(jnp.concatenate([r_eff, g_t], axis=0)), o_loc, h_t))
    return local


def _chunk_apply(local, z):
    (bh, bl), o_loc, h_t = local
    zh, zl = _split(z)
    prod = _dot(bh, zh) + (_dot(bh, zl) + _dot(bl, zh))
    return prod[:CHUNK] + o_loc, prod[CHUNK:] + h_t


def _rwkv_kernel(zmf_ref, zlf_ref, zmb_ref, zlb_ref, w0_ref, w2_ref, a0_ref, a2_ref, kkw_ref, kaw_ref,
                 ones_ref, s0f_ref, s0b_ref, of_ref, ob_ref, zf_ref, zb_ref, zf_scr, zb_scr):
    @pl.when(pl.program_id(1) == 0)
    def _():
        zf_scr[...] = s0f_ref[...]
        zb_scr[...] = s0b_ref[...]

    ones_pair = ones_ref[...]
    dirs = ((zmf_ref, zlf_ref, zf_scr, of_ref), (zmb_ref, zlb_ref, zb_scr, ob_ref))
    n_sub = zmf_ref.shape[0] // CHUNK
    probs = []
    for d, (zm_ref, zl_ref, _, _) in enumerate(dirs):
        r, v, kk, ld, a, k = _rwkv_dir_inputs(zm_ref[...].astype(F32), zl_ref[...].astype(F32), d, w0_ref[d:d + 1, :], w2_ref[d],
                                              a0_ref[d:d + 1, :], a2_ref[d], kkw_ref[...], kaw_ref[...], ones_pair)
        for sub in range(n_sub):
            rows = slice(CHUNK * sub, CHUNK * (sub + 1))
            full = _chunk_prepare(r[rows], v[rows], kk[rows], ld[rows], a[rows], k[rows], d == 1)
            for p in range(N_PAIR):
                ps = slice(PAIR * p, PAIR * (p + 1))
                probs.append({key: (val[:, ps] if hasattr(val, "shape") else val) for key, val in full.items()})
    local = _chunk_problems(probs)
    for d, (_, _, z_scr, o_ref) in enumerate(dirs):
        order = range(n_sub) if d == 0 else range(n_sub - 1, -1, -1)
        for p in range(N_PAIR):
            z = z_scr[p]
            for sub in order:
                out, z = _chunk_apply(local[(d * n_sub + sub) * N_PAIR + p], z)
                o_ref[CHUNK * sub:CHUNK * (sub + 1), PAIR * p:PAIR * (p + 1)] = out
            z_scr[p] = z
    zf_ref[...] = zf_scr[...]
    zb_ref[...] = zb_scr[...]


def _rwkv_call(z, lw, s0f, s0b, n_seq, seq_len, row0):
    rows = CHUNK * CHUNKS_PER_STEP
    nc = seq_len // rows
    base = row0 // rows

    def fwd(s, c):
        return base + s * nc + c

    def bwd(s, c):
        return base + s * nc + (nc - 1 - c)

    def zm_spec(f):
        return pl.BlockSpec((rows, 3 * RWKV_W), lambda s, c: (f(s, c), COL_RWKV // (3 * RWKV_W)))

    def zl_spec(f):
        return pl.BlockSpec((rows, LORA_W), lambda s, c: (f(s, c), COL_LORA // LORA_W))

    def full(shape):
        return pl.BlockSpec(shape, lambda s, c: (0,) * len(shape))

    st_spec = pl.BlockSpec((None, N_PAIR, PAIR, PAIR), lambda s, c: (s, 0, 0, 0))
    o_shape = jax.ShapeDtypeStruct((n_seq * seq_len, RWKV_W), F32)
    st_shape = jax.ShapeDtypeStruct((n_seq, N_PAIR, PAIR, PAIR), F32)
    return pl.pallas_call(
        _rwkv_kernel,
        grid=(n_seq, nc),
        in_specs=[
            zm_spec(fwd), zl_spec(fwd), zm_spec(bwd), zl_spec(bwd),
            full((2, RWKV_W)), full((2, DECAY_LORA, RWKV_W)), full((2, RWKV_W)), full((2, ICLR_LORA, RWKV_W)),
            full((1, RWKV_W)), full((1, RWKV_W)), full((PAIR, PAIR)),
            st_spec, st_spec,
        ],
        out_specs=[
            pl.BlockSpec((rows, RWKV_W), lambda s, c: (s * nc + c, 0)),
            pl.BlockSpec((rows, RWKV_W), lambda s, c: (s * nc + (nc - 1 - c), 0)),
            st_spec, st_spec,
        ],
        out_shape=[o_shape, o_shape, st_shape, st_shape],
        scratch_shapes=[pltpu.VMEM((N_PAIR, PAIR, PAIR), F32), pltpu.VMEM((N_PAIR, PAIR, PAIR), F32)],
        compiler_params=_cparams(("arbitrary", "arbitrary")),
        name="rwkv_scan",
    )(z, z, z, z, lw["w0"], lw["w2"], lw["a0"], lw["a2"], lw["kk_w"], lw["ka_w"], lw["ones_pair"], s0f, s0b)


def _state_to_z(s):
    n = s.shape[0]
    st = jnp.swapaxes(s, -1, -2).reshape(n, N_PAIR, 2, RWKV_HEAD, RWKV_HEAD)
    zero = jnp.zeros_like(st[:, :, 0])
    top = jnp.concatenate([st[:, :, 0], zero], axis=-1)
    bot = jnp.concatenate([zero, st[:, :, 1]], axis=-1)
    return jnp.concatenate([top, bot], axis=-2)


def _z_to_state(z):
    n = z.shape[0]
    h0 = z[:, :, :RWKV_HEAD, :RWKV_HEAD]
    h1 = z[:, :, RWKV_HEAD:, RWKV_HEAD:]
    st = jnp.stack([h0, h1], axis=2).reshape(n, 2 * N_PAIR, RWKV_HEAD, RWKV_HEAD)
    return jnp.swapaxes(st, -1, -2)


def _mix_kernel(tiles, n_x, *refs):
    n_ctx_tiles, len_ctx, len_lat = tiles
    i = pl.program_id(0)
    in_ctx = i < n_ctx_tiles

    def pick(pair):
        return jnp.where(in_ctx, pair[0][...], pair[1][...])

    x_refs, refs = refs[:n_x], refs[n_x:]
    (zc_ref, zcp_ref, zcn_ref, zm_ref, zl_ref, zg_ref, ofc_ref, ofl_ref, obc_ref, obl_ref, yac_ref, yal_ref, mod_ref,
     convw_ref, a0_ref, a2_ref, g2_ref, kaw_ref, rkw_ref, lng_ref, lnb_ref, ones_ref,
     wa_ref, wb_ref, wc_ref, wo_ref, o_ref) = refs
    x_in = x_refs[0][...] if n_x == 1 else pick(x_refs)
    tm = o_ref.shape[0]

    zc = zc_ref[...].astype(F32)
    u = zc[:, 2 * CONV_W:] * zc[:, :CONV_W]
    zp = zcp_ref[...].astype(F32)
    zn = zcn_ref[...].astype(F32)
    row = lax.broadcasted_iota(jnp.int32, (tm, 1), 0)
    pos = jnp.where(in_ctx, lax.rem(i * tm + row, len_ctx), lax.rem((i - n_ctx_tiles) * tm + row, len_lat))
    seq_len = jnp.where(in_ctx, len_ctx, len_lat)
    u_prev = jnp.where(row == 0, zp[HALO - 1:HALO, 2 * CONV_W:] * zp[HALO - 1:HALO, :CONV_W], pltpu.roll(u, 1, 0))
    u_next = jnp.where(row == tm - 1, zn[0:1, 2 * CONV_W:] * zn[0:1, :CONV_W], pltpu.roll(u, tm - 1, 0))
    u_prev = jnp.where(pos == 0, 0.0, u_prev)
    u_next = jnp.where(pos == seq_len - 1, 0.0, u_next)
    cw = convw_ref[...]
    y_conv = zc[:, CONV_W:2 * CONV_W] * (cw[0:1, :] * u_prev + cw[1:2, :] * u + cw[2:3, :] * u_next)

    zm = zm_ref[...].astype(F32)
    zl = zl_ref[...].astype(F32)
    r = zm[:, :RWKV_W]
    kraw = zm[:, RWKV_W:2 * RWKV_W]
    v = zm[:, 2 * RWKV_W:]
    ones_pair = ones_ref[...]
    o = pick((ofc_ref, ofl_ref)) + pick((obc_ref, obl_ref))
    mu = _head_sums(o, ones_pair) * (1.0 / RWKV_HEAD)
    dlt = o - mu
    var = _head_sums(dlt * dlt, ones_pair) * (1.0 / RWKV_HEAD)
    y = dlt * lax.rsqrt(var + GN_EPS) * lng_ref[...] + lnb_ref[...]
    for d in range(2):
        al = zl[:, 2 * DECAY_LORA + ICLR_LORA * d:2 * DECAY_LORA + ICLR_LORA * (d + 1)]
        a = _sigmoid(a0_ref[d:d + 1, :] + _dot(al.astype(BF16), a2_ref[d].astype(BF16)))
        k = kraw * (1.0 + (a - 1.0) * kaw_ref[...])
        y = y + _head_sums(r * k * rkw_ref[...], ones_pair) * v
    g1 = zl[:, 2 * DECAY_LORA + 2 * ICLR_LORA:]
    y_rwkv = y * _dot(_sigmoid(g1).astype(BF16), g2_ref[...].astype(BF16))

    zg = zg_ref[...].astype(F32)
    merged = (_sigmoid(zg[:, :D_MODEL]) * _dot(y_conv.astype(BF16), wa_ref[...])
              + _sigmoid(zg[:, D_MODEL:2 * D_MODEL]) * _dot(pick((yac_ref, yal_ref)), wb_ref[...])
              + _sigmoid(zg[:, 2 * D_MODEL:]) * _dot(y_rwkv.astype(BF16), wc_ref[...]))
    o_ref[...] = x_in + mod_ref[2:3, :] * _dot(merged.astype(BF16), wo_ref[...])


def _mix_call(xs, z, o_f, o_b, y_attn, mod, row_of_tile, lw, tiles):
    t, d = z.shape[0], xs[0].shape[1]
    tm = TM_MIX
    nt = t // tm
    hb = tm // HALO
    n_ctx_tiles = tiles[0]

    def rows(w, col):
        return pl.BlockSpec((tm, w), lambda i: (i, col // w))

    def pair(w):
        return [pl.BlockSpec((tm, w), lambda i: (jnp.minimum(i, n_ctx_tiles - 1), 0)),
                pl.BlockSpec((tm, w), lambda i: (jnp.maximum(i - n_ctx_tiles, 0), 0))]

    def full(shape):
        return pl.BlockSpec(shape, lambda i: (0,) * len(shape))

    in_specs = (pair(d) if len(xs) == 2 else [rows(d, 0)]) + [
        rows(3 * CONV_W, COL_CONV),
        pl.BlockSpec((HALO, 3 * CONV_W), lambda i: (jnp.maximum(i * hb - 1, 0), COL_CONV // (3 * CONV_W))),
        pl.BlockSpec((HALO, 3 * CONV_W), lambda i: (jnp.minimum((i + 1) * hb, nt * hb - 1), COL_CONV // (3 * CONV_W))),
        rows(3 * RWKV_W, COL_RWKV),
        rows(LORA_W, COL_LORA),
        rows(3 * D_MODEL, COL_GATES),
    ] + pair(RWKV_W) + pair(RWKV_W) + pair(ATT_W) + [
        pl.BlockSpec((None, N_MOD, d), lambda i: (row_of_tile(i, tm), 0, 0)),
        full((3, CONV_W)), full((2, RWKV_W)), full((2, ICLR_LORA, RWKV_W)), full((GATE_LORA, RWKV_W)),
        full((1, RWKV_W)), full((1, RWKV_W)), full((1, RWKV_W)), full((1, RWKV_W)),
        full((PAIR, PAIR)),
        full((CONV_W, d)), full((ATT_W, d)), full((RWKV_W, d)), full((d, d)),
    ]
    return pl.pallas_call(
        functools.partial(_mix_kernel, tiles, len(xs)),
        grid=(nt,),
        in_specs=in_specs,
        out_specs=rows(d, 0),
        out_shape=jax.ShapeDtypeStruct((t, d), F32),
        compiler_params=_cparams(("arbitrary",)),
        name="mix",
    )(*xs, z, z, z, z, z, z, *o_f, *o_b, *y_attn, mod,
      lw["conv_w"], lw["a0"], lw["a2"], lw["g2"], lw["ka_w"], lw["rk_w"], lw["ln_g"], lw["ln_b"],
      lw["ones_pair"], lw["wa"], lw["wb"], lw["wc"], lw["wo"])


def _route_kernel(x_ref, mod_ref, g_ref, rwt_ref, rb_ref, tri_ref, h_ref, e_ref, gate_ref, rank_ref, cnt_ref, cnt_scr):
    @pl.when(pl.program_id(0) == 0)
    def _():
        cnt_scr[...] = jnp.zeros_like(cnt_scr)

    h2 = _rms_mod(x_ref[...], g_ref[...], mod_ref[4:5, :], mod_ref[3:4, :])
    h_ref[...] = h2.reshape(h_ref.shape)
    logits = _dot3(rwt_ref[...], h2, _NT) + rb_ref[...]
    ne, tm = logits.shape
    ex = lax.broadcasted_iota(jnp.int32, (ne, tm), 0)
    work = logits
    vals, hots = [], []
    for kq in range(TOP_K):
        m = jnp.max(work, axis=0, keepdims=True)
        idx = jnp.min(jnp.where(work == m, ex, ne), axis=0, keepdims=True)
        hot = ex == idx
        vals.append(m)
        hots.append(hot)
        e_ref[kq:kq + 1, :] = idx
        work = jnp.where(hot, -jnp.inf, work)
    exps = [jnp.exp(vk - vals[0]) for vk in vals]
    den = exps[0] + exps[1] + exps[2] + exps[3]
    chosen = jnp.where(hots[0] | hots[1] | hots[2] | hots[3], 1.0, 0.0)
    before = cnt_scr[:, 0:1] + _dot(chosen.astype(BF16), tri_ref[...])
    for kq in range(TOP_K):
        gate_ref[kq:kq + 1, :] = exps[kq] / den
        rank_ref[kq:kq + 1, :] = jnp.sum(jnp.where(hots[kq], before, 0.0), axis=0, keepdims=True).astype(jnp.int32)
    cnt_scr[...] = cnt_scr[...] + jnp.sum(chosen, axis=1, keepdims=True)
    cnt_ref[...] = cnt_scr[...].astype(jnp.int32)


def _route_call(x, mod, row_of_tile, g_norm, router_w, router_b, tm):
    t, d = x.shape
    ne = router_w.shape[1]
    tri = jnp.triu(jnp.ones((tm, tm), F32), 1).astype(BF16)

    def full(shape):
        return pl.BlockSpec(shape, lambda i: (0,) * len(shape))

    kt_spec = pl.BlockSpec((TOP_K, tm), lambda i: (0, i))
    return pl.pallas_call(
        _route_kernel,
        grid=(t // tm,),
        in_specs=[
            pl.BlockSpec((tm, d), lambda i: (i, 0)),
            pl.BlockSpec((None, N_MOD, d), lambda i: (row_of_tile(i, tm), 0, 0)),
            full((1, d)), full((ne, d)), full((ne, 1)), full((tm, tm)),
        ],
        out_specs=[pl.BlockSpec((tm, 1, d), lambda i: (i, 0, 0)), kt_spec, kt_spec, kt_spec, full((ne, 128))],
        out_shape=[
            jax.ShapeDtypeStruct((t, 1, d), F32),
            jax.ShapeDtypeStruct((TOP_K, t), jnp.int32),
            jax.ShapeDtypeStruct((TOP_K, t), F32),
            jax.ShapeDtypeStruct((TOP_K, t), jnp.int32),
            jax.ShapeDtypeStruct((ne, 128), jnp.int32),
        ],
        scratch_shapes=[pltpu.VMEM((ne, 128), F32)],
        compiler_params=_cparams(("arbitrary",)),
        name="route",
    )(x, mod, g_norm.reshape(1, d), router_w.T, router_b.reshape(ne, 1), tri)


def _row_copy_wait(buf_hbm, n_rows, sem):
    view = buf_hbm.at[pl.ds(0, n_rows)]
    pltpu.make_async_copy(view, view, sem).wait()


def _dispatch_kernel(t_all, dest_ref, h_ref, rows_hbm, sem):
    tm = h_ref.shape[0]
    t0 = pl.program_id(0) * tm

    def body(t, carry):
        for kq in range(TOP_K):
            pltpu.make_async_copy(h_ref.at[t], rows_hbm.at[dest_ref[kq * t_all + t0 + t]], sem).start(priority=kq % 2)
        return carry

    lax.fori_loop(0, tm, body, 0, unroll=ISSUE_UNROLL)
    _row_copy_wait(rows_hbm, TOP_K * tm, sem)


def _dispatch_call(h3, dest_flat):
    t, _, d = h3.shape
    tm = TM_DISPATCH if t % TM_DISPATCH == 0 else TM_DISPATCH // 2
    grid_spec = pltpu.PrefetchScalarGridSpec(
        num_scalar_prefetch=1,
        grid=(t // tm,),
        in_specs=[pl.BlockSpec((tm, 1, d), lambda i, dr: (i, 0, 0))],
        out_specs=pl.BlockSpec(memory_space=pl.ANY),
        scratch_shapes=[pltpu.SemaphoreType.DMA(())],
    )
    return pl.pallas_call(
        functools.partial(_dispatch_kernel, t),
        grid_spec=grid_spec,
        out_shape=jax.ShapeDtypeStruct((TOP_K * t, 1, d), F32),
        compiler_params=_cparams(("arbitrary",)),
        name="moe_dispatch",
    )(dest_flat, h3)


def _moe_kernel(blk_ref, exp_ref, lo_ref, hi_ref, fblk_ref, fexp_ref, nit_ref,
                x_ref, w1_ref, b1_ref, w2_ref, b2_ref, o_ref, w1_scr, w2_scr, x_scr, acc_scr):
    del blk_ref, exp_ref
    it = pl.program_id(0)

    @pl.when(fexp_ref[it] == 1)
    def _():
        w1_scr[...] = w1_ref[...].astype(BF16)
        w2_scr[...] = w2_ref[...].astype(BF16)

    @pl.when(it < nit_ref[0])
    def _():
        x_scr[...] = x_ref[...].reshape(x_scr.shape)
        hm = _dot(x_scr[...].astype(BF16), w1_scr[...]) + b1_ref[...]
        glu = jnp.minimum(hm[:, :D_EXPERT], SWIGLU_LIMIT)
        lin = jnp.clip(hm[:, D_EXPERT:], -SWIGLU_LIMIT, SWIGLU_LIMIT)
        act = glu * _sigmoid(SWIGLU_ALPHA * glu) * (lin + 1.0)
        y = _dot(act.astype(BF16), w2_scr[...]) + b2_ref[...]
        row = lax.broadcasted_iota(jnp.int32, (y.shape[0], 1), 0)
        y = jnp.where((row >= lo_ref[it]) & (row < hi_ref[it]), y, 0.0)

        @pl.when(fblk_ref[it] == 1)
        def _():
            o_ref[...] = y.reshape(o_ref.shape)

        @pl.when(fblk_ref[it] == 0)
        def _():
            acc_scr[...] = o_ref[...].reshape(acc_scr.shape)
            o_ref[...] = (acc_scr[...] + y).reshape(o_ref.shape)


def _moe_call(rows, plan, layer, w1, b1, w2, b2):
    n_rows, _, d = rows.shape
    tm = TM_MOE
    depth, ne, _, dh2 = w1.shape
    de = w2.shape[2]
    n_items = plan[0].shape[0]

    def row_map(it, blk, ex, lo, hi, fb, fe, nit):
        return (blk[it], 0, 0)

    def w_map(it, blk, ex, lo, hi, fb, fe, nit):
        return (layer, ex[it], 0, 0)

    grid_spec = pltpu.PrefetchScalarGridSpec(
        num_scalar_prefetch=7,
        grid=(n_items,),
        in_specs=[
            pl.BlockSpec((tm, 1, d), row_map),
            pl.BlockSpec((None, None, d, dh2), w_map),
            pl.BlockSpec((None, None, 1, dh2), w_map),
            pl.BlockSpec((None, None, de, d), w_map),
            pl.BlockSpec((None, None, 1, d), w_map),
        ],
        out_specs=pl.BlockSpec((tm, 1, d), row_map),
        scratch_shapes=[pltpu.VMEM((d, dh2), BF16), pltpu.VMEM((de, d), BF16), pltpu.VMEM((tm, d), F32),
                        pltpu.VMEM((tm, d), F32)],
    )
    return pl.pallas_call(
        _moe_kernel,
        grid_spec=grid_spec,
        out_shape=jax.ShapeDtypeStruct((n_rows, 1, d), F32),
        compiler_params=_cparams(("arbitrary",)),
        name="moe",
    )(*plan, rows, w1, b1.reshape(depth, ne, 1, dh2), w2, b2.reshape(depth, ne, 1, d))


def _moe_plan(counts, n_rows):
    tm = TM_MOE
    ne = counts.shape[0]
    n_items = n_rows // tm + ne - 1
    counts = counts.astype(jnp.int32)
    ends = jnp.cumsum(counts)
    starts = ends - counts
    first_blk = starts // tm
    last_blk = jnp.maximum(ends - 1, 0) // tm
    per_exp = jnp.where(counts > 0, last_blk - first_blk + 1, 0)
    item_end = jnp.cumsum(per_exp)
    item_off = item_end - per_exp
    total = item_end[-1]
    it = jnp.arange(n_items, dtype=jnp.int32)
    itc = jnp.minimum(it, total - 1)
    ex = jnp.minimum(jnp.sum((itc[:, None] >= item_end[None, :]).astype(jnp.int32), axis=1), ne - 1)
    hot = ex[:, None] == jnp.arange(ne, dtype=jnp.int32)[None, :]

    def pick(v):
        return jnp.sum(jnp.where(hot, v[None, :], 0), axis=1)

    blk = pick(first_blk) + itc - pick(item_off)
    lo = jnp.maximum(pick(starts), blk * tm) - blk * tm
    hi = jnp.where(it < total, jnp.minimum(pick(ends), (blk + 1) * tm) - blk * tm, lo)
    one = jnp.ones((1,), jnp.int32)
    f_blk = jnp.concatenate([one, (blk[1:] != blk[:-1]).astype(jnp.int32)])
    f_exp = jnp.concatenate([one, (ex[1:] != ex[:-1]).astype(jnp.int32)])
    return (blk, ex, lo, hi, f_blk, f_exp, total.reshape(1)), starts


def _combine_kernel(t_all, has_norm, dest_ref, y_hbm, gate_ref, x_ref, mod_ref, *rest):
    g_ref = rest[0] if has_norm else None
    o_ref, buf, y_scr, sems = rest[-4:]
    tm = TM_COMBINE
    i = pl.program_id(0)

    def start_gather(tile, slot):
        t0 = tile * tm

        def body(t, carry):
            for kq in range(TOP_K):
                pltpu.make_async_copy(y_hbm.at[dest_ref[kq * t_all + t0 + t]], buf.at[slot, kq * tm + t],
                                      sems.at[slot]).start(priority=kq % 2)
            return carry

        lax.fori_loop(0, tm, body, 0, unroll=ISSUE_UNROLL)

    @pl.when(i == 0)
    def _():
        start_gather(0, 0)

    @pl.when(i + 1 < pl.num_programs(0))
    def _():
        start_gather(i + 1, (i + 1) % 2)

    slot = i % 2
    pltpu.make_async_copy(y_hbm.at[pl.ds(0, TOP_K * tm)], buf.at[slot], sems.at[slot]).wait()
    y_scr[...] = buf[slot].reshape(y_scr.shape)
    gate = gate_ref[...]
    acc = gate[:, 0:1] * y_scr[0:tm, :]
    for kq in range(1, TOP_K):
        acc = acc + gate[:, kq:kq + 1] * y_scr[kq * tm:(kq + 1) * tm, :]
    x = x_ref[...] + mod_ref[5:6, :] * acc
    if has_norm:
        x = x * lax.rsqrt(jnp.mean(x * x, axis=-1, keepdims=True) + RMS_EPS) * g_ref[...]
    o_ref[...] = x


def _combine_call(y_rows, dest_flat, gate_tk, x, mod, row_of_tile, g_final=None):
    t = gate_tk.shape[0]
    d = y_rows.shape[-1]
    tm = TM_COMBINE
    norm_specs, norm_args = [], []
    if g_final is not None:
        norm_specs, norm_args = [pl.BlockSpec((1, d), lambda i, dr: (0, 0))], [g_final.reshape(1, d)]
    grid_spec = pltpu.PrefetchScalarGridSpec(
        num_scalar_prefetch=1,
        grid=(t // tm,),
        in_specs=[pl.BlockSpec(memory_space=pl.ANY), pl.BlockSpec((tm, TOP_K), lambda i, dr: (i, 0)),
                  pl.BlockSpec((tm, d), lambda i, dr: (i, 0)),
                  pl.BlockSpec((None, N_MOD, d), lambda i, dr: (row_of_tile(i, tm), 0, 0))] + norm_specs,
        out_specs=pl.BlockSpec((tm, d), lambda i, dr: (i, 0)),
        scratch_shapes=[pltpu.VMEM((2, TOP_K * tm, 1, d), F32), pltpu.VMEM((TOP_K * tm, d), F32),
                        pltpu.SemaphoreType.DMA((2,))],
    )
    return pl.pallas_call(
        functools.partial(_combine_kernel, t, g_final is not None),
        grid_spec=grid_spec,
        out_shape=jax.ShapeDtypeStruct((t, d), F32),
        compiler_params=_cparams(("arbitrary",)),
        name="moe_combine",
    )(dest_flat, y_rows, gate_tk, x, mod, *norm_args)


def _moe_layer(x_mid, mod, row_of_tile, h3, e_t, gate_t, rank_t, counts, layer, w1, b1, w2, b2, g_final=None):
    t = h3.shape[0]
    ne = w1.shape[1]
    plan, starts = _moe_plan(counts, TOP_K * t)
    start_of = jnp.sum(jnp.where(e_t[..., None] == jnp.arange(ne, dtype=jnp.int32), starts, 0), axis=-1)
    dest_flat = (start_of + rank_t).reshape(-1)
    rows = _dispatch_call(h3, dest_flat)
    y_rows = _moe_call(rows, plan, layer, w1, b1, w2, b2)
    return _combine_call(y_rows, dest_flat, gate_t.T, x_mid, mod, row_of_tile, g_final)


def _permute_cols(w):
    o = _SRC_OFF
    return jnp.concatenate([
        w[..., o["rwkv"]:o["lora"]], w[..., o["conv"]:o["qkv"]], w[..., o["gates"]:],
        w[..., o["qkv"]:o["rwkv"]], w[..., o["lora"]:o["gates"]]], axis=-1)


def kernel(x_prompt, x_sample, cache_k, cache_v, state_rwkv_fwd, state_rwkv_bwd, c, c_ctx, w_ada, b_ada, g_norm1, g_norm2, w_in, conv_w, attn_sink, rwkv_w0, rwkv_w2, rwkv_a0, rwkv_a2, rwkv_g2, rwkv_k_k, rwkv_k_a, rwkv_r_k, rwkv_ln_g, rwkv_ln_b, w_branch_conv, w_branch_attn, w_branch_rwkv, w_out, router_w, router_b, moe_w1, moe_b1, moe_w2, moe_b2, g_final):
    bc, lc, d = x_prompt.shape
    bl, tl, _ = x_sample.shape
    depth = w_in.shape[0]
    n_ctx = bc * lc
    t_all = n_ctx + bl * tl
    tm_in = next(tm for tm in (TM_IN, TM_IN // 2) if n_ctx % tm == 0 and tl % tm == 0)
    assert n_ctx % TM_MIX == 0 and tl % TM_MIX == 0 and lc % (CHUNK * CHUNKS_PER_STEP) == 0 and t_all % TM_COMBINE == 0

    def row_of_tile(i, tm):
        return jnp.where(i < n_ctx // tm, 0, 1 + (i - n_ctx // tm) // (tl // tm))

    xs = (x_prompt.reshape(n_ctx, d), x_sample.reshape(bl * tl, d))
    n_cond = -(-(1 + bl) // 8) * 8
    cond = jnp.zeros((n_cond, d), F32).at[0].set(c_ctx).at[1:1 + bl].set(c)
    mods = _ada_call(cond, w_ada, b_ada).reshape(depth, n_cond, N_MOD, d)

    w_in_p = _permute_cols(w_in).astype(BF16)
    head_id = np.arange(PAIR) // RWKV_HEAD
    ones_pair = jnp.asarray((head_id[:, None] == head_id[None, :]).astype(np.float32), BF16)
    cos_t, sin_t = _rope_tables(tl)
    zeros_state = jnp.zeros((bc, 2 * N_PAIR, RWKV_HEAD, RWKV_HEAD), F32)
    tiles = (n_ctx // TM_MIX, lc, tl)

    new_k, new_v, new_sf, new_sb = [], [], [], []
    for l in range(depth):
        mod = mods[l]
        lw = dict(
            w0=rwkv_w0[l], w2=rwkv_w2[l], a0=rwkv_a0[l], a2=rwkv_a2[l], g2=rwkv_g2[l],
            kk_w=rwkv_k_k[l].reshape(1, -1), ka_w=rwkv_k_a[l].reshape(1, -1), rk_w=rwkv_r_k[l].reshape(1, -1),
            ln_g=rwkv_ln_g[l].reshape(1, -1), ln_b=rwkv_ln_b[l].reshape(1, -1), conv_w=conv_w[l],
            ones_pair=ones_pair,
            wa=w_branch_conv[l].astype(BF16), wb=w_branch_attn[l].astype(BF16), wc=w_branch_rwkv[l].astype(BF16),
            wo=w_out[l].astype(BF16),
        )
        z = _in_call(xs, w_in_p, l, g_norm1[l], mod, row_of_tile, tm_in)

        kv_ctx = z[:n_ctx, COL_QKV + ATT_W:COL_QKV + ATT_W + 2 * KV_W].astype(F32)
        new_k.append(kv_ctx[:, :KV_W].reshape(bc, lc, N_KV_HEADS, HEAD_DIM))
        new_v.append(kv_ctx[:, KV_W:].reshape(bc, lc, N_KV_HEADS, HEAD_DIM))

        ya_c = _attn_ctx_call(z, attn_sink[l], bc, lc)
        ya_l = _attn_lat_call(z, attn_sink[l], cache_k[:, l].reshape(bl, -1, KV_W), cache_v[:, l].reshape(bl, -1, KV_W),
                              cos_t, sin_t, bl, tl, n_ctx)

        of_c, ob_c, zf_c, zb_c = _rwkv_call(z, lw, _state_to_z(zeros_state), _state_to_z(zeros_state), bc, lc, 0)
        of_l, ob_l, _, _ = _rwkv_call(z, lw, _state_to_z(state_rwkv_fwd[:, l].astype(F32)),
                                      _state_to_z(state_rwkv_bwd[:, l].astype(F32)), bl, tl, n_ctx)
        new_sf.append(_z_to_state(zf_c))
        new_sb.append(_z_to_state(zb_c))

        x_mid = _mix_call(xs, z, (of_c, of_l), (ob_c, ob_l), (ya_c, ya_l), mod, row_of_tile, lw, tiles)
        h2, e_t, gate_t, rank_t, cnt = _route_call(x_mid, mod, row_of_tile, g_norm2[l], router_w[l], router_b[l], tm_in)
        xs = (_moe_layer(x_mid, mod, row_of_tile, h2, e_t, gate_t, rank_t, cnt[:, 0], l, moe_w1, moe_b1, moe_w2, moe_b2,
                         g_final if l == depth - 1 else None),)

    y = xs[0]
    y_prompt = y[:n_ctx].reshape(bc, lc, d)
    y_sample = y[n_ctx:].reshape(bl, tl, d)
    dt = x_prompt.dtype
    return (y_prompt, y_sample, jnp.stack(new_k, axis=1), jnp.stack(new_v, axis=1),
            jnp.stack(new_sf, axis=1).astype(dt), jnp.stack(new_sb, axis=1).astype(dt))
```
